```python
import math
import jax, jax.numpy as jnp
from jax import lax
import numpy as np

D_MODEL = 1024
BATCH = 8
SEQ = 2048
DEPTH = 2
DEC_BATCH = 128
DEC_SEQ = 4
PAST_LEN = 2048
PAGE_SIZE = 128

GDN_HEADS = 4
GDN_HEAD_DIM = 128
GDN_WIDTH = GDN_HEADS * GDN_HEAD_DIM
GDN_CONV = 4
GDN_CHUNK = 64
SC_WIDTH = 256
SC_CONV = 3
MOBA_HEADS = 4
MOBA_HEAD_DIM = 64
MOBA_WIDTH = MOBA_HEADS * MOBA_HEAD_DIM
MOBA_BLOCK = 256
MOBA_TOPK = 3
MOBA_QCHUNK = 64
MIX_WIDTH = GDN_WIDTH + SC_WIDTH + MOBA_WIDTH
PROJ_WIDTH = 4 * GDN_WIDTH + 2 * GDN_HEADS + 3 * SC_WIDTH + 3 * MOBA_WIDTH
D_FF = 2816
NORM_EPS = 1e-6
NEG_INF = -1e30

kernel_name = 'hymba_gdn_shortconv_moba_macaron_step'


def rms_norm(x, w):
    xf = x.astype(jnp.float32)
    y = xf * lax.rsqrt(jnp.mean(xf * xf, axis=-1, keepdims=True) + NORM_EPS)
    return (y * w.astype(jnp.float32)).astype(x.dtype)


def swiglu(x, w_gate, w_up, w_down):
    return (jax.nn.silu(x @ w_gate) * (x @ w_up)) @ w_down


def causal_conv(x, hist, w):
    width = w.shape[0]
    seq_len = x.shape[1]
    xp = jnp.concatenate([hist.astype(x.dtype), x], axis=1)
    y = xp[:, 0:seq_len] * w[0]
    for i in range(1, width):
        y = y + xp[:, i:i + seq_len] * w[i]
    return y, xp[:, xp.shape[1] - (width - 1):]


def l2_normalize(x):
    return x * lax.rsqrt(jnp.sum(x * x, axis=-1, keepdims=True) + NORM_EPS)


def split_projection(h):
    sizes = (GDN_WIDTH, GDN_WIDTH, GDN_WIDTH, GDN_WIDTH, GDN_HEADS, GDN_HEADS,
             SC_WIDTH, SC_WIDTH, SC_WIDTH, MOBA_WIDTH, MOBA_WIDTH, MOBA_WIDTH)
    outs = []
    off = 0
    for s in sizes:
        outs.append(h[..., off:off + s])
        off += s
    return outs


def gated_delta_rule(q, k, v, beta, g, s0):
    bsz, seq_len, n_heads, dk = q.shape
    cs = min(GDN_CHUNK, seq_len)
    n_chunks = -(-seq_len // cs)
    pad = n_chunks * cs - seq_len
    if pad:
        pw = ((0, 0), (0, pad), (0, 0), (0, 0))
        q, k, v = jnp.pad(q, pw), jnp.pad(k, pw), jnp.pad(v, pw)
        beta, g = jnp.pad(beta, pw[:3]), jnp.pad(g, pw[:3])

    def chunks(t):
        return t.reshape(bsz, n_chunks, cs, n_heads, -1).transpose(1, 0, 3, 2, 4)

    qc, kc, vc = chunks(q), chunks(k), chunks(v)
    bc = chunks(beta[..., None])[..., 0]
    gc = jnp.cumsum(chunks(g[..., None])[..., 0], axis=-1)
    idx = jnp.arange(cs)
    incl = idx[:, None] >= idx[None, :]
    strict = idx[:, None] > idx[None, :]
    diff = gc[..., :, None] - gc[..., None, :]
    decay = jnp.where(incl, jnp.exp(jnp.where(incl, diff, 0.0)), 0.0)
    kk = jnp.einsum('nbhid,nbhjd->nbhij', kc, kc)
    lower = jnp.where(strict, bc[..., :, None] * kk * decay, 0.0)
    unit_lower = lower + jnp.eye(cs, dtype=lower.dtype)
    rhs = jnp.concatenate([kc * (bc * jnp.exp(gc))[..., None], vc * bc[..., None]], axis=-1)
    sol = lax.linalg.triangular_solve(unit_lower, rhs, left_side=True, lower=True,
                                      unit_diagonal=True)
    w_k, u_v = sol[..., :dk], sol[..., dk:]
    attn = jnp.einsum('nbhid,nbhjd->nbhij', qc, kc) * decay
    q_decayed = qc * jnp.exp(gc)[..., None]
    k_tail = kc * jnp.exp(gc[..., -1:] - gc)[..., None]
    chunk_decay = jnp.exp(gc[..., -1])[..., None, None]

    def step(s, xs):
        w_i, u_i, a_i, qd_i, kt_i, cd_i = xs
        u = u_i - jnp.einsum('bhid,bhde->bhie', w_i, s)
        o = jnp.einsum('bhid,bhde->bhie', qd_i, s) + jnp.einsum('bhij,bhje->bhie', a_i, u)
        s = s * cd_i + jnp.einsum('bhid,bhie->bhde', kt_i, u)
        return s, o

    s_fin, o = lax.scan(step, s0, (w_k, u_v, attn, q_decayed, k_tail, chunk_decay))
    o = o.transpose(1, 0, 3, 2, 4).reshape(bsz, n_chunks * cs, n_heads, -1)[:, :seq_len]
    return o, s_fin


def gdn_mixer(q_in, k_in, v_in, z, b, a, conv_hist, s0, conv_w, a_log, dt_bias, norm_w):
    bsz, seq_len, _ = q_in.shape
    qkv, hist_new = causal_conv(jnp.concatenate([q_in, k_in, v_in], axis=-1), conv_hist, conv_w)
    qkv = jax.nn.silu(qkv.astype(jnp.float32))
    q, k, v = jnp.split(qkv, 3, axis=-1)
    shp = (bsz, seq_len, GDN_HEADS, GDN_HEAD_DIM)
    q = l2_normalize(q.reshape(shp)) * (GDN_HEAD_DIM ** -0.5)
    k = l2_normalize(k.reshape(shp))
    v = v.reshape(shp)
    beta = jax.nn.sigmoid(b.astype(jnp.float32))
    g = -jnp.exp(a_log.astype(jnp.float32)) * jax.nn.softplus(
        a.astype(jnp.float32) + dt_bias.astype(jnp.float32))
    o, s_new = gated_delta_rule(q, k, v, beta, g, s0.astype(jnp.float32))
    o = rms_norm(o, norm_w) * jax.nn.silu(z.reshape(shp).astype(jnp.float32))
    return o.reshape(bsz, seq_len, GDN_WIDTH).astype(q_in.dtype), hist_new, s_new.astype(s0.dtype)


def short_conv_mixer(h, c_gate, b_gate, hist, conv_w):
    y, hist_new = causal_conv(c_gate * h, hist, conv_w)
    return b_gate * y, hist_new


def moba_attention(q, k_full, v_full, q_pos0):
    bsz, q_len, n_heads, hd = q.shape
    k_len = k_full.shape[1]
    qf = q.astype(jnp.float32).transpose(0, 2, 1, 3) * (hd ** -0.5)
    kf = k_full.astype(jnp.float32)
    vf = v_full.astype(jnp.float32)
    n_blocks = max(-(-k_len // MOBA_BLOCK), MOBA_TOPK)
    tail = n_blocks * MOBA_BLOCK - k_len

    def blocks(t):
        t = jnp.pad(t, ((0, 0), (0, tail), (0, 0), (0, 0)))
        return t.reshape(bsz, n_blocks, MOBA_BLOCK, n_heads, hd).transpose(0, 3, 1, 2, 4)

    kb, vb = blocks(kf), blocks(vf)
    k_mean = jnp.mean(kb, axis=3)

    def window_src(t):
        return jnp.pad(t, ((0, 0), (MOBA_BLOCK - 1, 0), (0, 0), (0, 0))).transpose(0, 2, 1, 3)

    kw_src, vw_src = window_src(kf), window_src(vf)
    qc = MOBA_QCHUNK if q_len % MOBA_QCHUNK == 0 else q_len
    n_qc = q_len // qc
    win = MOBA_BLOCK + qc - 1
    q_chunks = qf.reshape(bsz, n_heads, n_qc, qc, hd).transpose(2, 0, 1, 3, 4)
    starts = q_pos0 + jnp.arange(n_qc, dtype=jnp.int32) * qc
    b_idx = jnp.arange(bsz)[:, None, None, None]
    h_idx = jnp.arange(n_heads)[None, :, None, None]
    blk_ids = jnp.arange(n_blocks)
    rank = jnp.arange(MOBA_TOPK)

    def one_chunk(args):
        qi, s = args
        t = s + jnp.arange(qc)
        bt = t // MOBA_BLOCK
        gate = jnp.einsum('bhqd,bhnd->bhqn', qi, k_mean)
        gate = jnp.where(blk_ids[None, :] < bt[:, None], gate, NEG_INF)
        _, sel = lax.top_k(gate, MOBA_TOPK)
        valid = rank[None, :] < bt[:, None]
        ks = kb[b_idx, h_idx, sel]
        vs = vb[b_idx, h_idx, sel]
        s_sel = jnp.einsum('bhqd,bhqnkd->bhqnk', qi, ks)
        s_sel = jnp.where(valid[:, :, None], s_sel, NEG_INF)
        s_sel = s_sel.reshape(bsz, n_heads, qc, MOBA_TOPK * MOBA_BLOCK)
        kw = lax.dynamic_slice_in_dim(kw_src, s, win, axis=2)
        vw = lax.dynamic_slice_in_dim(vw_src, s, win, axis=2)
        pos = s - (MOBA_BLOCK - 1) + jnp.arange(win)
        own = (pos[None, :] >= (bt * MOBA_BLOCK)[:, None]) & (pos[None, :] <= t[:, None])
        s_own = jnp.where(own, jnp.einsum('bhqd,bhkd->bhqk', qi, kw), NEG_INF)
        p = jax.nn.softmax(jnp.concatenate([s_sel, s_own], axis=-1), axis=-1)
        p_sel = p[..., :MOBA_TOPK * MOBA_BLOCK].reshape(bsz, n_heads, qc, MOBA_TOPK, MOBA_BLOCK)
        p_own = p[..., MOBA_TOPK * MOBA_BLOCK:]
        return (jnp.einsum('bhqnk,bhqnkd->bhqd', p_sel, vs)
                + jnp.einsum('bhqk,bhkd->bhqd', p_own, vw))

    out = lax.map(one_chunk, (q_chunks, starts))
    return out.transpose(1, 0, 3, 2, 4).reshape(bsz, q_len, n_heads, hd).astype(q.dtype)


def decoder_layer(x, gdn_s0, gdn_hist, sc_hist, k_past, v_past,
                  n_ffn1, ffn1_w_gate, ffn1_w_up, ffn1_w_down, n_mix, w_in,
                  gdn_conv_w, gdn_a_log, gdn_dt_bias, gdn_norm_w, sc_conv_w, w_out,
                  n_ffn2, ffn2_w_gate, ffn2_w_up, ffn2_w_down):
    bsz, seq_len, _ = x.shape
    x = x + 0.5 * swiglu(rms_norm(x, n_ffn1), ffn1_w_gate, ffn1_w_up, ffn1_w_down)
    h = rms_norm(x, n_mix) @ w_in
    q_a, k_a, v_a, z_a, b_a, a_a, h_b, c_b, g_b, q_c, k_c, v_c = split_projection(h)
    o_a, gdn_hist_new, gdn_s_new = gdn_mixer(q_a, k_a, v_a, z_a, b_a, a_a, gdn_hist, gdn_s0,
                                             gdn_conv_w, gdn_a_log, gdn_dt_bias, gdn_norm_w)
    o_b, sc_hist_new = short_conv_mixer(h_b, c_b, g_b, sc_hist, sc_conv_w)
    mshp = (bsz, seq_len, MOBA_HEADS, MOBA_HEAD_DIM)
    q_c, k_c, v_c = q_c.reshape(mshp), k_c.reshape(mshp), v_c.reshape(mshp)
    k_full = jnp.concatenate([k_past.astype(x.dtype), k_c], axis=1)
    v_full = jnp.concatenate([v_past.astype(x.dtype), v_c], axis=1)
    o_c = moba_attention(q_c, k_full, v_full, k_past.shape[1]).reshape(bsz, seq_len, MOBA_WIDTH)
    x = x + jnp.concatenate([o_a, o_b, o_c], axis=-1) @ w_out
    x = x + 0.5 * swiglu(rms_norm(x, n_ffn2), ffn2_w_gate, ffn2_w_up, ffn2_w_down)
    return x, gdn_s_new, gdn_hist_new, sc_hist_new, k_c, v_c


def trunk(x, layer_states, layer_params):
    new_states = []
    for l in range(DEPTH):
        out = decoder_layer(x, *layer_states[l], *[p[l] for p in layer_params])
        x = out[0]
        new_states.append(out[1:])
    stacked = [jnp.stack([st[i] for st in new_states]) for i in range(5)]
    return x, stacked


def setup_inputs(seed: int = 0) -> dict:
    key = jax.random.key(seed)
    ks = jax.random.split(key, 32)
    n_pages = PAST_LEN // PAGE_SIZE
    n_pool = (5 * DEC_BATCH * n_pages) // 4

    def nrm(k, shape, scale):
        return jax.random.normal(k, shape, jnp.float32) * scale

    def gain(k, shape):
        return 1.0 + 0.01 * jax.random.normal(k, shape, jnp.float32)

    dt = jnp.exp(jax.random.uniform(ks[14], (DEPTH, GDN_HEADS), jnp.float32,
                                    minval=math.log(1e-3), maxval=math.log(1e-1)))
    page_table = jax.random.permutation(ks[7], n_pool)[:DEC_BATCH * n_pages]
    page_table = page_table.reshape(DEC_BATCH, n_pages).astype(jnp.int32)
    return {
        'x_prompt': nrm(ks[0], (BATCH, SEQ, D_MODEL), 1.0),
        'x_sample': nrm(ks[1], (DEC_BATCH, DEC_SEQ, D_MODEL), 1.0),
        'state_gdn': nrm(ks[2], (DEPTH, DEC_BATCH, GDN_HEADS, GDN_HEAD_DIM, GDN_HEAD_DIM),
                         GDN_HEAD_DIM ** -0.5),
        'state_gdn_conv': nrm(ks[3], (DEPTH, DEC_BATCH, GDN_CONV - 1, 3 * GDN_WIDTH), 1.0),
        'state_sconv': nrm(ks[4], (DEPTH, DEC_BATCH, SC_CONV - 1, SC_WIDTH), 1.0),
        'cache_k': nrm(ks[5], (DEPTH, n_pool, PAGE_SIZE, MOBA_HEADS, MOBA_HEAD_DIM), 1.0),
        'cache_v': nrm(ks[6], (DEPTH, n_pool, PAGE_SIZE, MOBA_HEADS, MOBA_HEAD_DIM), 1.0),
        'page_table': page_table,
        'norm_ffn1': gain(ks[8], (DEPTH, D_MODEL)),
        'ffn1_w_gate': nrm(ks[9], (DEPTH, D_MODEL, D_FF), D_MODEL ** -0.5),
        'ffn1_w_up': nrm(ks[10], (DEPTH, D_MODEL, D_FF), D_MODEL ** -0.5),
        'ffn1_w_down': nrm(ks[11], (DEPTH, D_FF, D_MODEL), D_FF ** -0.5),
        'norm_mix': gain(ks[12], (DEPTH, D_MODEL)),
        'w_in': nrm(ks[13], (DEPTH, D_MODEL, PROJ_WIDTH), D_MODEL ** -0.5),
        'gdn_conv_w': nrm(ks[15], (DEPTH, GDN_CONV, 3 * GDN_WIDTH), GDN_CONV ** -0.5),
        'gdn_a_log': jnp.log(jax.random.uniform(ks[16], (DEPTH, GDN_HEADS), jnp.float32,
                                                minval=1.0, maxval=16.0)),
        'gdn_dt_bias': dt + jnp.log(-jnp.expm1(-dt)),
        'gdn_norm_w': gain(ks[17], (DEPTH, GDN_HEAD_DIM)),
        'sc_conv_w': nrm(ks[18], (DEPTH, SC_CONV, SC_WIDTH), SC_CONV ** -0.5),
        'w_out': nrm(ks[19], (DEPTH, MIX_WIDTH, D_MODEL), MIX_WIDTH ** -0.5),
        'norm_ffn2': gain(ks[20], (DEPTH, D_MODEL)),
        'ffn2_w_gate': nrm(ks[21], (DEPTH, D_MODEL, D_FF), D_MODEL ** -0.5),
        'ffn2_w_up': nrm(ks[22], (DEPTH, D_MODEL, D_FF), D_MODEL ** -0.5),
        'ffn2_w_down': nrm(ks[23], (DEPTH, D_FF, D_MODEL), D_FF ** -0.5),
        'norm_final': gain(ks[24], (D_MODEL,)),
    }


def reference(x_prompt, x_sample, state_gdn, state_gdn_conv, state_sconv, cache_k, cache_v,
              page_table, norm_ffn1, ffn1_w_gate, ffn1_w_up, ffn1_w_down, norm_mix, w_in,
              gdn_conv_w, gdn_a_log, gdn_dt_bias, gdn_norm_w, sc_conv_w, w_out,
              norm_ffn2, ffn2_w_gate, ffn2_w_up, ffn2_w_down, norm_final):
    layer_params = (norm_ffn1, ffn1_w_gate, ffn1_w_up, ffn1_w_down, norm_mix, w_in,
                    gdn_conv_w, gdn_a_log, gdn_dt_bias, gdn_norm_w, sc_conv_w, w_out,
                    norm_ffn2, ffn2_w_gate, ffn2_w_up, ffn2_w_down)
    pb = x_prompt.shape[0]
    pdt = x_prompt.dtype
    prompt_states = [(jnp.zeros((pb, GDN_HEADS, GDN_HEAD_DIM, GDN_HEAD_DIM), pdt),
                      jnp.zeros((pb, GDN_CONV - 1, 3 * GDN_WIDTH), pdt),
                      jnp.zeros((pb, SC_CONV - 1, SC_WIDTH), pdt),
                      jnp.zeros((pb, 0, MOBA_HEADS, MOBA_HEAD_DIM), pdt),
                      jnp.zeros((pb, 0, MOBA_HEADS, MOBA_HEAD_DIM), pdt)) for _ in range(DEPTH)]
    db, n_seq_pages = page_table.shape
    past_shape = (db, n_seq_pages * PAGE_SIZE, MOBA_HEADS, MOBA_HEAD_DIM)
    sample_states = [(state_gdn[l], state_gdn_conv[l], state_sconv[l],
                      cache_k[l][page_table].reshape(past_shape),
                      cache_v[l][page_table].reshape(past_shape)) for l in range(DEPTH)]
    h_p, (p_gdn, p_conv, p_sconv, p_k, p_v) = trunk(x_prompt, prompt_states, layer_params)
    h_s, (s_gdn, s_conv, s_sconv, s_k, s_v) = trunk(x_sample, sample_states, layer_params)
    y_prompt = rms_norm(h_p, norm_final)
    y_sample = rms_norm(h_s, norm_final)
    return (y_prompt, y_sample, p_gdn, p_conv, p_sconv, p_k, p_v,
            s_gdn, s_conv, s_sconv, s_k, s_v)
```

```python
import functools

import jax
import jax.numpy as jnp
from jax import lax
from jax.experimental import pallas as pl
from jax.experimental.pallas import tpu as pltpu

NORM_EPS = 1e-6
NEG_INF = -1e30

GDN_HEADS = 4
GDN_HEAD_DIM = 128
GDN_WIDTH = GDN_HEADS * GDN_HEAD_DIM
GDN_CONV = 4
GDN_CHUNK = 64
SC_WIDTH = 256
SC_CONV = 3
MOBA_HEADS = 4
MOBA_HEAD_DIM = 64
MOBA_WIDTH = MOBA_HEADS * MOBA_HEAD_DIM
MOBA_BLOCK = 256
MOBA_TOPK = 3
PAGE_SIZE = 128

LANES = 128
SUBLANES = 8
VMEM_LIMIT_BYTES = 56 * 1024 * 1024

S_ROWS = SUBLANES
S_LO = GDN_CONV - 1
S_HI = S_LO + 4

F32 = jnp.float32
BF16 = jnp.bfloat16
HI = lax.Precision.HIGHEST


def _dot(a, b, precision=None):
    return jnp.dot(a, b, preferred_element_type=F32, precision=precision)


def _dot_nt(a, b, precision=None):
    return lax.dot_general(a, b, (((1,), (1,)), ((), ())),
                           preferred_element_type=F32, precision=precision)


def _dot_tn(a, b, precision=None):
    return lax.dot_general(a, b, (((0,), (0,)), ((), ())),
                           preferred_element_type=F32, precision=precision)


def _rms(x, w):
    return x * lax.rsqrt(jnp.mean(x * x, axis=-1, keepdims=True) + NORM_EPS) * w


def _silu(x):
    return x * jax.nn.sigmoid(x)


def _softplus(x):
    return jnp.maximum(x, 0.0) + jnp.log1p(jnp.exp(-jnp.abs(x)))


def _resident(shape):
    nd = len(shape)
    return pl.BlockSpec(shape, lambda *_: (0,) * nd, pipeline_mode=pl.Buffered(1))


def _ffn_body(*refs, has_mix, has_final, f_chunk):
    refs = list(refs)
    x_ref = refs.pop(0)
    if has_mix:
        oa_ref, ob_ref, oc_ref, wo_ref = refs[:4]
        refs = refs[4:]
    nw_ref, wg_ref, wu_ref, wd_ref = refs[:4]
    refs = refs[4:]
    if has_final:
        nf_ref = refs.pop(0)
    out_ref, xn_ref, acc_ref = refs

    x = x_ref[...]
    if has_mix:
        ga = oa_ref.shape[1]
        gb = ga + ob_ref.shape[1]
        x = (x + _dot(oa_ref[...], wo_ref[0:ga, :]) + _dot(ob_ref[...], wo_ref[ga:gb, :])
             + _dot(oc_ref[...], wo_ref[gb:, :]))
    xn_ref[...] = _rms(x, nw_ref[...]).astype(BF16)
    d_ff = wg_ref.shape[1]
    for c in range(d_ff // f_chunk):
        lo = c * f_chunk
        xn = xn_ref[...]
        g = _dot(xn, wg_ref[:, lo:lo + f_chunk])
        u = _dot(xn, wu_ref[:, lo:lo + f_chunk])
        h = (_silu(g) * u).astype(BF16)
        part = _dot(h, wd_ref[lo:lo + f_chunk, :])
        if c == 0:
            acc_ref[...] = part
        else:
            acc_ref[...] += part
    y = x + 0.5 * acc_ref[...]
    if has_final:
        y = _rms(y, nf_ref[...])
    out_ref[...] = y


def _ffn(x, nw, wg, wu, wd, mix=None, final_w=None, tm=512, f_chunk=256):
    m, d = x.shape
    d_ff = wg.shape[1]
    tm = min(tm, m)
    assert m % tm == 0 and d_ff % f_chunk == 0
    row = lambda i: (i, 0)
    in_specs = [pl.BlockSpec((tm, d), row)]
    args = [x]
    if mix is not None:
        oa, ob, oc, wo = mix
        in_specs += [pl.BlockSpec((tm, oa.shape[1]), row), pl.BlockSpec((tm, ob.shape[1]), row),
                     pl.BlockSpec((tm, oc.shape[1]), row), _resident(wo.shape)]
        args += [oa, ob, oc, wo]
    in_specs += [_resident((1, d)), _resident(wg.shape), _resident(wu.shape), _resident(wd.shape)]
    args += [nw.reshape(1, d), wg, wu, wd]
    if final_w is not None:
        in_specs.append(_resident((1, d)))
        args.append(final_w.reshape(1, d))
    body = functools.partial(_ffn_body, has_mix=mix is not None, has_final=final_w is not None,
                             f_chunk=f_chunk)
    return pl.pallas_call(
        body,
        name="ffn_mix" if mix is not None else "ffn",
        grid=(m // tm,),
        in_specs=in_specs,
        out_specs=pl.BlockSpec((tm, d), row),
        out_shape=jax.ShapeDtypeStruct((m, d), F32),
        scratch_shapes=[pltpu.VMEM((tm, d), BF16), pltpu.VMEM((tm, d), F32)],
        compiler_params=pltpu.CompilerParams(dimension_semantics=("arbitrary",),
                                             vmem_limit_bytes=VMEM_LIMIT_BYTES),
    )(*args)


_PROJ_WIDTHS = (3 * GDN_WIDTH, GDN_WIDTH, 3 * SC_WIDTH, MOBA_WIDTH, MOBA_WIDTH, MOBA_WIDTH, LANES)


def _inproj_body(x_ref, nw_ref, w_ref, *out_refs):
    xn = _rms(x_ref[...], nw_ref[...]).astype(BF16)
    lo = 0
    for ref in out_refs:
        width = ref.shape[1]
        ref[...] = _dot(xn, w_ref[:, lo:lo + width])
        lo += width


def _inproj(x, nw, w, tm=512):
    m, d = x.shape
    tm = min(tm, m)
    assert m % tm == 0 and w.shape[1] == sum(_PROJ_WIDTHS)
    row = lambda i: (i, 0)
    return pl.pallas_call(
        _inproj_body,
        name="inproj",
        grid=(m // tm,),
        in_specs=[pl.BlockSpec((tm, d), row), _resident((1, d)), _resident(w.shape)],
        out_specs=[pl.BlockSpec((tm, wd), row) for wd in _PROJ_WIDTHS],
        out_shape=[jax.ShapeDtypeStruct((m, wd), F32) for wd in _PROJ_WIDTHS],
        compiler_params=pltpu.CompilerParams(dimension_semantics=("arbitrary",),
                                             vmem_limit_bytes=VMEM_LIMIT_BYTES),
    )(x, nw.reshape(1, d), w)


def _unit_lower_inverse(m_strict, c):
    ri = lax.broadcasted_iota(jnp.int32, (c, c), 0)
    ci = lax.broadcasted_iota(jnp.int32, (c, c), 1)
    inv = jnp.where(ri == ci, 1.0, 0.0) - m_strict
    p = m_strict
    span = 2
    while span < c:
        p = _dot(p, p, HI)
        inv = inv + _dot(inv, p, HI)
        span *= 2
    return inv


def _mixer_body(*refs, c, has_state, row_lo, row_hi):
    refs = list(refs)
    qkv_ref, z_ref, ba_ref, sc_ref = refs[:4]
    refs = refs[4:]
    if has_state:
        qh_ref, sh_ref, s0_ref = refs[:3]
        refs = refs[3:]
    cw_ref, gp_ref, nw_ref, scw_ref = refs[:4]
    oa_ref, ob_ref, st_ref, sct_ref = refs[4:8]
    xbuf, gbuf, s_ref = refs[8:]
    n = pl.program_id(1)
    hd = GDN_HEAD_DIM
    pad = SUBLANES

    @pl.when(n == 0)
    def _():
        xbuf[0:pad, :] = jnp.zeros((pad, xbuf.shape[1]), F32)
        gbuf[0:pad, :] = jnp.zeros((pad, gbuf.shape[1]), F32)
        if has_state:
            s_ref[...] = s0_ref[0, 0]
        else:
            s_ref[...] = jnp.zeros(s_ref.shape, F32)

    rows = lax.broadcasted_iota(jnp.int32, (c, 1), 0)
    is_hist = rows < row_lo

    x = qkv_ref[0]
    if has_state:
        x = jnp.where(is_hist, qh_ref[0], x)
    xbuf[pad:pad + c, :] = x
    y = xbuf[pad - 3:pad - 3 + c, :] * cw_ref[0:1, :]
    y = y + xbuf[pad - 2:pad - 2 + c, :] * cw_ref[1:2, :]
    y = y + xbuf[pad - 1:pad - 1 + c, :] * cw_ref[2:3, :]
    y = y + x * cw_ref[3:4, :]
    tail = xbuf[c:c + pad, :]
    xbuf[0:pad, :] = tail
    act = _silu(y)

    sc = sc_ref[0]
    scw = SC_WIDTH
    gated = sc[:, scw:2 * scw] * sc[:, 0:scw]
    if has_state:
        gated = jnp.where(is_hist, sh_ref[0], gated)
    gbuf[pad:pad + c, :] = gated
    yb = gbuf[pad - 2:pad - 2 + c, :] * scw_ref[0:1, :]
    yb = yb + gbuf[pad - 1:pad - 1 + c, :] * scw_ref[1:2, :]
    yb = yb + gated * scw_ref[2:3, :]
    ob_ref[0] = (sc[:, 2 * scw:3 * scw] * yb).astype(ob_ref.dtype)
    gtail = gbuf[c:c + pad, :]
    gbuf[0:pad, :] = gtail
    sct_ref[0] = gtail

    ba = ba_ref[0]
    live = jnp.where((rows >= row_lo) & (rows < row_hi), 1.0, 0.0)
    beta_all = jax.nn.sigmoid(ba) * live
    g_all = -jnp.exp(gp_ref[0:1, :]) * _softplus(ba + gp_ref[1:2, :]) * live
    ri = lax.broadcasted_iota(jnp.int32, (c, c), 0)
    ci = lax.broadcasted_iota(jnp.int32, (c, c), 1)
    incl = ri >= ci
    strict = ri > ci
    gc_all = _dot(jnp.where(incl, 1.0, 0.0), g_all, HI)
    gct_all = gc_all.T
    glast_all = gc_all[c - 1:c, :]
    eg_all = jnp.exp(gc_all)
    etail_all = jnp.exp(glast_all - gc_all)
    cd_all = jnp.exp(glast_all)

    for h in range(GDN_HEADS):
        xq = act[:, h * hd:(h + 1) * hd]
        xk = act[:, GDN_WIDTH + h * hd:GDN_WIDTH + (h + 1) * hd]
        v = act[:, 2 * GDN_WIDTH + h * hd:2 * GDN_WIDTH + (h + 1) * hd]
        q = xq * lax.rsqrt(jnp.sum(xq * xq, axis=-1, keepdims=True) + NORM_EPS) * (hd ** -0.5)
        k = xk * lax.rsqrt(jnp.sum(xk * xk, axis=-1, keepdims=True) + NORM_EPS)
        gcol = GDN_HEADS + h
        beta = beta_all[:, h:h + 1]
        gc = gc_all[:, gcol:gcol + 1]
        gct = gct_all[gcol:gcol + 1, :]
        decay = jnp.where(incl, jnp.exp(jnp.where(incl, gc - gct, 0.0)), 0.0)
        kk = _dot_nt(k, k, HI)
        m_strict = jnp.where(strict, beta * kk * decay, 0.0)
        inv = _unit_lower_inverse(m_strict, c)
        rhs = jnp.concatenate([k * (beta * eg_all[:, gcol:gcol + 1]), v * beta], axis=-1)
        sol = _dot(inv, rhs, HI)
        w_k, u_v = sol[:, :hd], sol[:, hd:]
        attn = _dot_nt(q, k, HI) * decay
        qd = q * eg_all[:, gcol:gcol + 1]
        kt = k * etail_all[:, gcol:gcol + 1]
        s = s_ref[h]
        u = u_v - _dot(w_k, s, HI)
        o = _dot(qd, s, HI) + _dot(attn, u, HI)
        s_ref[h] = s * cd_all[:, gcol:gcol + 1] + _dot_tn(kt, u, HI)
        zh = z_ref[0, :, h * hd:(h + 1) * hd]
        oa_ref[0, :, h * hd:(h + 1) * hd] = (_rms(o, nw_ref[...]) * _silu(zh)).astype(oa_ref.dtype)

    @pl.when(n == pl.num_programs(1) - 1)
    def _():
        st_ref[0] = s_ref[...]


def _mixer(qkv, z, ba, sc, conv_w, gparams, norm_w, sc_conv_w, *, chunk, state=None,
           row_lo=0, row_hi=None):
    b, l, _ = qkv.shape
    c = chunk
    assert l % c == 0 and c % SUBLANES == 0
    nc = l // c
    row_hi = c if row_hi is None else row_hi
    blk = lambda w: pl.BlockSpec((1, c, w), lambda i, n: (i, n, 0))
    in_specs = [blk(qkv.shape[2]), blk(z.shape[2]), blk(ba.shape[2]), blk(sc.shape[2])]
    args = [qkv, z, ba, sc]
    if state is not None:
        assert nc == 1 and c == S_ROWS
        qh, sh, s0, layer = state
        in_specs += [pl.BlockSpec((1, S_ROWS, qh.shape[2]), lambda i, n: (i, 0, 0)),
                     pl.BlockSpec((1, S_ROWS, sh.shape[2]), lambda i, n: (i, 0, 0)),
                     pl.BlockSpec((1, 1) + s0.shape[2:], lambda i, n: (layer, i, 0, 0, 0))]
        args += [qh, sh, s0]
    const = lambda a: pl.BlockSpec(a.shape, lambda i, n: (0,) * a.ndim)
    norm_w = norm_w.reshape(1, -1)
    in_specs += [const(conv_w), const(gparams), const(norm_w), const(sc_conv_w)]
    args += [conv_w, gparams, norm_w, sc_conv_w]
    st_shape = (b, GDN_HEADS, GDN_HEAD_DIM, GDN_HEAD_DIM)
    out_shape = [jax.ShapeDtypeStruct((b, l, GDN_WIDTH), BF16),
                 jax.ShapeDtypeStruct((b, l, SC_WIDTH), BF16),
                 jax.ShapeDtypeStruct(st_shape, F32),
                 jax.ShapeDtypeStruct((b, SUBLANES, SC_WIDTH), F32)]
    out_specs = [pl.BlockSpec((1, c, GDN_WIDTH), lambda i, n: (i, n, 0)),
                 pl.BlockSpec((1, c, SC_WIDTH), lambda i, n: (i, n, 0)),
                 pl.BlockSpec((1,) + st_shape[1:], lambda i, n: (i, 0, 0, 0)),
                 pl.BlockSpec((1, SUBLANES, SC_WIDTH), lambda i, n: (i, 0, 0))]
    body = functools.partial(_mixer_body, c=c, has_state=state is not None,
                             row_lo=row_lo, row_hi=row_hi)
    return pl.pallas_call(
        body,
        name="seq_mixer",
        grid=(b, nc),
        in_specs=in_specs,
        out_specs=out_specs,
        out_shape=out_shape,
        scratch_shapes=[pltpu.VMEM((c + SUBLANES, qkv.shape[2]), F32),
                        pltpu.VMEM((c + SUBLANES, SC_WIDTH), F32),
                        pltpu.VMEM(st_shape[1:], F32)],
        compiler_params=pltpu.CompilerParams(dimension_semantics=("arbitrary", "arbitrary"),
                                             vmem_limit_bytes=VMEM_LIMIT_BYTES),
    )(*args)


def _select_blocks(gate, n_valid):
    r, nb = gate.shape
    lane = lax.broadcasted_iota(jnp.int32, (r, nb), 1)
    gate = jnp.where(lane < n_valid, gate, NEG_INF)
    sel = jnp.zeros((r, nb), F32)
    for j in range(nb):
        gj = gate[:, j:j + 1]
        beats = (gate > gj) | ((gate == gj) & (lane < j))
        cnt = jnp.sum(jnp.where(beats, 1.0, 0.0), axis=1, keepdims=True)
        hit = (cnt < MOBA_TOPK) & (j < n_valid)
        sel = jnp.where((lane == j) & hit, 1.0, sel)
    return sel


def _moba_prompt_body(q_ref, k_ref, v_ref, o_ref, kmean_ref):
    i = pl.program_id(1)
    blk = MOBA_BLOCK
    hd = MOBA_HEAD_DIM
    nb = k_ref.shape[1] // blk

    @pl.when(i == 0)
    def _():
        for j in range(nb):
            kmean_ref[j:j + 1, :] = jnp.mean(k_ref[0, j * blk:(j + 1) * blk, :], axis=0,
                                             keepdims=True)

    qrow = lax.broadcasted_iota(jnp.int32, (blk, blk), 0)
    kcol = lax.broadcasted_iota(jnp.int32, (blk, blk), 1)
    causal = kcol <= qrow
    lane_nb = lax.broadcasted_iota(jnp.int32, (blk, nb), 1)
    own_lo = pl.multiple_of(i * blk, blk)
    outs = []
    for h in range(MOBA_HEADS):
        hs = slice(h * hd, (h + 1) * hd)
        qh = q_ref[0, :, hs] * (hd ** -0.5)
        gate = _dot_nt(qh, kmean_ref[:, hs], HI)
        sel = _select_blocks(gate, i)
        qb = qh.astype(BF16)

        s = _dot_nt(qb, k_ref[0, pl.ds(own_lo, blk), hs].astype(BF16))
        s = jnp.where(causal, s, NEG_INF)
        m0 = jnp.max(s, axis=1, keepdims=True)
        p = jnp.exp(s - m0)
        l0 = jnp.sum(p, axis=1, keepdims=True)
        acc0 = _dot(p.astype(BF16), v_ref[0, pl.ds(own_lo, blk), hs].astype(BF16))

        def past_block(j, carry, hs=hs, qb=qb, sel=sel):
            m, l, acc = carry
            lo = pl.multiple_of(j * blk, blk)
            s = _dot_nt(qb, k_ref[0, pl.ds(lo, blk), hs].astype(BF16))
            keep = jnp.sum(jnp.where(lane_nb == j, sel, 0.0), axis=1, keepdims=True) > 0.5
            s = jnp.where(keep, s, NEG_INF)
            m_new = jnp.maximum(m, jnp.max(s, axis=1, keepdims=True))
            alpha = jnp.exp(m - m_new)
            p = jnp.exp(s - m_new)
            l = alpha * l + jnp.sum(p, axis=1, keepdims=True)
            acc = alpha * acc + _dot(p.astype(BF16), v_ref[0, pl.ds(lo, blk), hs].astype(BF16))
            return m_new, l, acc

        _, l, acc = lax.fori_loop(0, i, past_block, (m0, l0, acc0))
        outs.append(acc / l)
    o_ref[0] = jnp.concatenate(outs, axis=-1).astype(o_ref.dtype)


def _moba_prompt(q, k, v):
    b, l, w = q.shape
    blk = MOBA_BLOCK
    assert l % blk == 0
    return pl.pallas_call(
        _moba_prompt_body,
        name="moba_prompt",
        grid=(b, l // blk),
        in_specs=[pl.BlockSpec((1, blk, w), lambda i, t: (i, t, 0)),
                  pl.BlockSpec((1, l, w), lambda i, t: (i, 0, 0)),
                  pl.BlockSpec((1, l, w), lambda i, t: (i, 0, 0))],
        out_specs=pl.BlockSpec((1, blk, w), lambda i, t: (i, t, 0)),
        out_shape=jax.ShapeDtypeStruct((b, l, w), BF16),
        scratch_shapes=[pltpu.VMEM((l // blk, w), F32)],
        compiler_params=pltpu.CompilerParams(dimension_semantics=("arbitrary", "arbitrary"),
                                             vmem_limit_bytes=VMEM_LIMIT_BYTES),
    )(q, k, v)


def _moba_sample_body(pt_ref, q_ref, kn_ref, vn_ref, *refs, n_pages):
    del pt_ref
    k_refs = refs[:n_pages]
    v_refs = refs[n_pages:2 * n_pages]
    o_ref = refs[2 * n_pages]
    hd = MOBA_HEAD_DIM
    r = S_ROWS
    w = MOBA_WIDTH
    pages_per_block = MOBA_BLOCK // PAGE_SIZE
    nb = n_pages // pages_per_block

    lane = lax.broadcasted_iota(jnp.int32, (r, w), 1)
    head_masks = [jnp.where((lane >= h * hd) & (lane < (h + 1) * hd), 1.0, 0.0)
                  for h in range(MOBA_HEADS)]
    q = q_ref[0] * (hd ** -0.5)
    q_bd = jnp.concatenate([q * hm for hm in head_masks], axis=0)
    q_bb = q_bd.astype(BF16)

    kmeans, m_cols, l_cols, accs = [], [], [], []
    for j in range(nb):
        s_parts, ksum = [], None
        for pp in range(pages_per_block):
            kp = k_refs[j * pages_per_block + pp][0]
            part = jnp.sum(kp, axis=0, keepdims=True)
            ksum = part if ksum is None else ksum + part
            s_parts.append(_dot_nt(q_bb, kp.astype(BF16)))
        kmeans.append(ksum * (1.0 / MOBA_BLOCK))
        s = jnp.concatenate(s_parts, axis=-1)
        mj = jnp.max(s, axis=1, keepdims=True)
        e = jnp.exp(s - mj)
        m_cols.append(mj)
        l_cols.append(jnp.sum(e, axis=1, keepdims=True))
        acc = None
        for pp in range(pages_per_block):
            vp = v_refs[j * pages_per_block + pp][0].astype(BF16)
            part = _dot(e[:, pp * PAGE_SIZE:(pp + 1) * PAGE_SIZE].astype(BF16), vp)
            acc = part if acc is None else acc + part
        accs.append(acc)

    kmean = jnp.concatenate(kmeans, axis=0)
    gate = _dot_nt(q_bd, kmean, HI)
    sel = _select_blocks(gate, nb) > 0.5
    m_all = jnp.concatenate(m_cols, axis=1)
    l_all = jnp.concatenate(l_cols, axis=1)

    s_own = _dot_nt(q_bb, kn_ref[0].astype(BF16))
    qt = lax.broadcasted_iota(jnp.int32, (MOBA_HEADS * r, r), 0) % r
    kt = lax.broadcasted_iota(jnp.int32, (MOBA_HEADS * r, r), 1)
    own = (kt >= S_LO) & (kt <= qt)
    m_tot = jnp.maximum(jnp.max(jnp.where(sel, m_all, NEG_INF), axis=1, keepdims=True),
                        jnp.max(jnp.where(own, s_own, NEG_INF), axis=1, keepdims=True))
    wj = jnp.where(sel, jnp.exp(jnp.where(sel, m_all - m_tot, 0.0)), 0.0)
    p_own = jnp.where(own, jnp.exp(jnp.where(own, s_own - m_tot, 0.0)), 0.0)
    l_tot = (jnp.sum(wj * l_all, axis=1, keepdims=True)
             + jnp.sum(p_own, axis=1, keepdims=True))
    acc = _dot(p_own.astype(BF16), vn_ref[0].astype(BF16))
    for j in range(nb):
        acc = acc + wj[:, j:j + 1] * accs[j]
    o_bd = acc / l_tot
    o = None
    for h in range(MOBA_HEADS):
        part = o_bd[h * r:(h + 1) * r, :] * head_masks[h]
        o = part if o is None else o + part
    o_ref[0] = o.astype(o_ref.dtype)


def _moba_sample(q, k_new, v_new, cache_k, cache_v, page_table, layer):
    b, r, w = q.shape
    depth, n_pool, page, _, _ = cache_k.shape
    n_pages = page_table.shape[1]
    assert page == PAGE_SIZE and r == S_ROWS
    ck = cache_k.reshape(depth * n_pool, page, w)
    cv = cache_v.reshape(depth * n_pool, page, w)
    base = layer * n_pool
    tile = pl.BlockSpec((1, r, w), lambda i, pt: (i, 0, 0))

    def page_spec(p):
        return pl.BlockSpec((1, page, w), lambda i, pt: (base + pt[i, p], 0, 0))

    grid_spec = pltpu.PrefetchScalarGridSpec(
        num_scalar_prefetch=1,
        grid=(b,),
        in_specs=([tile, tile, tile] + [page_spec(p) for p in range(n_pages)]
                  + [page_spec(p) for p in range(n_pages)]),
        out_specs=tile,
    )
    body = functools.partial(_moba_sample_body, n_pages=n_pages)
    return pl.pallas_call(
        body,
        name="moba_sample",
        grid_spec=grid_spec,
        out_shape=jax.ShapeDtypeStruct((b, r, w), BF16),
        compiler_params=pltpu.CompilerParams(dimension_semantics=("arbitrary",),
                                             vmem_limit_bytes=VMEM_LIMIT_BYTES),
    )(page_table, q, k_new, v_new, *([ck] * n_pages), *([cv] * n_pages))


def _trunk(x, weights, norm_final, *, chunk, sample=None):
    b, l, d = x.shape
    depth = len(weights)
    x = x.reshape(b * l, d)
    row_lo, row_hi = (S_LO, S_HI) if sample is not None else (0, chunk)
    states = []
    for layer, wts in enumerate(weights):
        x = _ffn(x, wts["n_ffn1"], wts["ffn1_g"], wts["ffn1_u"], wts["ffn1_d"])
        qkv, z, sc, mq, mk, mv, ba = _inproj(x, wts["n_mix"], wts["w_in"])
        seq = lambda a: a.reshape(b, l, a.shape[-1])
        state = None
        if sample is not None:
            state = (sample["gdn_conv"][layer], sample["sconv"][layer], sample["gdn"], layer)
        o_a, o_b, s_new, sc_tail = _mixer(
            seq(qkv), seq(z), seq(ba), seq(sc), wts["gdn_conv_w"], wts["gparams"],
            wts["gdn_norm_w"], wts["sc_conv_w"], chunk=chunk, state=state,
            row_lo=row_lo, row_hi=row_hi)
        if sample is None:
            o_c = _moba_prompt(seq(mq), seq(mk), seq(mv))
        else:
            o_c = _moba_sample(seq(mq), seq(mk), seq(mv), sample["cache_k"], sample["cache_v"],
                               sample["page_table"], layer)
        mix = (o_a.reshape(b * l, -1), o_b.reshape(b * l, -1), o_c.reshape(b * l, -1),
               wts["w_out"])
        x = _ffn(x, wts["n_ffn2"], wts["ffn2_g"], wts["ffn2_u"], wts["ffn2_d"], mix=mix,
                 final_w=norm_final if layer == depth - 1 else None)
        states.append((s_new, seq(qkv), sc_tail, seq(mk), seq(mv)))
    return x.reshape(b, l, d), states


def kernel(x_prompt, x_sample, state_gdn, state_gdn_conv, state_sconv, cache_k, cache_v,
           page_table, norm_ffn1, ffn1_w_gate, ffn1_w_up, ffn1_w_down, norm_mix, w_in,
           gdn_conv_w, gdn_a_log, gdn_dt_bias, gdn_norm_w, sc_conv_w, w_out,
           norm_ffn2, ffn2_w_gate, ffn2_w_up, ffn2_w_down, norm_final):
    depth = w_in.shape[0]
    gw, sw, mw = GDN_WIDTH, SC_WIDTH, MOBA_WIDTH
    nh = GDN_HEADS

    weights = []
    for la in range(depth):
        w = w_in[la]
        ba_off = 4 * gw
        sc_off = ba_off + 2 * nh
        mo_off = sc_off + 3 * sw
        w_perm = jnp.concatenate(
            [w[:, :ba_off], w[:, sc_off:mo_off], w[:, mo_off:mo_off + 3 * mw],
             jnp.pad(w[:, ba_off:sc_off], ((0, 0), (0, LANES - 2 * nh)))], axis=1)
        gparams = jnp.zeros((SUBLANES, LANES), F32)
        gparams = gparams.at[0, nh:2 * nh].set(gdn_a_log[la].astype(F32))
        gparams = gparams.at[1, nh:2 * nh].set(gdn_dt_bias[la].astype(F32))
        weights.append(dict(
            n_ffn1=norm_ffn1[la], ffn1_g=ffn1_w_gate[la].astype(BF16),
            ffn1_u=ffn1_w_up[la].astype(BF16), ffn1_d=ffn1_w_down[la].astype(BF16),
            n_mix=norm_mix[la], w_in=w_perm.astype(BF16),
            gdn_conv_w=gdn_conv_w[la], gparams=gparams, gdn_norm_w=gdn_norm_w[la],
            sc_conv_w=sc_conv_w[la], w_out=w_out[la].astype(BF16),
            n_ffn2=norm_ffn2[la], ffn2_g=ffn2_w_gate[la].astype(BF16),
            ffn2_u=ffn2_w_up[la].astype(BF16), ffn2_d=ffn2_w_down[la].astype(BF16)))

    y_p, st_p = _trunk(x_prompt, weights, norm_final, chunk=GDN_CHUNK)
    hist = GDN_CONV - 1
    p_gdn = jnp.stack([s[0] for s in st_p])
    p_conv = jnp.stack([s[1][:, -hist:] for s in st_p])
    p_sconv = jnp.stack([s[2][:, SUBLANES - (SC_CONV - 1):] for s in st_p])
    pb, pl_, _ = x_prompt.shape
    p_k = jnp.stack([s[3] for s in st_p]).reshape(depth, pb, pl_, MOBA_HEADS, MOBA_HEAD_DIM)
    p_v = jnp.stack([s[4] for s in st_p]).reshape(depth, pb, pl_, MOBA_HEADS, MOBA_HEAD_DIM)

    db, dl, d = x_sample.shape
    assert dl == S_HI - S_LO
    tile_pad = lambda a, lo: jnp.pad(a, ((0, 0),) * (a.ndim - 2)
                                     + ((lo, S_ROWS - lo - a.shape[-2]), (0, 0)))
    sample = dict(
        gdn=state_gdn,
        gdn_conv=tile_pad(state_gdn_conv, 0),
        sconv=tile_pad(state_sconv, S_LO - (SC_CONV - 1)),
        cache_k=cache_k, cache_v=cache_v, page_table=page_table)
    y_s, st_s = _trunk(tile_pad(x_sample, S_LO), weights, norm_final, chunk=S_ROWS,
                       sample=sample)
    y_s = y_s[:, S_LO:S_HI]
    s_gdn = jnp.stack([s[0] for s in st_s])
    s_conv = jnp.stack([s[1][:, S_HI - hist:S_HI] for s in st_s])
    s_sconv = jnp.stack([s[2][:, S_HI - (SC_CONV - 1):S_HI] for s in st_s])
    s_k = jnp.stack([s[3][:, S_LO:S_HI] for s in st_s]).reshape(
        depth, db, dl, MOBA_HEADS, MOBA_HEAD_DIM)
    s_v = jnp.stack([s[4][:, S_LO:S_HI] for s in st_s]).reshape(
        depth, db, dl, MOBA_HEADS, MOBA_HEAD_DIM)
    return (y_p, y_s, p_gdn, p_conv, p_sconv, p_k, p_v, s_gdn, s_conv, s_sconv, s_k, s_v)
```

```python
import functools

import jax
import jax.numpy as jnp
from jax import lax
from jax.experimental import pallas as pl
from jax.experimental.pallas import tpu as pltpu

NORM_EPS = 1e-6
NEG_INF = -1e30

GDN_HEADS = 4
GDN_HEAD_DIM = 128
GDN_WIDTH = GDN_HEADS * GDN_HEAD_DIM
GDN_CONV = 4
GDN_CHUNK = 64
SC_WIDTH = 256
SC_CONV = 3
MOBA_HEADS = 4
MOBA_HEAD_DIM = 64
MOBA_WIDTH = MOBA_HEADS * MOBA_HEAD_DIM
MOBA_BLOCK = 256
MOBA_TOPK = 3
PAGE_SIZE = 128

LANES = 128
SUBLANES = 8
VMEM_LIMIT_BYTES = 56 * 1024 * 1024

S_ROWS = SUBLANES
S_LO = GDN_CONV - 1
S_HI = S_LO + 4

GROUP_ROWS = GDN_CHUNK
STACK_ROWS = GDN_HEADS * GROUP_ROWS

F32 = jnp.float32
BF16 = jnp.bfloat16
HI = lax.Precision.HIGHEST


def _dot(a, b, precision=None):
    return jnp.dot(a, b, preferred_element_type=F32, precision=precision)


def _dot_nt(a, b, precision=None):
    return lax.dot_general(a, b, (((1,), (1,)), ((), ())),
                           preferred_element_type=F32, precision=precision)


def _dot_tn(a, b, precision=None):
    return lax.dot_general(a, b, (((0,), (0,)), ((), ())),
                           preferred_element_type=F32, precision=precision)


def _rms(x, w):
    return x * lax.rsqrt(jnp.mean(x * x, axis=-1, keepdims=True) + NORM_EPS) * w


def _silu(x):
    return x * jax.nn.sigmoid(x)


def _softplus(x):
    return jnp.maximum(x, 0.0) + jnp.log1p(jnp.exp(-jnp.abs(x)))


def _resident(shape):
    nd = len(shape)
    return pl.BlockSpec(shape, lambda *_: (0,) * nd, pipeline_mode=pl.Buffered(1))


def _params(*semantics):
    return pltpu.CompilerParams(dimension_semantics=semantics,
                                vmem_limit_bytes=VMEM_LIMIT_BYTES)


def _ffn_body(*refs, has_mix, oc_transposed, has_final, f_chunk):
    refs = list(refs)
    x_ref = refs.pop(0)
    if has_mix:
        oa_ref, ob_ref, oc_ref, wo_ref = refs[:4]
        refs = refs[4:]
    nw_ref, wg_ref, wu_ref, wd_ref = refs[:4]
    refs = refs[4:]
    if has_final:
        nf_ref = refs.pop(0)
    out_ref, xn_ref, acc_ref = refs

    x = x_ref[...]
    if has_mix:
        ga = oa_ref.shape[1]
        gb = ga + ob_ref.shape[1]
        x = x + _dot(oa_ref[...], wo_ref[0:ga, :]) + _dot(ob_ref[...], wo_ref[ga:gb, :])
        if oc_transposed:
            x = x + _dot_tn(oc_ref[0], wo_ref[gb:, :])
        else:
            x = x + _dot(oc_ref[...], wo_ref[gb:, :])
    xn_ref[...] = _rms(x, nw_ref[...]).astype(BF16)
    d_ff = wg_ref.shape[1]
    for c in range(d_ff // f_chunk):
        lo = c * f_chunk
        xn = xn_ref[...]
        g = _dot(xn, wg_ref[:, lo:lo + f_chunk])
        u = _dot(xn, wu_ref[:, lo:lo + f_chunk])
        h = (_silu(g) * u).astype(BF16)
        part = _dot(h, wd_ref[lo:lo + f_chunk, :])
        if c == 0:
            acc_ref[...] = part
        else:
            acc_ref[...] += part
    y = x + 0.5 * acc_ref[...]
    if has_final:
        y = _rms(y, nf_ref[...])
    out_ref[...] = y


def _ffn(x, nw, wg, wu, wd, mix=None, final_w=None, tm=512, f_chunk=256):
    m, d = x.shape
    d_ff = wg.shape[1]
    tm = min(tm, m)
    assert m % tm == 0 and d_ff % f_chunk == 0
    row = lambda i: (i, 0)
    in_specs = [pl.BlockSpec((tm, d), row)]
    args = [x]
    oc_transposed = False
    if mix is not None:
        oa, ob, oc, wo = mix
        oc_transposed = oc.ndim == 3
        if oc_transposed:
            tiles = oc.shape[2] // tm
            assert oc.shape[2] % tm == 0
            oc_spec = pl.BlockSpec((1, oc.shape[1], tm), lambda i: (i // tiles, 0, i % tiles))
        else:
            oc_spec = pl.BlockSpec((tm, oc.shape[1]), row)
        in_specs += [pl.BlockSpec((tm, oa.shape[1]), row), pl.BlockSpec((tm, ob.shape[1]), row),
                     oc_spec, _resident(wo.shape)]
        args += [oa, ob, oc, wo]
    in_specs += [_resident((1, d)), _resident(wg.shape), _resident(wu.shape), _resident(wd.shape)]
    args += [nw.reshape(1, d), wg, wu, wd]
    if final_w is not None:
        in_specs.append(_resident((1, d)))
        args.append(final_w.reshape(1, d))
    body = functools.partial(_ffn_body, has_mix=mix is not None, oc_transposed=oc_transposed,
                             has_final=final_w is not None, f_chunk=f_chunk)
    return pl.pallas_call(
        body,
        name="ffn_mix" if mix is not None else "ffn",
        grid=(m // tm,),
        in_specs=in_specs,
        out_specs=pl.BlockSpec((tm, d), row),
        out_shape=jax.ShapeDtypeStruct((m, d), F32),
        scratch_shapes=[pltpu.VMEM((tm, d), BF16), pltpu.VMEM((tm, d), F32)],
        compiler_params=_params("arbitrary"),
    )(*args)


_ROW_WIDTHS = (3 * GDN_WIDTH, GDN_WIDTH, 3 * SC_WIDTH, LANES, MOBA_WIDTH)


def _inproj_body(x_ref, nw_ref, w_ref, wqv_ref, *out_refs, transposed_qv):
    xn = _rms(x_ref[...], nw_ref[...]).astype(BF16)
    lo = 0
    for ref in out_refs[:len(_ROW_WIDTHS)]:
        width = ref.shape[1]
        ref[...] = _dot(xn, w_ref[:, lo:lo + width])
        lo += width
    q_ref, v_ref = out_refs[len(_ROW_WIDTHS):]
    mw = MOBA_WIDTH
    if transposed_qv:
        q_ref[0] = _dot_nt(wqv_ref[0:mw, :], xn)
        v_ref[0] = _dot_nt(wqv_ref[mw:2 * mw, :], xn)
    else:
        q_ref[...] = _dot(xn, wqv_ref[:, 0:mw])
        v_ref[...] = _dot(xn, wqv_ref[:, mw:2 * mw])


def _inproj(x, nw, w_rows, w_qv, *, seq_len=None, tm=512):
    m, d = x.shape
    tm = min(tm, m)
    assert m % tm == 0 and w_rows.shape[1] == sum(_ROW_WIDTHS)
    row = lambda i: (i, 0)
    out_specs = [pl.BlockSpec((tm, wd), row) for wd in _ROW_WIDTHS]
    out_shape = [jax.ShapeDtypeStruct((m, wd), F32) for wd in _ROW_WIDTHS]
    transposed_qv = seq_len is not None
    if transposed_qv:
        assert seq_len % tm == 0
        tiles = seq_len // tm
        spec = pl.BlockSpec((1, MOBA_WIDTH, tm), lambda i: (i // tiles, 0, i % tiles))
        shape = jax.ShapeDtypeStruct((m // seq_len, MOBA_WIDTH, seq_len), F32)
    else:
        spec = pl.BlockSpec((tm, MOBA_WIDTH), row)
        shape = jax.ShapeDtypeStruct((m, MOBA_WIDTH), F32)
    out_specs += [spec, spec]
    out_shape += [shape, shape]
    return pl.pallas_call(
        functools.partial(_inproj_body, transposed_qv=transposed_qv),
        name="inproj",
        grid=(m // tm,),
        in_specs=[pl.BlockSpec((tm, d), row), _resident((1, d)), _resident(w_rows.shape),
                  _resident(w_qv.shape)],
        out_specs=out_specs,
        out_shape=out_shape,
        compiler_params=_params("arbitrary"),
    )(x, nw.reshape(1, d), w_rows, w_qv)


def _mixer_body(*refs, c, ng, has_state, row_lo, row_hi):
    refs = list(refs)
    qkv_ref, z_ref, ba_ref, sc_ref = refs[:4]
    refs = refs[4:]
    if has_state:
        qh_ref, sh_ref, s0_ref = refs[:3]
        refs = refs[3:]
    cw_ref, gp_ref, nw_ref, scw_ref = refs[:4]
    oa_ref, ob_ref, st_ref, gt_ref = refs[4:8]
    xbuf, gbuf, s_ref = refs[8:]
    n = pl.program_id(1)
    hd = GDN_HEAD_DIM
    gr = GROUP_ROWS
    sr = STACK_ROWS
    pad = SUBLANES
    seqs = gr // c
    shift = c.bit_length() - 1
    assert 1 << shift == c

    @pl.when(n == 0)
    def _():
        xbuf[:, 0:pad, :] = jnp.zeros((ng, pad, xbuf.shape[2]), F32)
        gbuf[:, 0:pad, :] = jnp.zeros((ng, pad, gbuf.shape[2]), F32)
        if has_state:
            s_ref[...] = s0_ref[0]
        else:
            s_ref[...] = jnp.zeros(s_ref.shape, F32)

    rows = lax.broadcasted_iota(jnp.int32, (gr, 1), 0) & (c - 1)
    is_hist = rows < row_lo
    live = jnp.where((rows >= row_lo) & (rows < row_hi), 1.0, 0.0)
    ri = lax.broadcasted_iota(jnp.int32, (gr, gr), 0)
    ci = lax.broadcasted_iota(jnp.int32, (gr, gr), 1)
    same = (ri >> shift) == (ci >> shift)
    cum_op = jnp.concatenate([jnp.where(same & (ri >= ci), 1.0, 0.0),
                              jnp.where(same, 1.0, 0.0)], axis=0)
    rs = lax.broadcasted_iota(jnp.int32, (sr, sr), 0)
    cs = lax.broadcasted_iota(jnp.int32, (sr, sr), 1)
    same_s = (rs >> shift) == (cs >> shift)
    incl = same_s & (rs >= cs)
    strict = same_s & (rs > cs)
    eye = jnp.where(rs == cs, 1.0, 0.0)

    def stack_heads(a, lo):
        return jnp.concatenate([a[:, lo + h * hd:lo + (h + 1) * hd] for h in range(GDN_HEADS)],
                               axis=0)

    def stack_cols(a, lo):
        return jnp.concatenate([a[:, lo + h:lo + h + 1] for h in range(GDN_HEADS)], axis=0)

    nh = GDN_HEADS
    groups = range(ng)

    def front(g):
        x = qkv_ref[g]
        if has_state:
            x = jnp.where(is_hist, qh_ref[g], x)
        xbuf[g, pad:pad + gr, :] = x
        y = xbuf[g, pad - 3:pad - 3 + gr, :] * cw_ref[0:1, :]
        y = y + xbuf[g, pad - 2:pad - 2 + gr, :] * cw_ref[1:2, :]
        y = y + xbuf[g, pad - 1:pad - 1 + gr, :] * cw_ref[2:3, :]
        y = y + x * cw_ref[3:4, :]
        tail = xbuf[g, gr:gr + pad, :]
        xbuf[g, 0:pad, :] = tail
        act = _silu(y)

        sc = sc_ref[g]
        scw = SC_WIDTH
        gated = sc[:, scw:2 * scw] * sc[:, 0:scw]
        if has_state:
            gated = jnp.where(is_hist, sh_ref[g], gated)
        gbuf[g, pad:pad + gr, :] = gated
        yb = gbuf[g, pad - 2:pad - 2 + gr, :] * scw_ref[0:1, :]
        yb = yb + gbuf[g, pad - 1:pad - 1 + gr, :] * scw_ref[1:2, :]
        yb = yb + gated * scw_ref[2:3, :]
        ob_ref[g] = (sc[:, 2 * scw:3 * scw] * yb).astype(ob_ref.dtype)
        gtail = gbuf[g, gr:gr + pad, :]
        gbuf[g, 0:pad, :] = gtail
        gt_ref[g] = gated

        ba = ba_ref[g]
        beta_all = jax.nn.sigmoid(ba) * live
        g_all = -jnp.exp(gp_ref[0:1, :]) * _softplus(ba + gp_ref[1:2, :]) * live
        cum = _dot(cum_op, g_all, HI)
        gc_all, glast_all = cum[:gr], cum[gr:]
        beta = stack_cols(beta_all, 0)
        gc = stack_cols(gc_all, nh)
        eg = jnp.exp(gc)
        etail = jnp.exp(stack_cols(glast_all, nh) - gc)
        gc_row = jnp.broadcast_to(gc, (sr, LANES)).T[0:1, :]
        decay = jnp.where(incl, jnp.exp(jnp.where(incl, gc - gc_row, 0.0)), 0.0)

        xq = stack_heads(act, 0)
        xk = stack_heads(act, GDN_WIDTH)
        v = stack_heads(act, 2 * GDN_WIDTH)
        q = xq * lax.rsqrt(jnp.sum(xq * xq, axis=-1, keepdims=True) + NORM_EPS) * (hd ** -0.5)
        k = xk * lax.rsqrt(jnp.sum(xk * xk, axis=-1, keepdims=True) + NORM_EPS)
        kb = k.astype(BF16)
        m_strict = jnp.where(strict, beta * _dot_nt(kb, kb) * decay, 0.0)
        return dict(q=q, k=k, v=v, kb=kb, beta=beta, eg=eg, etail=etail, decay=decay,
                    cd_all=jnp.exp(glast_all), m_strict=m_strict)

    fr = [front(g) for g in groups]

    invs = [eye - f["m_strict"] for f in fr]
    ps = [f["m_strict"] for f in fr]
    for _ in range(shift - 1):
        pbs = [p.astype(BF16) for p in ps]
        ps = [_dot(pb, pb) for pb in pbs]
        invs = [inv + _dot(inv.astype(BF16), p.astype(BF16)) for inv, p in zip(invs, ps)]

    def solve(f, inv):
        rhs = jnp.concatenate([f["k"] * (f["beta"] * f["eg"]), f["v"] * f["beta"]],
                              axis=-1).astype(BF16)
        sol = _dot(inv.astype(BF16), rhs)
        attn = (_dot_nt(f["q"].astype(BF16), f["kb"]) * f["decay"]).astype(BF16)
        return dict(w_k=sol[:, :hd], u_v=sol[:, hd:], attn=attn, qd=f["q"] * f["eg"],
                    kt=(f["k"] * f["etail"]).astype(BF16))

    sv = [solve(f, inv) for f, inv in zip(fr, invs)]

    pair_ids = [(h, s) for h in range(GDN_HEADS) for s in range(seqs)]

    def read_state(g, t):
        parts = []
        for h, s in pair_ids:
            r0 = h * gr + s * c
            s_old = s_ref[g, s * nh + h]
            lhs = jnp.concatenate([t["w_k"][r0:r0 + c], t["qd"][r0:r0 + c]], axis=0)
            res = _dot(lhs.astype(BF16), s_old.astype(BF16))
            parts.append((s_old, t["u_v"][r0:r0 + c] - res[:c], res[c:]))
        return parts

    rd = [read_state(g, t) for g, t in zip(groups, sv)]
    us = [jnp.concatenate([p[1] for p in parts], axis=0) for parts in rd]
    outs = [jnp.concatenate([p[2] for p in parts], axis=0) + _dot(t["attn"], u.astype(BF16))
            for parts, t, u in zip(rd, sv, us)]
    for g in groups:
        for (h, s), (s_old, u_p, _) in zip(pair_ids, rd[g]):
            r0 = h * gr + s * c
            cd = fr[g]["cd_all"][s * c:s * c + 1, nh + h:nh + h + 1]
            s_ref[g, s * nh + h] = s_old * cd + _dot_tn(sv[g]["kt"][r0:r0 + c],
                                                        u_p.astype(BF16))
    for g in groups:
        zs = stack_heads(z_ref[g], 0)
        o_n = _rms(outs[g], nw_ref[...]) * _silu(zs)
        oa_ref[g] = jnp.concatenate([o_n[h * gr:(h + 1) * gr] for h in range(GDN_HEADS)],
                                    axis=1).astype(oa_ref.dtype)

    @pl.when(n == pl.num_programs(1) - 1)
    def _():
        st_ref[...] = s_ref[...]


def _mixer(qkv, z, ba, sc, conv_w, gparams, norm_w, sc_conv_w, *, c, ng, state=None,
           row_lo=0, row_hi=None):
    g_total, l, _ = qkv.shape
    gr = GROUP_ROWS
    assert l % gr == 0 and gr % c == 0 and g_total % ng == 0
    nc = l // gr
    pairs = GDN_HEADS * (gr // c)
    row_hi = c if row_hi is None else row_hi
    blk = lambda w: pl.BlockSpec((ng, gr, w), lambda i, n: (i, n, 0))
    in_specs = [blk(qkv.shape[2]), blk(z.shape[2]), blk(ba.shape[2]), blk(sc.shape[2])]
    args = [qkv, z, ba, sc]
    st_shape = (g_total, pairs, GDN_HEAD_DIM, GDN_HEAD_DIM)
    if state is not None:
        assert nc == 1
        qh, sh, s0, layer = state
        assert s0.shape[1:] == st_shape
        in_specs += [pl.BlockSpec((ng, gr, qh.shape[2]), lambda i, n: (i, 0, 0)),
                     pl.BlockSpec((ng, gr, sh.shape[2]), lambda i, n: (i, 0, 0)),
                     pl.BlockSpec((1, ng) + st_shape[1:], lambda i, n: (layer, i, 0, 0, 0))]
        args += [qh, sh, s0]
    const = lambda a: pl.BlockSpec(a.shape, lambda i, n: (0,) * a.ndim)
    norm_w = norm_w.reshape(1, -1)
    in_specs += [const(conv_w), const(gparams), const(norm_w), const(sc_conv_w)]
    args += [conv_w, gparams, norm_w, sc_conv_w]
    out_shape = [jax.ShapeDtypeStruct((g_total, l, GDN_WIDTH), BF16),
                 jax.ShapeDtypeStruct((g_total, l, SC_WIDTH), BF16),
                 jax.ShapeDtypeStruct(st_shape, F32),
                 jax.ShapeDtypeStruct((g_total, l, SC_WIDTH), F32)]
    out_specs = [pl.BlockSpec((ng, gr, GDN_WIDTH), lambda i, n: (i, n, 0)),
                 pl.BlockSpec((ng, gr, SC_WIDTH), lambda i, n: (i, n, 0)),
                 pl.BlockSpec((ng,) + st_shape[1:], lambda i, n: (i, 0, 0, 0)),
                 pl.BlockSpec((ng, gr, SC_WIDTH), lambda i, n: (i, n, 0))]
    body = functools.partial(_mixer_body, c=c, ng=ng, has_state=state is not None,
                             row_lo=row_lo, row_hi=row_hi)
    return pl.pallas_call(
        body,
        name="seq_mixer",
        grid=(g_total // ng, nc),
        in_specs=in_specs,
        out_specs=out_specs,
        out_shape=out_shape,
        scratch_shapes=[pltpu.VMEM((ng, gr + SUBLANES, qkv.shape[2]), F32),
                        pltpu.VMEM((ng, gr + SUBLANES, SC_WIDTH), F32),
                        pltpu.VMEM((ng,) + st_shape[1:], F32)],
        compiler_params=_params("arbitrary", "arbitrary"),
    )(*args)


def _select_blocks(gate, n_valid, axis=0):
    nb = gate.shape[axis]
    sub = lax.broadcasted_iota(jnp.int32, gate.shape, axis)
    gate = jnp.where(sub < n_valid, gate, NEG_INF)
    sel = jnp.zeros(gate.shape, F32)
    for j in range(nb):
        gj = gate[j:j + 1, :] if axis == 0 else gate[:, j:j + 1]
        beats = (gate > gj) | ((gate == gj) & (sub < j))
        cnt = jnp.sum(jnp.where(beats, 1.0, 0.0), axis=axis, keepdims=True)
        hit = (cnt < MOBA_TOPK) & (j < n_valid)
        sel = jnp.where((sub == j) & hit, 1.0, sel)
    return sel


def _moba_prompt_body(q_ref, k_ref, v_ref, o_ref, kmean_ref, kb_ref, vb_ref, sel_ref):
    i = pl.program_id(1)
    blk = MOBA_BLOCK
    hd = MOBA_HEAD_DIM
    w = MOBA_WIDTH
    nb = k_ref.shape[0] // blk

    @pl.when(i == 0)
    def _():
        for j in range(nb):
            kj = k_ref[j * blk:(j + 1) * blk, :]
            kmean_ref[j:j + 1, :] = jnp.mean(kj, axis=0, keepdims=True)
            kb_ref[j] = kj.astype(BF16)
            vb_ref[j] = v_ref[0, :, j * blk:(j + 1) * blk].astype(BF16)

    feat = lax.broadcasted_iota(jnp.int32, (w, blk), 0)
    key_i = lax.broadcasted_iota(jnp.int32, (blk, blk), 0)
    qry_i = lax.broadcasted_iota(jnp.int32, (blk, blk), 1)
    causal = key_i <= qry_i
    qt = q_ref[0] * (hd ** -0.5)
    heads = range(MOBA_HEADS)
    head_rows = [slice(h * hd, (h + 1) * hd) for h in heads]
    qms = [jnp.where((feat >= h * hd) & (feat < (h + 1) * hd), qt, 0.0) for h in heads]
    qmb = jnp.concatenate(qms, axis=1).astype(BF16)
    for h in heads:
        gate = _dot(kmean_ref[...], qms[h], HI)
        sel_ref[h] = _select_blocks(gate, i)

    def head_scores(j):
        s_all = _dot(kb_ref[j], qmb)
        return [s_all[:, h * blk:(h + 1) * blk] for h in heads]

    init = []
    for h, s in zip(heads, head_scores(i)):
        s = jnp.where(causal, s, NEG_INF)
        m0 = jnp.max(s, axis=0, keepdims=True)
        p = jnp.exp(s - m0)
        l0 = jnp.sum(p, axis=0, keepdims=True)
        init += [m0, l0, _dot(vb_ref[i, head_rows[h], :], p.astype(BF16))]

    def past_block(j, carry):
        new = []
        for h, s in zip(heads, head_scores(j)):
            m, l, acc = carry[3 * h:3 * h + 3]
            keep = sel_ref[h, pl.ds(j, 1), :] > 0.5
            s = jnp.where(keep, s, NEG_INF)
            m_new = jnp.maximum(m, jnp.max(s, axis=0, keepdims=True))
            alpha = jnp.exp(m - m_new)
            p = jnp.exp(s - m_new)
            l = alpha * l + jnp.sum(p, axis=0, keepdims=True)
            acc = alpha * acc + _dot(vb_ref[j, head_rows[h], :], p.astype(BF16))
            new += [m_new, l, acc]
        return tuple(new)

    final = lax.fori_loop(0, i, past_block, tuple(init))
    for h in heads:
        _, l, acc = final[3 * h:3 * h + 3]
        o_ref[0, head_rows[h], :] = (acc / l).astype(o_ref.dtype)


def _moba_prompt(qt, k, vt):
    b, w, l = qt.shape
    blk = MOBA_BLOCK
    assert l % blk == 0
    nb = l // blk
    return pl.pallas_call(
        _moba_prompt_body,
        name="moba_prompt",
        grid=(b, nb),
        in_specs=[pl.BlockSpec((1, w, blk), lambda i, t: (i, 0, t)),
                  pl.BlockSpec((l, w), lambda i, t: (i, 0)),
                  pl.BlockSpec((1, w, l), lambda i, t: (i, 0, 0))],
        out_specs=pl.BlockSpec((1, w, blk), lambda i, t: (i, 0, t)),
        out_shape=jax.ShapeDtypeStruct((b, w, l), BF16),
        scratch_shapes=[pltpu.VMEM((nb, w), F32), pltpu.VMEM((nb, blk, w), BF16),
                        pltpu.VMEM((nb, w, blk), BF16),
                        pltpu.VMEM((MOBA_HEADS, nb, blk), F32)],
        compiler_params=_params("arbitrary", "arbitrary"),
    )(qt, k, vt)


def _moba_sample_body(pt_ref, q_ref, kn_ref, vn_ref, *refs, n_pages):
    del pt_ref
    k_refs = refs[:n_pages]
    v_refs = refs[n_pages:2 * n_pages]
    o_ref = refs[2 * n_pages]
    hd = MOBA_HEAD_DIM
    r = S_ROWS
    w = MOBA_WIDTH
    ppb = MOBA_BLOCK // PAGE_SIZE
    nb = n_pages // ppb

    lane = lax.broadcasted_iota(jnp.int32, (r, w), 1)
    head_masks = [jnp.where((lane >= h * hd) & (lane < (h + 1) * hd), 1.0, 0.0)
                  for h in range(MOBA_HEADS)]
    q = q_ref[0] * (hd ** -0.5)
    q_bd = jnp.concatenate([q * hm for hm in head_masks], axis=0)
    q_bb = q_bd.astype(BF16)
    col = lax.broadcasted_iota(jnp.int32, (w, LANES), 1)

    kmean_t = jnp.zeros((w, LANES), F32)
    m_cols, l_cols, accs = [], [], []
    for j in range(nb):
        s_parts, ksum = [], None
        for pp in range(ppb):
            kp = k_refs[j * ppb + pp][0]
            part = jnp.sum(kp, axis=1, keepdims=True)
            ksum = part if ksum is None else ksum + part
            s_parts.append(_dot(q_bb, kp.astype(BF16)))
        kmean_t = jnp.where(col == j, ksum * (1.0 / MOBA_BLOCK), kmean_t)
        s = jnp.concatenate(s_parts, axis=-1)
        mj = jnp.max(s, axis=1, keepdims=True)
        e = jnp.exp(s - mj)
        m_cols.append(mj)
        l_cols.append(jnp.sum(e, axis=1, keepdims=True))
        acc = None
        for pp in range(ppb):
            vp = v_refs[j * ppb + pp][0].astype(BF16)
            part = _dot_nt(e[:, pp * PAGE_SIZE:(pp + 1) * PAGE_SIZE].astype(BF16), vp)
            acc = part if acc is None else acc + part
        accs.append(acc)

    gate = _dot(q_bd, kmean_t, HI)[:, :nb]
    sel = _select_blocks(gate, nb, axis=1) > 0.5
    m_all = jnp.concatenate(m_cols, axis=1)
    l_all = jnp.concatenate(l_cols, axis=1)

    s_own = _dot_nt(q_bb, kn_ref[0].astype(BF16))
    qt = lax.broadcasted_iota(jnp.int32, (MOBA_HEADS * r, r), 0) & (r - 1)
    kt = lax.broadcasted_iota(jnp.int32, (MOBA_HEADS * r, r), 1)
    own = (kt >= S_LO) & (kt <= qt)
    m_tot = jnp.maximum(jnp.max(jnp.where(sel, m_all, NEG_INF), axis=1, keepdims=True),
                        jnp.max(jnp.where(own, s_own, NEG_INF), axis=1, keepdims=True))
    wj = jnp.where(sel, jnp.exp(jnp.where(sel, m_all - m_tot, 0.0)), 0.0)
    p_own = jnp.where(own, jnp.exp(jnp.where(own, s_own - m_tot, 0.0)), 0.0)
    l_tot = (jnp.sum(wj * l_all, axis=1, keepdims=True)
             + jnp.sum(p_own, axis=1, keepdims=True))
    acc = _dot(p_own.astype(BF16), vn_ref[0].astype(BF16))
    for j in range(nb):
        acc = acc + wj[:, j:j + 1] * accs[j]
    o_bd = acc / l_tot
    o = None
    for h in range(MOBA_HEADS):
        part = o_bd[h * r:(h + 1) * r, :] * head_masks[h]
        o = part if o is None else o + part
    o_ref[0] = o.astype(o_ref.dtype)


def _moba_sample(q, k_new, v_new, cache_kt, cache_vt, page_table, base):
    b, r, w = q.shape
    page = cache_kt.shape[2]
    n_pages = page_table.shape[1]
    assert page == PAGE_SIZE and r == S_ROWS and cache_kt.shape[1] == w
    tile = pl.BlockSpec((1, r, w), lambda i, pt: (i, 0, 0))

    def page_spec(p):
        return pl.BlockSpec((1, w, page), lambda i, pt: (base + pt[i, p], 0, 0))

    grid_spec = pltpu.PrefetchScalarGridSpec(
        num_scalar_prefetch=1,
        grid=(b,),
        in_specs=([tile, tile, tile] + [page_spec(p) for p in range(n_pages)]
                  + [page_spec(p) for p in range(n_pages)]),
        out_specs=tile,
    )
    body = functools.partial(_moba_sample_body, n_pages=n_pages)
    return pl.pallas_call(
        body,
        name="moba_sample",
        grid_spec=grid_spec,
        out_shape=jax.ShapeDtypeStruct((b, r, w), BF16),
        compiler_params=_params("arbitrary"),
    )(page_table, q, k_new, v_new, *([cache_kt] * n_pages), *([cache_vt] * n_pages))


def _trunk(x, weights, norm_final, *, sample=None):
    b, l, d = x.shape
    depth = len(weights)
    x = x.reshape(b * l, d)
    gr = GROUP_ROWS
    states = []
    for layer, wts in enumerate(weights):
        x = _ffn(x, wts["n_ffn1"], wts["ffn1_g"], wts["ffn1_u"], wts["ffn1_d"])
        if sample is None:
            qkv, z, sc, ba, mk, mq, mv = _inproj(x, wts["n_mix"], wts["w_rows"], wts["w_qv_t"],
                                                 seq_len=l)
            grp = lambda a: a.reshape(b, l, a.shape[-1])
            o_a, o_b, s_new, gated = _mixer(
                grp(qkv), grp(z), grp(ba), grp(sc), wts["gdn_conv_w"], wts["gparams"],
                wts["gdn_norm_w"], wts["sc_conv_w"], c=GDN_CHUNK, ng=4 if b % 4 == 0 else 1)
            o_c = _moba_prompt(mq, mk, mv)
        else:
            qkv, z, sc, ba, mk, mq, mv = _inproj(x, wts["n_mix"], wts["w_rows"], wts["w_qv"])
            ngroups = b * l // gr
            grp = lambda a: a.reshape(ngroups, gr, a.shape[-1])
            state = (grp(sample["gdn_conv"][layer]), grp(sample["sconv"][layer]),
                     sample["gdn"], layer)
            o_a, o_b, s_new, gated = _mixer(
                grp(qkv), grp(z), grp(ba), grp(sc), wts["gdn_conv_w"], wts["gparams"],
                wts["gdn_norm_w"], wts["sc_conv_w"], c=l, ng=2 if ngroups % 2 == 0 else 1,
                state=state, row_lo=S_LO, row_hi=S_HI)
            seq = lambda a: a.reshape(b, l, a.shape[-1])
            o_c = _moba_sample(seq(mq), seq(mk), seq(mv), sample["cache_kt"], sample["cache_vt"],
                               sample["page_table"], layer * sample["n_pool"])
            o_c = o_c.reshape(b * l, -1)
        mix = (o_a.reshape(b * l, -1), o_b.reshape(b * l, -1), o_c, wts["w_out"])
        x = _ffn(x, wts["n_ffn2"], wts["ffn2_g"], wts["ffn2_u"], wts["ffn2_d"], mix=mix,
                 final_w=norm_final if layer == depth - 1 else None)
        states.append(dict(gdn=s_new, qkv=qkv.reshape(b, l, -1), gated=gated.reshape(b, l, -1),
                           k=mk.reshape(b, l, -1), v=mv))
    return x.reshape(b, l, d), states


def kernel(x_prompt, x_sample, state_gdn, state_gdn_conv, state_sconv, cache_k, cache_v,
           page_table, norm_ffn1, ffn1_w_gate, ffn1_w_up, ffn1_w_down, norm_mix, w_in,
           gdn_conv_w, gdn_a_log, gdn_dt_bias, gdn_norm_w, sc_conv_w, w_out,
           norm_ffn2, ffn2_w_gate, ffn2_w_up, ffn2_w_down, norm_final):
    depth = w_in.shape[0]
    gw, sw, mw = GDN_WIDTH, SC_WIDTH, MOBA_WIDTH
    nh = GDN_HEADS
    mh, md = MOBA_HEADS, MOBA_HEAD_DIM

    weights = []
    for la in range(depth):
        w = w_in[la]
        ba_off = 4 * gw
        sc_off = ba_off + 2 * nh
        mo_off = sc_off + 3 * sw
        w_rows = jnp.concatenate(
            [w[:, :ba_off], w[:, sc_off:mo_off],
             jnp.pad(w[:, ba_off:sc_off], ((0, 0), (0, LANES - 2 * nh))),
             w[:, mo_off + mw:mo_off + 2 * mw]], axis=1).astype(BF16)
        w_qv = jnp.concatenate([w[:, mo_off:mo_off + mw], w[:, mo_off + 2 * mw:mo_off + 3 * mw]],
                               axis=1).astype(BF16)
        gparams = jnp.zeros((SUBLANES, LANES), F32)
        gparams = gparams.at[0, nh:2 * nh].set(gdn_a_log[la].astype(F32))
        gparams = gparams.at[1, nh:2 * nh].set(gdn_dt_bias[la].astype(F32))
        weights.append(dict(
            n_ffn1=norm_ffn1[la], ffn1_g=ffn1_w_gate[la].astype(BF16),
            ffn1_u=ffn1_w_up[la].astype(BF16), ffn1_d=ffn1_w_down[la].astype(BF16),
            n_mix=norm_mix[la], w_rows=w_rows, w_qv=w_qv, w_qv_t=w_qv.T,
            gdn_conv_w=gdn_conv_w[la], gparams=gparams, gdn_norm_w=gdn_norm_w[la],
            sc_conv_w=sc_conv_w[la], w_out=w_out[la].astype(BF16),
            n_ffn2=norm_ffn2[la], ffn2_g=ffn2_w_gate[la].astype(BF16),
            ffn2_u=ffn2_w_up[la].astype(BF16), ffn2_d=ffn2_w_down[la].astype(BF16)))

    pb, pl_, _ = x_prompt.shape
    y_p, st_p = _trunk(x_prompt, weights, norm_final)
    hist = GDN_CONV - 1
    p_gdn = jnp.stack([s["gdn"] for s in st_p])
    p_conv = jnp.stack([s["qkv"][:, -hist:] for s in st_p])
    p_sconv = jnp.stack([s["gated"][:, -(SC_CONV - 1):] for s in st_p])
    p_k = jnp.stack([s["k"] for s in st_p]).reshape(depth, pb, pl_, mh, md)
    p_v = jnp.stack([s["v"] for s in st_p]).reshape(depth, pb, mh, md, pl_)
    p_v = p_v.transpose(0, 1, 4, 2, 3)

    db, dl, d = x_sample.shape
    assert dl == S_HI - S_LO and (db * S_ROWS) % GROUP_ROWS == 0
    n_pool = cache_k.shape[1]
    tile_pad = lambda a, lo: jnp.pad(a, ((0, 0),) * (a.ndim - 2)
                                     + ((lo, S_ROWS - lo - a.shape[-2]), (0, 0)))
    seqs_per_group = GROUP_ROWS // S_ROWS
    paged_t = lambda c: c.transpose(0, 1, 3, 4, 2).reshape(depth * n_pool, mw, PAGE_SIZE)
    sample = dict(
        gdn=state_gdn.reshape(depth, db // seqs_per_group, seqs_per_group * nh,
                              GDN_HEAD_DIM, GDN_HEAD_DIM),
        gdn_conv=tile_pad(state_gdn_conv, 0),
        sconv=tile_pad(state_sconv, S_LO - (SC_CONV - 1)),
        cache_kt=paged_t(cache_k), cache_vt=paged_t(cache_v), n_pool=n_pool,
        page_table=page_table)
    y_s, st_s = _trunk(tile_pad(x_sample, S_LO), weights, norm_final, sample=sample)
    y_s = y_s[:, S_LO:S_HI]
    s_gdn = jnp.stack([s["gdn"] for s in st_s]).reshape(state_gdn.shape)
    s_conv = jnp.stack([s["qkv"][:, S_HI - hist:S_HI] for s in st_s])
    s_sconv = jnp.stack([s["gated"][:, S_HI - (SC_CONV - 1):S_HI] for s in st_s])
    s_k = jnp.stack([s["k"][:, S_LO:S_HI] for s in st_s]).reshape(depth, db, dl, mh, md)
    s_v = jnp.stack([s["v"].reshape(db, S_ROWS, mw)[:, S_LO:S_HI] for s in st_s]).reshape(
        depth, db, dl, mh, md)
    return (y_p, y_s, p_gdn, p_conv, p_sconv, p_k, p_v, s_gdn, s_conv, s_sconv, s_k, s_v)
```

```python
import functools

import jax
import jax.numpy as jnp
from jax import lax
from jax.experimental import pallas as pl
from jax.experimental.pallas import tpu as pltpu

NORM_EPS = 1e-6
NEG_INF = -1e30

GDN_HEADS = 4
GDN_HEAD_DIM = 128
GDN_WIDTH = GDN_HEADS * GDN_HEAD_DIM
GDN_CONV = 4
GDN_CHUNK = 64
SC_WIDTH = 256
SC_CONV = 3
MOBA_HEADS = 4
MOBA_HEAD_DIM = 64
MOBA_WIDTH = MOBA_HEADS * MOBA_HEAD_DIM
MOBA_BLOCK = 256
MOBA_TOPK = 3
PAGE_SIZE = 128

LANES = 128
SUBLANES = 8
VMEM_LIMIT_BYTES = 56 * 1024 * 1024

S_ROWS = SUBLANES
S_LO = GDN_CONV - 1
S_HI = S_LO + 4

GROUP_ROWS = GDN_CHUNK
STACK_ROWS = GDN_HEADS * GROUP_ROWS

F32 = jnp.float32
BF16 = jnp.bfloat16
HI = lax.Precision.HIGHEST


def _dot(a, b, precision=None):
    return jnp.dot(a, b, preferred_element_type=F32, precision=precision)


def _dot_nt(a, b, precision=None):
    return lax.dot_general(a, b, (((1,), (1,)), ((), ())),
                           preferred_element_type=F32, precision=precision)


def _dot_tn(a, b, precision=None):
    return lax.dot_general(a, b, (((0,), (0,)), ((), ())),
                           preferred_element_type=F32, precision=precision)


def _rms(x, w):
    return x * lax.rsqrt(jnp.mean(x * x, axis=-1, keepdims=True) + NORM_EPS) * w


def _silu(x):
    return x * jax.nn.sigmoid(x)


def _softplus(x):
    return jnp.maximum(x, 0.0) + jnp.log1p(jnp.exp(-jnp.abs(x)))


def _resident(shape):
    nd = len(shape)
    return pl.BlockSpec(shape, lambda *_: (0,) * nd, pipeline_mode=pl.Buffered(1))


def _params(*semantics):
    return pltpu.CompilerParams(dimension_semantics=semantics,
                                vmem_limit_bytes=VMEM_LIMIT_BYTES)


def _cast_body(w_ref, o_ref):
    o_ref[...] = w_ref[0].astype(o_ref.dtype)


def _to_bf16(w, layer, rows=256):
    _, r, c = w.shape
    rows = min(rows, r)
    assert r % rows == 0
    return pl.pallas_call(
        _cast_body,
        name="to_bf16",
        grid=(r // rows,),
        in_specs=[pl.BlockSpec((1, rows, c), lambda i: (layer, i, 0))],
        out_specs=pl.BlockSpec((rows, c), lambda i: (i, 0)),
        out_shape=jax.ShapeDtypeStruct((r, c), BF16),
        compiler_params=_params("arbitrary"),
    )(w)


def _ffn_body(*refs, has_mix, oc_transposed, has_final, f_chunk):
    refs = list(refs)
    x_ref = refs.pop(0)
    if has_mix:
        oa_ref, ob_ref, oc_ref, wo_ref = refs[:4]
        refs = refs[4:]
    nw_ref, wg_ref, wu_ref, wd_ref = refs[:4]
    refs = refs[4:]
    if has_final:
        nf_ref = refs.pop(0)
    out_ref, xn_ref, acc_ref = refs

    x = x_ref[...]
    if has_mix:
        ga = oa_ref.shape[1]
        gb = ga + ob_ref.shape[1]
        x = x + _dot(oa_ref[...], wo_ref[0:ga, :]) + _dot(ob_ref[...], wo_ref[ga:gb, :])
        if oc_transposed:
            x = x + _dot_tn(oc_ref[0], wo_ref[gb:, :])
        else:
            x = x + _dot(oc_ref[...], wo_ref[gb:, :])
    xn_ref[...] = _rms(x, nw_ref[...]).astype(BF16)
    d_ff = wg_ref.shape[1]
    for c in range(d_ff // f_chunk):
        lo = c * f_chunk
        xn = xn_ref[...]
        g = _dot(xn, wg_ref[:, lo:lo + f_chunk])
        u = _dot(xn, wu_ref[:, lo:lo + f_chunk])
        h = (_silu(g) * u).astype(BF16)
        part = _dot(h, wd_ref[lo:lo + f_chunk, :])
        if c == 0:
            acc_ref[...] = part
        else:
            acc_ref[...] += part
    y = x + 0.5 * acc_ref[...]
    if has_final:
        y = _rms(y, nf_ref[...])
    out_ref[...] = y


def _ffn(x, nw, wg, wu, wd, mix=None, final_w=None, tm=512, f_chunk=256):
    m, d = x.shape
    d_ff = wg.shape[1]
    tm = min(tm, m)
    assert m % tm == 0 and d_ff % f_chunk == 0
    row = lambda i: (i, 0)
    in_specs = [pl.BlockSpec((tm, d), row)]
    args = [x]
    oc_transposed = False
    if mix is not None:
        oa, ob, oc, wo = mix
        oc_transposed = oc.ndim == 3
        if oc_transposed:
            tiles = oc.shape[2] // tm
            assert oc.shape[2] % tm == 0
            oc_spec = pl.BlockSpec((1, oc.shape[1], tm), lambda i: (i // tiles, 0, i % tiles))
        else:
            oc_spec = pl.BlockSpec((tm, oc.shape[1]), row)
        in_specs += [pl.BlockSpec((tm, oa.shape[1]), row), pl.BlockSpec((tm, ob.shape[1]), row),
                     oc_spec, _resident(wo.shape)]
        args += [oa, ob, oc, wo]
    in_specs += [_resident((1, d)), _resident(wg.shape), _resident(wu.shape), _resident(wd.shape)]
    args += [nw.reshape(1, d), wg, wu, wd]
    if final_w is not None:
        in_specs.append(_resident((1, d)))
        args.append(final_w.reshape(1, d))
    body = functools.partial(_ffn_body, has_mix=mix is not None, oc_transposed=oc_transposed,
                             has_final=final_w is not None, f_chunk=f_chunk)
    return pl.pallas_call(
        body,
        name="ffn_mix" if mix is not None else "ffn",
        grid=(m // tm,),
        in_specs=in_specs,
        out_specs=pl.BlockSpec((tm, d), row),
        out_shape=jax.ShapeDtypeStruct((m, d), F32),
        scratch_shapes=[pltpu.VMEM((tm, d), BF16), pltpu.VMEM((tm, d), F32)],
        compiler_params=_params("arbitrary"),
    )(*args)


_ROW_WIDTHS = (3 * GDN_WIDTH, GDN_WIDTH, 3 * SC_WIDTH, LANES, MOBA_WIDTH)


def _inproj_body(x_ref, nw_ref, w_ref, wqv_ref, *out_refs, transposed_qv):
    xn = _rms(x_ref[...], nw_ref[...]).astype(BF16)
    lo = 0
    for ref in out_refs[:len(_ROW_WIDTHS)]:
        width = ref.shape[1]
        ref[...] = _dot(xn, w_ref[:, lo:lo + width])
        lo += width
    q_ref, v_ref = out_refs[len(_ROW_WIDTHS):]
    mw = MOBA_WIDTH
    if transposed_qv:
        q_ref[0] = _dot_nt(wqv_ref[0:mw, :], xn)
        v_ref[0] = _dot_nt(wqv_ref[mw:2 * mw, :], xn)
    else:
        q_ref[...] = _dot(xn, wqv_ref[:, 0:mw])
        v_ref[...] = _dot(xn, wqv_ref[:, mw:2 * mw])


def _inproj(x, nw, w_rows, w_qv, *, seq_len=None, tm=512):
    m, d = x.shape
    tm = min(tm, m)
    assert m % tm == 0 and w_rows.shape[1] == sum(_ROW_WIDTHS)
    row = lambda i: (i, 0)
    out_specs = [pl.BlockSpec((tm, wd), row) for wd in _ROW_WIDTHS]
    out_shape = [jax.ShapeDtypeStruct((m, wd), F32) for wd in _ROW_WIDTHS]
    transposed_qv = seq_len is not None
    if transposed_qv:
        assert seq_len % tm == 0
        tiles = seq_len // tm
        spec = pl.BlockSpec((1, MOBA_WIDTH, tm), lambda i: (i // tiles, 0, i % tiles))
        shape = jax.ShapeDtypeStruct((m // seq_len, MOBA_WIDTH, seq_len), F32)
    else:
        spec = pl.BlockSpec((tm, MOBA_WIDTH), row)
        shape = jax.ShapeDtypeStruct((m, MOBA_WIDTH), F32)
    out_specs += [spec, spec]
    out_shape += [shape, shape]
    return pl.pallas_call(
        functools.partial(_inproj_body, transposed_qv=transposed_qv),
        name="inproj",
        grid=(m // tm,),
        in_specs=[pl.BlockSpec((tm, d), row), _resident((1, d)), _resident(w_rows.shape),
                  _resident(w_qv.shape)],
        out_specs=out_specs,
        out_shape=out_shape,
        compiler_params=_params("arbitrary"),
    )(x, nw.reshape(1, d), w_rows, w_qv)


def _mixer_body(*refs, c, ng, has_state, row_lo, row_hi):
    refs = list(refs)
    qkv_ref, z_ref, ba_ref, sc_ref = refs[:4]
    refs = refs[4:]
    if has_state:
        qh_ref, sh_ref, s0_ref = refs[:3]
        refs = refs[3:]
    cw_ref, gp_ref, nw_ref, scw_ref = refs[:4]
    oa_ref, ob_ref, st_ref, gt_ref = refs[4:8]
    xbuf, gbuf, s_ref = refs[8:]
    n = pl.program_id(1)
    hd = GDN_HEAD_DIM
    gr = GROUP_ROWS
    sr = STACK_ROWS
    pad = SUBLANES
    seqs = gr // c
    shift = c.bit_length() - 1
    assert 1 << shift == c

    @pl.when(n == 0)
    def _():
        xbuf[:, 0:pad, :] = jnp.zeros((ng, pad, xbuf.shape[2]), F32)
        gbuf[:, 0:pad, :] = jnp.zeros((ng, pad, gbuf.shape[2]), F32)
        if has_state:
            s_ref[...] = s0_ref[0]
        else:
            s_ref[...] = jnp.zeros(s_ref.shape, F32)

    rows = lax.broadcasted_iota(jnp.int32, (gr, 1), 0) & (c - 1)
    is_hist = rows < row_lo
    live = jnp.where((rows >= row_lo) & (rows < row_hi), 1.0, 0.0)
    ri = lax.broadcasted_iota(jnp.int32, (gr, gr), 0)
    ci = lax.broadcasted_iota(jnp.int32, (gr, gr), 1)
    same = (ri >> shift) == (ci >> shift)
    cum_op = jnp.concatenate([jnp.where(same & (ri >= ci), 1.0, 0.0),
                              jnp.where(same, 1.0, 0.0)], axis=0)
    rs = lax.broadcasted_iota(jnp.int32, (sr, sr), 0)
    cs = lax.broadcasted_iota(jnp.int32, (sr, sr), 1)
    same_s = (rs >> shift) == (cs >> shift)
    incl = same_s & (rs >= cs)
    strict = same_s & (rs > cs)
    eye = jnp.where(rs == cs, 1.0, 0.0)

    def stack_heads(a, lo):
        return jnp.concatenate([a[:, lo + h * hd:lo + (h + 1) * hd] for h in range(GDN_HEADS)],
                               axis=0)

    def stack_cols(a, lo):
        return jnp.concatenate([a[:, lo + h:lo + h + 1] for h in range(GDN_HEADS)], axis=0)

    nh = GDN_HEADS
    groups = range(ng)

    def front(g):
        x = qkv_ref[g]
        if has_state:
            x = jnp.where(is_hist, qh_ref[g], x)
        xbuf[g, pad:pad + gr, :] = x
        y = xbuf[g, pad - 3:pad - 3 + gr, :] * cw_ref[0:1, :]
        y = y + xbuf[g, pad - 2:pad - 2 + gr, :] * cw_ref[1:2, :]
        y = y + xbuf[g, pad - 1:pad - 1 + gr, :] * cw_ref[2:3, :]
        y = y + x * cw_ref[3:4, :]
        tail = xbuf[g, gr:gr + pad, :]
        xbuf[g, 0:pad, :] = tail
        act = _silu(y)

        sc = sc_ref[g]
        scw = SC_WIDTH
        gated = sc[:, scw:2 * scw] * sc[:, 0:scw]
        if has_state:
            gated = jnp.where(is_hist, sh_ref[g], gated)
        gbuf[g, pad:pad + gr, :] = gated
        yb = gbuf[g, pad - 2:pad - 2 + gr, :] * scw_ref[0:1, :]
        yb = yb + gbuf[g, pad - 1:pad - 1 + gr, :] * scw_ref[1:2, :]
        yb = yb + gated * scw_ref[2:3, :]
        ob_ref[g] = (sc[:, 2 * scw:3 * scw] * yb).astype(ob_ref.dtype)
        gtail = gbuf[g, gr:gr + pad, :]
        gbuf[g, 0:pad, :] = gtail
        gt_ref[g] = gated

        ba = ba_ref[g]
        beta_all = jax.nn.sigmoid(ba) * live
        g_all = -jnp.exp(gp_ref[0:1, :]) * _softplus(ba + gp_ref[1:2, :]) * live
        cum = _dot(cum_op, g_all, HI)
        gc_all, glast_all = cum[:gr], cum[gr:]
        beta = stack_cols(beta_all, 0)
        gc = stack_cols(gc_all, nh)
        eg = jnp.exp(gc)
        etail = jnp.exp(stack_cols(glast_all, nh) - gc)
        gc_row = jnp.broadcast_to(gc, (sr, LANES)).T[0:1, :]
        decay = jnp.where(incl, jnp.exp(jnp.where(incl, gc - gc_row, 0.0)), 0.0)

        xq = stack_heads(act, 0)
        xk = stack_heads(act, GDN_WIDTH)
        v = stack_heads(act, 2 * GDN_WIDTH)
        q = xq * lax.rsqrt(jnp.sum(xq * xq, axis=-1, keepdims=True) + NORM_EPS) * (hd ** -0.5)
        k = xk * lax.rsqrt(jnp.sum(xk * xk, axis=-1, keepdims=True) + NORM_EPS)
        kb = k.astype(BF16)
        m_strict = jnp.where(strict, beta * _dot_nt(kb, kb) * decay, 0.0)
        return dict(q=q, k=k, v=v, kb=kb, beta=beta, eg=eg, etail=etail, decay=decay,
                    cd_all=jnp.exp(glast_all), m_strict=m_strict)

    fr = [front(g) for g in groups]

    invs = [eye - f["m_strict"] for f in fr]
    ps = [f["m_strict"] for f in fr]
    for _ in range(shift - 1):
        pbs = [p.astype(BF16) for p in ps]
        ps = [_dot(pb, pb) for pb in pbs]
        invs = [inv + _dot(inv.astype(BF16), p.astype(BF16)) for inv, p in zip(invs, ps)]

    def solve(f, inv):
        rhs = jnp.concatenate([f["k"] * (f["beta"] * f["eg"]), f["v"] * f["beta"]],
                              axis=-1).astype(BF16)
        sol = _dot(inv.astype(BF16), rhs)
        attn = (_dot_nt(f["q"].astype(BF16), f["kb"]) * f["decay"]).astype(BF16)
        return dict(w_k=sol[:, :hd], u_v=sol[:, hd:], attn=attn, qd=f["q"] * f["eg"],
                    kt=(f["k"] * f["etail"]).astype(BF16))

    sv = [solve(f, inv) for f, inv in zip(fr, invs)]

    pair_ids = [(h, s) for h in range(GDN_HEADS) for s in range(seqs)]

    def read_state(g, t):
        parts = []
        for h, s in pair_ids:
            r0 = h * gr + s * c
            s_old = s_ref[g, s * nh + h]
            lhs = jnp.concatenate([t["w_k"][r0:r0 + c], t["qd"][r0:r0 + c]], axis=0)
            res = _dot(lhs.astype(BF16), s_old.astype(BF16))
            parts.append((s_old, t["u_v"][r0:r0 + c] - res[:c], res[c:]))
        return parts

    rd = [read_state(g, t) for g, t in zip(groups, sv)]
    us = [jnp.concatenate([p[1] for p in parts], axis=0) for parts in rd]
    outs = [jnp.concatenate([p[2] for p in parts], axis=0) + _dot(t["attn"], u.astype(BF16))
            for parts, t, u in zip(rd, sv, us)]
    for g in groups:
        for (h, s), (s_old, u_p, _) in zip(pair_ids, rd[g]):
            r0 = h * gr + s * c
            cd = fr[g]["cd_all"][s * c:s * c + 1, nh + h:nh + h + 1]
            s_ref[g, s * nh + h] = s_old * cd + _dot_tn(sv[g]["kt"][r0:r0 + c],
                                                        u_p.astype(BF16))
    for g in groups:
        zs = stack_heads(z_ref[g], 0)
        o_n = _rms(outs[g], nw_ref[...]) * _silu(zs)
        oa_ref[g] = jnp.concatenate([o_n[h * gr:(h + 1) * gr] for h in range(GDN_HEADS)],
                                    axis=1).astype(oa_ref.dtype)

    @pl.when(n == pl.num_programs(1) - 1)
    def _():
        st_ref[...] = s_ref[...]


def _mixer(qkv, z, ba, sc, conv_w, gparams, norm_w, sc_conv_w, *, c, ng, state=None,
           row_lo=0, row_hi=None):
    g_total, l, _ = qkv.shape
    gr = GROUP_ROWS
    assert l % gr == 0 and gr % c == 0 and g_total % ng == 0
    nc = l // gr
    pairs = GDN_HEADS * (gr // c)
    row_hi = c if row_hi is None else row_hi
    blk = lambda w: pl.BlockSpec((ng, gr, w), lambda i, n: (i, n, 0))
    in_specs = [blk(qkv.shape[2]), blk(z.shape[2]), blk(ba.shape[2]), blk(sc.shape[2])]
    args = [qkv, z, ba, sc]
    st_shape = (g_total, pairs, GDN_HEAD_DIM, GDN_HEAD_DIM)
    if state is not None:
        assert nc == 1
        qh, sh, s0, layer = state
        assert s0.shape[1:] == st_shape
        in_specs += [pl.BlockSpec((ng, gr, qh.shape[2]), lambda i, n: (i, 0, 0)),
                     pl.BlockSpec((ng, gr, sh.shape[2]), lambda i, n: (i, 0, 0)),
                     pl.BlockSpec((1, ng) + st_shape[1:], lambda i, n: (layer, i, 0, 0, 0))]
        args += [qh, sh, s0]
    const = lambda a: pl.BlockSpec(a.shape, lambda i, n: (0,) * a.ndim)
    norm_w = norm_w.reshape(1, -1)
    in_specs += [const(conv_w), const(gparams), const(norm_w), const(sc_conv_w)]
    args += [conv_w, gparams, norm_w, sc_conv_w]
    out_shape = [jax.ShapeDtypeStruct((g_total, l, GDN_WIDTH), BF16),
                 jax.ShapeDtypeStruct((g_total, l, SC_WIDTH), BF16),
                 jax.ShapeDtypeStruct(st_shape, F32),
                 jax.ShapeDtypeStruct((g_total, l, SC_WIDTH), F32)]
    out_specs = [pl.BlockSpec((ng, gr, GDN_WIDTH), lambda i, n: (i, n, 0)),
                 pl.BlockSpec((ng, gr, SC_WIDTH), lambda i, n: (i, n, 0)),
                 pl.BlockSpec((ng,) + st_shape[1:], lambda i, n: (i, 0, 0, 0)),
                 pl.BlockSpec((ng, gr, SC_WIDTH), lambda i, n: (i, n, 0))]
    body = functools.partial(_mixer_body, c=c, ng=ng, has_state=state is not None,
                             row_lo=row_lo, row_hi=row_hi)
    return pl.pallas_call(
        body,
        name="seq_mixer",
        grid=(g_total // ng, nc),
        in_specs=in_specs,
        out_specs=out_specs,
        out_shape=out_shape,
        scratch_shapes=[pltpu.VMEM((ng, gr + SUBLANES, qkv.shape[2]), F32),
                        pltpu.VMEM((ng, gr + SUBLANES, SC_WIDTH), F32),
                        pltpu.VMEM((ng,) + st_shape[1:], F32)],
        compiler_params=_params("arbitrary", "arbitrary"),
    )(*args)


def _select_blocks(gate, n_valid, axis=0):
    nb = gate.shape[axis]
    sub = lax.broadcasted_iota(jnp.int32, gate.shape, axis)
    gate = jnp.where(sub < n_valid, gate, NEG_INF)
    sel = jnp.zeros(gate.shape, F32)
    for j in range(nb):
        gj = gate[j:j + 1, :] if axis == 0 else gate[:, j:j + 1]
        beats = (gate > gj) | ((gate == gj) & (sub < j))
        cnt = jnp.sum(jnp.where(beats, 1.0, 0.0), axis=axis, keepdims=True)
        hit = (cnt < MOBA_TOPK) & (j < n_valid)
        sel = jnp.where((sub == j) & hit, 1.0, sel)
    return sel


def _moba_prompt_body(q_ref, k_ref, v_ref, o_ref, kmean_ref, kb_ref, vb_ref, sel_ref):
    i = pl.program_id(1)
    blk = MOBA_BLOCK
    hd = MOBA_HEAD_DIM
    w = MOBA_WIDTH
    nb = k_ref.shape[0] // blk

    @pl.when(i == 0)
    def _():
        for j in range(nb):
            kj = k_ref[j * blk:(j + 1) * blk, :]
            kmean_ref[j:j + 1, :] = jnp.mean(kj, axis=0, keepdims=True)
            kb_ref[j] = kj.astype(BF16)
            vb_ref[j] = v_ref[0, :, j * blk:(j + 1) * blk].astype(BF16)

    feat = lax.broadcasted_iota(jnp.int32, (w, blk), 0)
    key_i = lax.broadcasted_iota(jnp.int32, (blk, blk), 0)
    qry_i = lax.broadcasted_iota(jnp.int32, (blk, blk), 1)
    causal = key_i <= qry_i
    qt = q_ref[0] * (hd ** -0.5)
    heads = range(MOBA_HEADS)
    head_rows = [slice(h * hd, (h + 1) * hd) for h in heads]
    qms = [jnp.where((feat >= h * hd) & (feat < (h + 1) * hd), qt, 0.0) for h in heads]
    qmb = jnp.concatenate(qms, axis=1).astype(BF16)
    for h in heads:
        gate = _dot(kmean_ref[...], qms[h], HI)
        sel_ref[h] = _select_blocks(gate, i)

    def head_scores(j):
        s_all = _dot(kb_ref[j], qmb)
        return [s_all[:, h * blk:(h + 1) * blk] for h in heads]

    init = []
    for h, s in zip(heads, head_scores(i)):
        s = jnp.where(causal, s, NEG_INF)
        m0 = jnp.max(s, axis=0, keepdims=True)
        p = jnp.exp(s - m0)
        l0 = jnp.sum(p, axis=0, keepdims=True)
        init += [m0, l0, _dot(vb_ref[i, head_rows[h], :], p.astype(BF16))]

    def past_block(j, carry):
        new = []
        for h, s in zip(heads, head_scores(j)):
            m, l, acc = carry[3 * h:3 * h + 3]
            keep = sel_ref[h, pl.ds(j, 1), :] > 0.5
            s = jnp.where(keep, s, NEG_INF)
            m_new = jnp.maximum(m, jnp.max(s, axis=0, keepdims=True))
            alpha = jnp.exp(m - m_new)
            p = jnp.exp(s - m_new)
            l = alpha * l + jnp.sum(p, axis=0, keepdims=True)
            acc = alpha * acc + _dot(vb_ref[j, head_rows[h], :], p.astype(BF16))
            new += [m_new, l, acc]
        return tuple(new)

    final = lax.fori_loop(0, i, past_block, tuple(init))
    for h in heads:
        _, l, acc = final[3 * h:3 * h + 3]
        o_ref[0, head_rows[h], :] = (acc / l).astype(o_ref.dtype)


def _moba_prompt(qt, k, vt):
    b, w, l = qt.shape
    blk = MOBA_BLOCK
    assert l % blk == 0
    nb = l // blk
    return pl.pallas_call(
        _moba_prompt_body,
        name="moba_prompt",
        grid=(b, nb),
        in_specs=[pl.BlockSpec((1, w, blk), lambda i, t: (i, 0, t)),
                  pl.BlockSpec((l, w), lambda i, t: (i, 0)),
                  pl.BlockSpec((1, w, l), lambda i, t: (i, 0, 0))],
        out_specs=pl.BlockSpec((1, w, blk), lambda i, t: (i, 0, t)),
        out_shape=jax.ShapeDtypeStruct((b, w, l), BF16),
        scratch_shapes=[pltpu.VMEM((nb, w), F32), pltpu.VMEM((nb, blk, w), BF16),
                        pltpu.VMEM((nb, w, blk), BF16),
                        pltpu.VMEM((MOBA_HEADS, nb, blk), F32)],
        compiler_params=_params("arbitrary", "arbitrary"),
    )(qt, k, vt)


PAGE_SLOTS = 3


def _moba_sample_body(pt_ref, q_ref, kn_ref, vn_ref, ck_ref, cv_ref, o_ref, kbuf, vbuf, sem,
                      *, n_pages, base, n_seq):
    i = pl.program_id(0)
    hd = MOBA_HEAD_DIM
    r = S_ROWS
    w = MOBA_WIDTH
    ppb = MOBA_BLOCK // PAGE_SIZE
    nb = n_pages // ppb
    ahead = PAGE_SLOTS - 1

    def page_copies(seq, slot):
        cps = []
        for p in range(n_pages):
            row = base + pt_ref[seq, p]
            cps.append(pltpu.make_async_copy(ck_ref.at[row], kbuf.at[slot, p], sem.at[0, slot]))
            cps.append(pltpu.make_async_copy(cv_ref.at[row], vbuf.at[slot, p], sem.at[1, slot]))
        return cps

    @pl.when(i == 0)
    def _():
        for d in range(ahead):
            for cp in page_copies(d, d):
                cp.start()

    slot = lax.rem(i, PAGE_SLOTS)
    for cp in page_copies(i, slot):
        cp.wait()

    lane = lax.broadcasted_iota(jnp.int32, (r, w), 1)
    head_masks = [jnp.where((lane >= h * hd) & (lane < (h + 1) * hd), 1.0, 0.0)
                  for h in range(MOBA_HEADS)]
    q = q_ref[0] * (hd ** -0.5)
    q_bd = jnp.concatenate([q * hm for hm in head_masks], axis=0)
    q_bb = q_bd.astype(BF16)
    col = lax.broadcasted_iota(jnp.int32, (w, LANES), 1)
    blocks = range(nb)

    kp = [kbuf[slot, p] for p in range(n_pages)]
    s_pages = [_dot(q_bb, x.astype(BF16)) for x in kp]
    kmean_t = jnp.zeros((w, LANES), F32)
    for j in blocks:
        ksum = kp[j * ppb]
        for pp in range(1, ppb):
            ksum = ksum + kp[j * ppb + pp]
        kmean_t = jnp.where(col == j, jnp.sum(ksum, axis=1, keepdims=True) * (1.0 / MOBA_BLOCK),
                            kmean_t)
    s_blk = [jnp.concatenate(s_pages[j * ppb:(j + 1) * ppb], axis=-1) for j in blocks]
    m_cols = [jnp.max(s, axis=1, keepdims=True) for s in s_blk]
    e_blk = [jnp.exp(s - m) for s, m in zip(s_blk, m_cols)]
    l_cols = [jnp.sum(e, axis=1, keepdims=True) for e in e_blk]
    accs = []
    for j in blocks:
        eb = e_blk[j].astype(BF16)
        acc = None
        for pp in range(ppb):
            vp = vbuf[slot, j * ppb + pp].astype(BF16)
            part = _dot_nt(eb[:, pp * PAGE_SIZE:(pp + 1) * PAGE_SIZE], vp)
            acc = part if acc is None else acc + part
        accs.append(acc)

    nxt = jnp.minimum(i + ahead, n_seq - 1)
    for cp in page_copies(nxt, lax.rem(i + ahead, PAGE_SLOTS)):
        cp.start()

    gate = _dot(q_bd, kmean_t, HI)[:, :nb]
    sel = _select_blocks(gate, nb, axis=1) > 0.5
    m_all = jnp.concatenate(m_cols, axis=1)
    l_all = jnp.concatenate(l_cols, axis=1)

    s_own = _dot_nt(q_bb, kn_ref[0].astype(BF16))
    qt = lax.broadcasted_iota(jnp.int32, (MOBA_HEADS * r, r), 0) & (r - 1)
    kt = lax.broadcasted_iota(jnp.int32, (MOBA_HEADS * r, r), 1)
    own = (kt >= S_LO) & (kt <= qt)
    m_tot = jnp.maximum(jnp.max(jnp.where(sel, m_all, NEG_INF), axis=1, keepdims=True),
                        jnp.max(jnp.where(own, s_own, NEG_INF), axis=1, keepdims=True))
    wj = jnp.where(sel, jnp.exp(jnp.where(sel, m_all - m_tot, 0.0)), 0.0)
    p_own = jnp.where(own, jnp.exp(jnp.where(own, s_own - m_tot, 0.0)), 0.0)
    l_tot = (jnp.sum(wj * l_all, axis=1, keepdims=True)
             + jnp.sum(p_own, axis=1, keepdims=True))
    acc = _dot(p_own.astype(BF16), vn_ref[0].astype(BF16))
    for j in blocks:
        acc = acc + wj[:, j:j + 1] * accs[j]
    o_bd = acc / l_tot
    o = None
    for h in range(MOBA_HEADS):
        part = o_bd[h * r:(h + 1) * r, :] * head_masks[h]
        o = part if o is None else o + part
    o_ref[0] = o.astype(o_ref.dtype)

    @pl.when(i == n_seq - 1)
    def _():
        for d in range(1, PAGE_SLOTS):
            for cp in page_copies(n_seq - 1, lax.rem(i + d, PAGE_SLOTS)):
                cp.wait()


def _moba_sample(q, k_new, v_new, cache_kt, cache_vt, page_table, base):
    b, r, w = q.shape
    page = cache_kt.shape[2]
    n_pages = page_table.shape[1]
    assert page == PAGE_SIZE and r == S_ROWS and cache_kt.shape[1] == w and b >= PAGE_SLOTS
    tile = pl.BlockSpec((1, r, w), lambda i, pt: (i, 0, 0))
    hbm = pl.BlockSpec(memory_space=pl.ANY)
    grid_spec = pltpu.PrefetchScalarGridSpec(
        num_scalar_prefetch=1,
        grid=(b,),
        in_specs=[tile, tile, tile, hbm, hbm],
        out_specs=tile,
        scratch_shapes=[pltpu.VMEM((PAGE_SLOTS, n_pages, w, page), F32),
                        pltpu.VMEM((PAGE_SLOTS, n_pages, w, page), F32),
                        pltpu.SemaphoreType.DMA((2, PAGE_SLOTS))],
    )
    body = functools.partial(_moba_sample_body, n_pages=n_pages, base=base, n_seq=b)
    return pl.pallas_call(
        body,
        name="moba_sample",
        grid_spec=grid_spec,
        out_shape=jax.ShapeDtypeStruct((b, r, w), BF16),
        compiler_params=_params("arbitrary"),
    )(page_table, q, k_new, v_new, cache_kt, cache_vt)


def _trunk(x, weights, norm_final, *, sample=None):
    b, l, d = x.shape
    depth = len(weights)
    x = x.reshape(b * l, d)
    gr = GROUP_ROWS
    states = []
    for layer, wts in enumerate(weights):
        x = _ffn(x, wts["n_ffn1"], wts["ffn1_g"], wts["ffn1_u"], wts["ffn1_d"])
        if sample is None:
            qkv, z, sc, ba, mk, mq, mv = _inproj(x, wts["n_mix"], wts["w_rows"], wts["w_qv_t"],
                                                 seq_len=l)
            grp = lambda a: a.reshape(b, l, a.shape[-1])
            o_a, o_b, s_new, gated = _mixer(
                grp(qkv), grp(z), grp(ba), grp(sc), wts["gdn_conv_w"], wts["gparams"],
                wts["gdn_norm_w"], wts["sc_conv_w"], c=GDN_CHUNK, ng=4 if b % 4 == 0 else 1)
            o_c = _moba_prompt(mq, mk, mv)
        else:
            qkv, z, sc, ba, mk, mq, mv = _inproj(x, wts["n_mix"], wts["w_rows"], wts["w_qv"])
            ngroups = b * l // gr
            grp = lambda a: a.reshape(ngroups, gr, a.shape[-1])
            state = (grp(sample["gdn_conv"][layer]), grp(sample["sconv"][layer]),
                     sample["gdn"], layer)
            o_a, o_b, s_new, gated = _mixer(
                grp(qkv), grp(z), grp(ba), grp(sc), wts["gdn_conv_w"], wts["gparams"],
                wts["gdn_norm_w"], wts["sc_conv_w"], c=l, ng=2 if ngroups % 2 == 0 else 1,
                state=state, row_lo=S_LO, row_hi=S_HI)
            seq = lambda a: a.reshape(b, l, a.shape[-1])
            o_c = _moba_sample(seq(mq), seq(mk), seq(mv), sample["cache_kt"], sample["cache_vt"],
                               sample["page_table"], layer * sample["n_pool"])
            o_c = o_c.reshape(b * l, -1)
        mix = (o_a.reshape(b * l, -1), o_b.reshape(b * l, -1), o_c, wts["w_out"])
        x = _ffn(x, wts["n_ffn2"], wts["ffn2_g"], wts["ffn2_u"], wts["ffn2_d"], mix=mix,
                 final_w=norm_final if layer == depth - 1 else None)
        states.append(dict(gdn=s_new, qkv=qkv.reshape(b, l, -1), gated=gated.reshape(b, l, -1),
                           k=mk.reshape(b, l, -1), v=mv))
    return x.reshape(b, l, d), states


def kernel(x_prompt, x_sample, state_gdn, state_gdn_conv, state_sconv, cache_k, cache_v,
           page_table, norm_ffn1, ffn1_w_gate, ffn1_w_up, ffn1_w_down, norm_mix, w_in,
           gdn_conv_w, gdn_a_log, gdn_dt_bias, gdn_norm_w, sc_conv_w, w_out,
           norm_ffn2, ffn2_w_gate, ffn2_w_up, ffn2_w_down, norm_final):
    depth = w_in.shape[0]
    gw, sw, mw = GDN_WIDTH, SC_WIDTH, MOBA_WIDTH
    nh = GDN_HEADS
    mh, md = MOBA_HEADS, MOBA_HEAD_DIM

    weights = []
    for la in range(depth):
        w = w_in[la]
        ba_off = 4 * gw
        sc_off = ba_off + 2 * nh
        mo_off = sc_off + 3 * sw
        w_rows = jnp.concatenate(
            [w[:, :ba_off], w[:, sc_off:mo_off],
             jnp.pad(w[:, ba_off:sc_off], ((0, 0), (0, LANES - 2 * nh))),
             w[:, mo_off + mw:mo_off + 2 * mw]], axis=1).astype(BF16)
        w_qv = jnp.concatenate([w[:, mo_off:mo_off + mw], w[:, mo_off + 2 * mw:mo_off + 3 * mw]],
                               axis=1).astype(BF16)
        gparams = jnp.zeros((SUBLANES, LANES), F32)
        gparams = gparams.at[0, nh:2 * nh].set(gdn_a_log[la].astype(F32))
        gparams = gparams.at[1, nh:2 * nh].set(gdn_dt_bias[la].astype(F32))
        weights.append(dict(
            n_ffn1=norm_ffn1[la], ffn1_g=_to_bf16(ffn1_w_gate, la),
            ffn1_u=_to_bf16(ffn1_w_up, la), ffn1_d=_to_bf16(ffn1_w_down, la),
            n_mix=norm_mix[la], w_rows=w_rows, w_qv=w_qv, w_qv_t=w_qv.T,
            gdn_conv_w=gdn_conv_w[la], gparams=gparams, gdn_norm_w=gdn_norm_w[la],
            sc_conv_w=sc_conv_w[la], w_out=_to_bf16(w_out, la),
            n_ffn2=norm_ffn2[la], ffn2_g=_to_bf16(ffn2_w_gate, la),
            ffn2_u=_to_bf16(ffn2_w_up, la), ffn2_d=_to_bf16(ffn2_w_down, la)))

    pb, pl_, _ = x_prompt.shape
    y_p, st_p = _trunk(x_prompt, weights, norm_final)
    hist = GDN_CONV - 1
    p_gdn = jnp.stack([s["gdn"] for s in st_p])
    p_conv = jnp.stack([s["qkv"][:, -hist:] for s in st_p])
    p_sconv = jnp.stack([s["gated"][:, -(SC_CONV - 1):] for s in st_p])
    p_k = jnp.stack([s["k"] for s in st_p]).reshape(depth, pb, pl_, mh, md)
    p_v = jnp.stack([s["v"] for s in st_p]).reshape(depth, pb, mh, md, pl_)
    p_v = p_v.transpose(0, 1, 4, 2, 3)

    db, dl, d = x_sample.shape
    assert dl == S_HI - S_LO and (db * S_ROWS) % GROUP_ROWS == 0
    n_pool = cache_k.shape[1]
    tile_pad = lambda a, lo: jnp.pad(a, ((0, 0),) * (a.ndim - 2)
                                     + ((lo, S_ROWS - lo - a.shape[-2]), (0, 0)))
    seqs_per_group = GROUP_ROWS // S_ROWS
    paged_t = lambda c: c.transpose(0, 1, 3, 4, 2).reshape(depth * n_pool, mw, PAGE_SIZE)
    sample = dict(
        gdn=state_gdn.reshape(depth, db // seqs_per_group, seqs_per_group * nh,
                              GDN_HEAD_DIM, GDN_HEAD_DIM),
        gdn_conv=tile_pad(state_gdn_conv, 0),
        sconv=tile_pad(state_sconv, S_LO - (SC_CONV - 1)),
        cache_kt=paged_t(cache_k), cache_vt=paged_t(cache_v), n_pool=n_pool,
        page_table=page_table)
    y_s, st_s = _trunk(tile_pad(x_sample, S_LO), weights, norm_final, sample=sample)
    y_s = y_s[:, S_LO:S_HI]
    s_gdn = jnp.stack([s["gdn"] for s in st_s]).reshape(state_gdn.shape)
    s_conv = jnp.stack([s["qkv"][:, S_HI - hist:S_HI] for s in st_s])
    s_sconv = jnp.stack([s["gated"][:, S_HI - (SC_CONV - 1):S_HI] for s in st_s])
    s_k = jnp.stack([s["k"][:, S_LO:S_HI] for s in st_s]).reshape(depth, db, dl, mh, md)
    s_v = jnp.stack([s["v"].reshape(db, S_ROWS, mw)[:, S_LO:S_HI] for s in st_s]).reshape(
        depth, db, dl, mh, md)
    return (y_p, y_s, p_gdn, p_conv, p_sconv, p_k, p_v, s_gdn, s_conv, s_sconv, s_k, s_v)
```

```python
import functools

import jax
import jax.numpy as jnp
from jax import lax
from jax.experimental import pallas as pl
from jax.experimental.pallas import tpu as pltpu

NORM_EPS = 1e-6
NEG_INF = -1e30

GDN_HEADS = 4
GDN_HEAD_DIM = 128
GDN_WIDTH = GDN_HEADS * GDN_HEAD_DIM
GDN_CONV = 4
GDN_CHUNK = 64
SC_WIDTH = 256
SC_CONV = 3
MOBA_HEADS = 4
MOBA_HEAD_DIM = 64
MOBA_WIDTH = MOBA_HEADS * MOBA_HEAD_DIM
MOBA_BLOCK = 256
MOBA_TOPK = 3
PAGE_SIZE = 128

LANES = 128
SUBLANES = 8
VMEM_LIMIT_BYTES = 56 * 1024 * 1024

S_ROWS = SUBLANES
S_LO = GDN_CONV - 1
S_HI = S_LO + 4

GROUP_ROWS = GDN_CHUNK
STACK_ROWS = GDN_HEADS * GROUP_ROWS

F32 = jnp.float32
BF16 = jnp.bfloat16
HI = lax.Precision.HIGHEST


def _dot(a, b, precision=None):
    return jnp.dot(a, b, preferred_element_type=F32, precision=precision)


def _dot_nt(a, b, precision=None):
    return lax.dot_general(a, b, (((1,), (1,)), ((), ())),
                           preferred_element_type=F32, precision=precision)


def _dot_tn(a, b, precision=None):
    return lax.dot_general(a, b, (((0,), (0,)), ((), ())),
                           preferred_element_type=F32, precision=precision)


def _rms(x, w):
    return x * lax.rsqrt(jnp.mean(x * x, axis=-1, keepdims=True) + NORM_EPS) * w


def _silu(x):
    return x * jax.nn.sigmoid(x)


def _softplus(x):
    return jnp.maximum(x, 0.0) + jnp.log1p(jnp.exp(-jnp.abs(x)))


def _resident(shape):
    nd = len(shape)
    return pl.BlockSpec(shape, lambda *_: (0,) * nd, pipeline_mode=pl.Buffered(1))


def _params(*semantics):
    return pltpu.CompilerParams(dimension_semantics=semantics,
                                vmem_limit_bytes=VMEM_LIMIT_BYTES)


def _cast_body(w_ref, o_ref):
    o_ref[...] = w_ref[0].astype(o_ref.dtype)


def _to_bf16(w, layer, rows=256):
    _, r, c = w.shape
    rows = min(rows, r)
    assert r % rows == 0
    return pl.pallas_call(
        _cast_body,
        name="to_bf16",
        grid=(r // rows,),
        in_specs=[pl.BlockSpec((1, rows, c), lambda i: (layer, i, 0))],
        out_specs=pl.BlockSpec((rows, c), lambda i: (i, 0)),
        out_shape=jax.ShapeDtypeStruct((r, c), BF16),
        compiler_params=_params("arbitrary"),
    )(w)


def _ffn_body(*refs, has_mix, oc_transposed, has_final, f_chunk):
    refs = list(refs)
    x_ref = refs.pop(0)
    if has_mix:
        oa_ref, ob_ref, oc_ref, wo_ref = refs[:4]
        refs = refs[4:]
    nw_ref, wg_ref, wu_ref, wd_ref = refs[:4]
    refs = refs[4:]
    if has_final:
        nf_ref = refs.pop(0)
    out_ref, xn_ref, acc_ref = refs

    x = x_ref[...]
    if has_mix:
        ga = oa_ref.shape[1]
        gb = ga + ob_ref.shape[1]
        x = x + _dot(oa_ref[...], wo_ref[0:ga, :]) + _dot(ob_ref[...], wo_ref[ga:gb, :])
        if oc_transposed:
            x = x + _dot_tn(oc_ref[0], wo_ref[gb:, :])
        else:
            x = x + _dot(oc_ref[...], wo_ref[gb:, :])
    xn_ref[...] = _rms(x, nw_ref[...]).astype(BF16)
    d_ff = wg_ref.shape[1]
    for c in range(d_ff // f_chunk):
        lo = c * f_chunk
        xn = xn_ref[...]
        g = _dot(xn, wg_ref[:, lo:lo + f_chunk])
        u = _dot(xn, wu_ref[:, lo:lo + f_chunk])
        h = (_silu(g) * u).astype(BF16)
        part = _dot(h, wd_ref[lo:lo + f_chunk, :])
        if c == 0:
            acc_ref[...] = part
        else:
            acc_ref[...] += part
    y = x + 0.5 * acc_ref[...]
    if has_final:
        y = _rms(y, nf_ref[...])
    out_ref[...] = y


def _ffn(x, nw, wg, wu, wd, mix=None, final_w=None, tm=512, f_chunk=256):
    m, d = x.shape
    d_ff = wg.shape[1]
    tm = min(tm, m)
    assert m % tm == 0 and d_ff % f_chunk == 0
    row = lambda i: (i, 0)
    in_specs = [pl.BlockSpec((tm, d), row)]
    args = [x]
    oc_transposed = False
    if mix is not None:
        oa, ob, oc, wo = mix
        oc_transposed = oc.ndim == 3
        if oc_transposed:
            tiles = oc.shape[2] // tm
            assert oc.shape[2] % tm == 0
            oc_spec = pl.BlockSpec((1, oc.shape[1], tm), lambda i: (i // tiles, 0, i % tiles))
        else:
            oc_spec = pl.BlockSpec((tm, oc.shape[1]), row)
        in_specs += [pl.BlockSpec((tm, oa.shape[1]), row), pl.BlockSpec((tm, ob.shape[1]), row),
                     oc_spec, _resident(wo.shape)]
        args += [oa, ob, oc, wo]
    in_specs += [_resident((1, d)), _resident(wg.shape), _resident(wu.shape), _resident(wd.shape)]
    args += [nw.reshape(1, d), wg, wu, wd]
    if final_w is not None:
        in_specs.append(_resident((1, d)))
        args.append(final_w.reshape(1, d))
    body = functools.partial(_ffn_body, has_mix=mix is not None, oc_transposed=oc_transposed,
                             has_final=final_w is not None, f_chunk=f_chunk)
    return pl.pallas_call(
        body,
        name="ffn_mix" if mix is not None else "ffn",
        grid=(m // tm,),
        in_specs=in_specs,
        out_specs=pl.BlockSpec((tm, d), row),
        out_shape=jax.ShapeDtypeStruct((m, d), F32),
        scratch_shapes=[pltpu.VMEM((tm, d), BF16), pltpu.VMEM((tm, d), F32)],
        compiler_params=_params("arbitrary"),
    )(*args)


_ROW_WIDTHS = (3 * GDN_WIDTH, GDN_WIDTH, 3 * SC_WIDTH, LANES, MOBA_WIDTH)


def _inproj_body(x_ref, nw_ref, w_ref, wqv_ref, *out_refs, transposed_qv):
    xn = _rms(x_ref[...], nw_ref[...]).astype(BF16)
    lo = 0
    for ref in out_refs[:len(_ROW_WIDTHS)]:
        width = ref.shape[1]
        ref[...] = _dot(xn, w_ref[:, lo:lo + width])
        lo += width
    q_ref, v_ref = out_refs[len(_ROW_WIDTHS):]
    mw = MOBA_WIDTH
    if transposed_qv:
        q_ref[0] = _dot_nt(wqv_ref[0:mw, :], xn)
        v_ref[0] = _dot_nt(wqv_ref[mw:2 * mw, :], xn)
    else:
        q_ref[...] = _dot(xn, wqv_ref[:, 0:mw])
        v_ref[...] = _dot(xn, wqv_ref[:, mw:2 * mw])


def _inproj(x, nw, w_rows, w_qv, *, seq_len=None, tm=512):
    m, d = x.shape
    tm = min(tm, m)
    assert m % tm == 0 and w_rows.shape[1] == sum(_ROW_WIDTHS)
    row = lambda i: (i, 0)
    out_specs = [pl.BlockSpec((tm, wd), row) for wd in _ROW_WIDTHS]
    out_shape = [jax.ShapeDtypeStruct((m, wd), F32) for wd in _ROW_WIDTHS]
    transposed_qv = seq_len is not None
    if transposed_qv:
        assert seq_len % tm == 0
        tiles = seq_len // tm
        spec = pl.BlockSpec((1, MOBA_WIDTH, tm), lambda i: (i // tiles, 0, i % tiles))
        shape = jax.ShapeDtypeStruct((m // seq_len, MOBA_WIDTH, seq_len), F32)
    else:
        spec = pl.BlockSpec((tm, MOBA_WIDTH), row)
        shape = jax.ShapeDtypeStruct((m, MOBA_WIDTH), F32)
    out_specs += [spec, spec]
    out_shape += [shape, shape]
    return pl.pallas_call(
        functools.partial(_inproj_body, transposed_qv=transposed_qv),
        name="inproj",
        grid=(m // tm,),
        in_specs=[pl.BlockSpec((tm, d), row), _resident((1, d)), _resident(w_rows.shape),
                  _resident(w_qv.shape)],
        out_specs=out_specs,
        out_shape=out_shape,
        compiler_params=_params("arbitrary"),
    )(x, nw.reshape(1, d), w_rows, w_qv)


def _mixer_body(*refs, c, ng, has_state, row_lo, row_hi):
    refs = list(refs)
    qkv_ref, z_ref, ba_ref, sc_ref = refs[:4]
    refs = refs[4:]
    if has_state:
        qh_ref, sh_ref, s0_ref = refs[:3]
        refs = refs[3:]
    cw_ref, gp_ref, nw_ref, scw_ref = refs[:4]
    refs = refs[5:]
    oa_ref, ob_ref, st_ref, gt_ref = refs[:4]
    xbuf, gbuf, s_ref = refs[4:]
    n = pl.program_id(1)
    hd = GDN_HEAD_DIM
    gr = GROUP_ROWS
    sr = STACK_ROWS
    pad = SUBLANES
    seqs = gr // c
    shift = c.bit_length() - 1
    assert 1 << shift == c

    @pl.when(n == 0)
    def _():
        xbuf[:, 0:pad, :] = jnp.zeros((ng, pad, xbuf.shape[2]), F32)
        gbuf[:, 0:pad, :] = jnp.zeros((ng, pad, gbuf.shape[2]), F32)
        if has_state:
            s_ref[...] = s0_ref[0]
        else:
            s_ref[...] = jnp.zeros(s_ref.shape, F32)

    rows = lax.broadcasted_iota(jnp.int32, (gr, 1), 0) & (c - 1)
    is_hist = rows < row_lo
    live = jnp.where((rows >= row_lo) & (rows < row_hi), 1.0, 0.0)
    ri = lax.broadcasted_iota(jnp.int32, (gr, gr), 0)
    ci = lax.broadcasted_iota(jnp.int32, (gr, gr), 1)
    same = (ri >> shift) == (ci >> shift)
    cum_op = jnp.concatenate([jnp.where(same & (ri >= ci), 1.0, 0.0),
                              jnp.where(same, 1.0, 0.0)], axis=0)
    rs = lax.broadcasted_iota(jnp.int32, (sr, sr), 0)
    cs = lax.broadcasted_iota(jnp.int32, (sr, sr), 1)
    same_s = (rs >> shift) == (cs >> shift)
    incl = same_s & (rs >= cs)
    strict = same_s & (rs > cs)
    eye = jnp.where(rs == cs, 1.0, 0.0)

    def stack_heads(a, lo):
        return jnp.concatenate([a[:, lo + h * hd:lo + (h + 1) * hd] for h in range(GDN_HEADS)],
                               axis=0)

    def stack_cols(a, lo):
        return jnp.concatenate([a[:, lo + h:lo + h + 1] for h in range(GDN_HEADS)], axis=0)

    nh = GDN_HEADS
    groups = range(ng)

    def front(g):
        x = qkv_ref[g]
        if has_state:
            x = jnp.where(is_hist, qh_ref[g], x)
        xbuf[g, pad:pad + gr, :] = x
        y = xbuf[g, pad - 3:pad - 3 + gr, :] * cw_ref[0:1, :]
        y = y + xbuf[g, pad - 2:pad - 2 + gr, :] * cw_ref[1:2, :]
        y = y + xbuf[g, pad - 1:pad - 1 + gr, :] * cw_ref[2:3, :]
        y = y + x * cw_ref[3:4, :]
        tail = xbuf[g, gr:gr + pad, :]
        xbuf[g, 0:pad, :] = tail
        act = _silu(y)

        sc = sc_ref[g]
        scw = SC_WIDTH
        gated = sc[:, scw:2 * scw] * sc[:, 0:scw]
        if has_state:
            gated = jnp.where(is_hist, sh_ref[g], gated)
        gbuf[g, pad:pad + gr, :] = gated
        yb = gbuf[g, pad - 2:pad - 2 + gr, :] * scw_ref[0:1, :]
        yb = yb + gbuf[g, pad - 1:pad - 1 + gr, :] * scw_ref[1:2, :]
        yb = yb + gated * scw_ref[2:3, :]
        ob_ref[g] = (sc[:, 2 * scw:3 * scw] * yb).astype(ob_ref.dtype)
        gtail = gbuf[g, gr:gr + pad, :]
        gbuf[g, 0:pad, :] = gtail
        gt_ref[g] = gated

        ba = ba_ref[g]
        beta_all = jax.nn.sigmoid(ba) * live
        g_all = -jnp.exp(gp_ref[0:1, :]) * _softplus(ba + gp_ref[1:2, :]) * live
        cum = _dot(cum_op, g_all, HI)
        gc_all, glast_all = cum[:gr], cum[gr:]
        beta = stack_cols(beta_all, 0)
        gc = stack_cols(gc_all, nh)
        eg = jnp.exp(gc)
        etail = jnp.exp(stack_cols(glast_all, nh) - gc)
        gc_row = jnp.broadcast_to(gc, (sr, LANES)).T[0:1, :]
        decay = jnp.where(incl, jnp.exp(jnp.where(incl, gc - gc_row, 0.0)), 0.0)

        xq = stack_heads(act, 0)
        xk = stack_heads(act, GDN_WIDTH)
        v = stack_heads(act, 2 * GDN_WIDTH)
        q = xq * lax.rsqrt(jnp.sum(xq * xq, axis=-1, keepdims=True) + NORM_EPS) * (hd ** -0.5)
        k = xk * lax.rsqrt(jnp.sum(xk * xk, axis=-1, keepdims=True) + NORM_EPS)
        kb = k.astype(BF16)
        m_strict = jnp.where(strict, beta * _dot_nt(kb, kb) * decay, 0.0)
        return dict(q=q, k=k, v=v, kb=kb, beta=beta, eg=eg, etail=etail, decay=decay,
                    cd_all=jnp.exp(glast_all), m_strict=m_strict)

    fr = [front(g) for g in groups]

    invs = [eye - f["m_strict"] for f in fr]
    ps = [f["m_strict"] for f in fr]
    for _ in range(shift - 1):
        pbs = [p.astype(BF16) for p in ps]
        ps = [_dot(pb, pb) for pb in pbs]
        invs = [inv + _dot(inv.astype(BF16), p.astype(BF16)) for inv, p in zip(invs, ps)]

    def solve(f, inv):
        rhs = jnp.concatenate([f["k"] * (f["beta"] * f["eg"]), f["v"] * f["beta"]],
                              axis=-1).astype(BF16)
        sol = _dot(inv.astype(BF16), rhs)
        attn = (_dot_nt(f["q"].astype(BF16), f["kb"]) * f["decay"]).astype(BF16)
        return dict(w_k=sol[:, :hd], u_v=sol[:, hd:], attn=attn, qd=f["q"] * f["eg"],
                    kt=(f["k"] * f["etail"]).astype(BF16))

    sv = [solve(f, inv) for f, inv in zip(fr, invs)]

    pair_ids = [(h, s) for h in range(GDN_HEADS) for s in range(seqs)]

    def read_state(g, t):
        parts = []
        for h, s in pair_ids:
            r0 = h * gr + s * c
            s_old = s_ref[g, s * nh + h]
            lhs = jnp.concatenate([t["w_k"][r0:r0 + c], t["qd"][r0:r0 + c]], axis=0)
            res = _dot(lhs.astype(BF16), s_old.astype(BF16))
            parts.append((s_old, t["u_v"][r0:r0 + c] - res[:c], res[c:]))
        return parts

    rd = [read_state(g, t) for g, t in zip(groups, sv)]
    us = [jnp.concatenate([p[1] for p in parts], axis=0) for parts in rd]
    outs = [jnp.concatenate([p[2] for p in parts], axis=0) + _dot(t["attn"], u.astype(BF16))
            for parts, t, u in zip(rd, sv, us)]
    for g in groups:
        for (h, s), (s_old, u_p, _) in zip(pair_ids, rd[g]):
            r0 = h * gr + s * c
            cd = fr[g]["cd_all"][s * c:s * c + 1, nh + h:nh + h + 1]
            s_ref[g, s * nh + h] = s_old * cd + _dot_tn(sv[g]["kt"][r0:r0 + c],
                                                        u_p.astype(BF16))
    for g in groups:
        zs = stack_heads(z_ref[g], 0)
        o_n = _rms(outs[g], nw_ref[...]) * _silu(zs)
        oa_ref[g] = jnp.concatenate([o_n[h * gr:(h + 1) * gr] for h in range(GDN_HEADS)],
                                    axis=1).astype(oa_ref.dtype)

    @pl.when(n == pl.num_programs(1) - 1)
    def _():
        st_ref[0] = s_ref[...]


def _mixer(qkv, z, ba, sc, conv_w, gparams, norm_w, sc_conv_w, *, c, ng, layer, depth,
           prev_states=None, state=None, row_lo=0, row_hi=None):
    g_total, l, _ = qkv.shape
    gr = GROUP_ROWS
    assert l % gr == 0 and gr % c == 0 and g_total % ng == 0
    nc = l // gr
    pairs = GDN_HEADS * (gr // c)
    row_hi = c if row_hi is None else row_hi
    blk = lambda w: pl.BlockSpec((ng, gr, w), lambda i, n: (i, n, 0))
    in_specs = [blk(qkv.shape[2]), blk(z.shape[2]), blk(ba.shape[2]), blk(sc.shape[2])]
    args = [qkv, z, ba, sc]
    st_shape = (g_total, pairs, GDN_HEAD_DIM, GDN_HEAD_DIM)
    if state is not None:
        assert nc == 1
        qh, sh, s0 = state
        assert s0.shape[1:] == st_shape
        in_specs += [pl.BlockSpec((ng, gr, qh.shape[2]), lambda i, n: (i, 0, 0)),
                     pl.BlockSpec((ng, gr, sh.shape[2]), lambda i, n: (i, 0, 0)),
                     pl.BlockSpec((1, ng) + st_shape[1:], lambda i, n: (layer, i, 0, 0, 0))]
        args += [qh, sh, s0]
    const = lambda a: pl.BlockSpec(a.shape, lambda i, n: (0,) * a.ndim)
    norm_w = norm_w.reshape(1, -1)
    in_specs += [const(conv_w), const(gparams), const(norm_w), const(sc_conv_w)]
    args += [conv_w, gparams, norm_w, sc_conv_w]
    if prev_states is None:
        prev_states = jnp.zeros((depth,) + st_shape, F32)
    assert prev_states.shape == (depth,) + st_shape
    aliases = {len(args): 2}
    in_specs.append(pl.BlockSpec(memory_space=pl.ANY))
    args.append(prev_states)
    out_shape = [jax.ShapeDtypeStruct((g_total, l, GDN_WIDTH), BF16),
                 jax.ShapeDtypeStruct((g_total, l, SC_WIDTH), BF16),
                 jax.ShapeDtypeStruct((depth,) + st_shape, F32),
                 jax.ShapeDtypeStruct((g_total, l, SC_WIDTH), F32)]
    out_specs = [pl.BlockSpec((ng, gr, GDN_WIDTH), lambda i, n: (i, n, 0)),
                 pl.BlockSpec((ng, gr, SC_WIDTH), lambda i, n: (i, n, 0)),
                 pl.BlockSpec((1, ng) + st_shape[1:], lambda i, n: (layer, i, 0, 0, 0)),
                 pl.BlockSpec((ng, gr, SC_WIDTH), lambda i, n: (i, n, 0))]
    body = functools.partial(_mixer_body, c=c, ng=ng, has_state=state is not None,
                             row_lo=row_lo, row_hi=row_hi)
    return pl.pallas_call(
        body,
        name="seq_mixer",
        grid=(g_total // ng, nc),
        in_specs=in_specs,
        out_specs=out_specs,
        out_shape=out_shape,
        input_output_aliases=aliases,
        scratch_shapes=[pltpu.VMEM((ng, gr + SUBLANES, qkv.shape[2]), F32),
                        pltpu.VMEM((ng, gr + SUBLANES, SC_WIDTH), F32),
                        pltpu.VMEM((ng,) + st_shape[1:], F32)],
        compiler_params=_params("arbitrary", "arbitrary"),
    )(*args)


def _select_blocks(gate, n_valid, axis=0):
    nb = gate.shape[axis]
    sub = lax.broadcasted_iota(jnp.int32, gate.shape, axis)
    gate = jnp.where(sub < n_valid, gate, NEG_INF)
    sel = jnp.zeros(gate.shape, F32)
    for j in range(nb):
        gj = gate[j:j + 1, :] if axis == 0 else gate[:, j:j + 1]
        beats = (gate > gj) | ((gate == gj) & (sub < j))
        cnt = jnp.sum(jnp.where(beats, 1.0, 0.0), axis=axis, keepdims=True)
        hit = (cnt < MOBA_TOPK) & (j < n_valid)
        sel = jnp.where((sub == j) & hit, 1.0, sel)
    return sel


_PV_ROWS = MOBA_HEAD_DIM + 16


def _moba_prompt_body(q_ref, k_ref, v_ref, o_ref, kmean_ref, kb_ref, vb_ref, sel_ref, *, nbat):
    i = pl.program_id(1)
    blk = MOBA_BLOCK
    hd = MOBA_HEAD_DIM
    w = MOBA_WIDTH
    l_seq = v_ref.shape[2]
    nb = l_seq // blk
    heads = range(MOBA_HEADS)
    head_rows = [slice(h * hd, (h + 1) * hd) for h in heads]
    chains = [(b, h) for b in range(nbat) for h in heads]

    @pl.when(i == 0)
    def _():
        ones = jnp.ones((_PV_ROWS - hd, blk), BF16)
        for b in range(nbat):
            for j in range(nb):
                kj = k_ref[b * l_seq + j * blk:b * l_seq + (j + 1) * blk, :]
                kmean_ref[b, j:j + 1, :] = jnp.mean(kj, axis=0, keepdims=True)
                kb_ref[b, j] = kj.astype(BF16)
                for h in heads:
                    vb_ref[b, j, h, 0:hd, :] = v_ref[b, head_rows[h],
                                                     j * blk:(j + 1) * blk].astype(BF16)
                    vb_ref[b, j, h, hd:_PV_ROWS, :] = ones

    feat = lax.broadcasted_iota(jnp.int32, (w, blk), 0)
    key_i = lax.broadcasted_iota(jnp.int32, (blk, blk), 0)
    qry_i = lax.broadcasted_iota(jnp.int32, (blk, blk), 1)
    causal = key_i <= qry_i
    qmb = []
    for b in range(nbat):
        qt = q_ref[b] * (hd ** -0.5)
        qms = [jnp.where((feat >= h * hd) & (feat < (h + 1) * hd), qt, 0.0) for h in heads]
        qmb.append(jnp.concatenate(qms, axis=1).astype(BF16))
        for h in heads:
            gate = _dot(kmean_ref[b], qms[h], HI)
            sel_ref[b, h] = _select_blocks(gate, i)

    def scores(j):
        s_all = [_dot(kb_ref[b, j], qmb[b]) for b in range(nbat)]
        return [s_all[b][:, h * blk:(h + 1) * blk] for b, h in chains]

    def attend(j, s_list, m_list):
        ps = [jnp.exp(s - m).astype(BF16) for s, m in zip(s_list, m_list)]
        return [_dot(vb_ref[b, j, h], p) for (b, h), p in zip(chains, ps)]

    s0 = [jnp.where(causal, s, NEG_INF) for s in scores(i)]
    m0 = [jnp.max(s, axis=0, keepdims=True) for s in s0]
    pv0 = attend(i, s0, m0)
    init = []
    for m, pv in zip(m0, pv0):
        init += [m, pv[hd:hd + 1], pv[:hd]]

    def past_block(j, carry):
        s_list = [jnp.where(sel_ref[b, h, pl.ds(j, 1), :] > 0.5, s, NEG_INF)
                  for (b, h), s in zip(chains, scores(j))]
        m_old = carry[0::3]
        m_new = [jnp.maximum(m, jnp.max(s, axis=0, keepdims=True))
                 for m, s in zip(m_old, s_list)]
        pv = attend(j, s_list, m_new)
        new = []
        for c in range(len(chains)):
            alpha = jnp.exp(m_old[c] - m_new[c])
            new += [m_new[c], alpha * carry[3 * c + 1] + pv[c][hd:hd + 1],
                    alpha * carry[3 * c + 2] + pv[c][:hd]]
        return tuple(new)

    final = lax.fori_loop(0, i, past_block, tuple(init))
    for c, (b, h) in enumerate(chains):
        o_ref[b, head_rows[h], :] = (final[3 * c + 2] / final[3 * c + 1]).astype(o_ref.dtype)


def _moba_prompt(qt, k, vt, nbat=2):
    b, w, l = qt.shape
    blk = MOBA_BLOCK
    nbat = nbat if b % nbat == 0 else 1
    assert l % blk == 0
    nb = l // blk
    return pl.pallas_call(
        functools.partial(_moba_prompt_body, nbat=nbat),
        name="moba_prompt",
        grid=(b // nbat, nb),
        in_specs=[pl.BlockSpec((nbat, w, blk), lambda i, t: (i, 0, t)),
                  pl.BlockSpec((nbat * l, w), lambda i, t: (i, 0)),
                  pl.BlockSpec((nbat, w, l), lambda i, t: (i, 0, 0))],
        out_specs=pl.BlockSpec((nbat, w, blk), lambda i, t: (i, 0, t)),
        out_shape=jax.ShapeDtypeStruct((b, w, l), BF16),
        scratch_shapes=[pltpu.VMEM((nbat, nb, w), F32), pltpu.VMEM((nbat, nb, blk, w), BF16),
                        pltpu.VMEM((nbat, nb, MOBA_HEADS, _PV_ROWS, blk), BF16),
                        pltpu.VMEM((nbat, MOBA_HEADS, nb, blk), F32)],
        compiler_params=_params("arbitrary", "arbitrary"),
    )(qt, k, vt)


PAGE_SLOTS = 3


def _moba_sample_body(pt_ref, q_ref, kn_ref, vn_ref, ck_ref, cv_ref, o_ref, kbuf, vbuf, sem,
                      *, n_pages, base, n_seq):
    i = pl.program_id(0)
    hd = MOBA_HEAD_DIM
    r = S_ROWS
    w = MOBA_WIDTH
    ppb = MOBA_BLOCK // PAGE_SIZE
    nb = n_pages // ppb
    ahead = PAGE_SLOTS - 1

    def page_copies(seq, slot):
        cps = []
        for p in range(n_pages):
            row = base + pt_ref[seq, p]
            cps.append(pltpu.make_async_copy(ck_ref.at[row], kbuf.at[slot, p], sem.at[0, slot]))
            cps.append(pltpu.make_async_copy(cv_ref.at[row], vbuf.at[slot, p], sem.at[1, slot]))
        return cps

    @pl.when(i == 0)
    def _():
        for d in range(ahead):
            for cp in page_copies(d, d):
                cp.start()

    slot = lax.rem(i, PAGE_SLOTS)
    for cp in page_copies(i, slot):
        cp.wait()

    lane = lax.broadcasted_iota(jnp.int32, (r, w), 1)
    head_masks = [jnp.where((lane >= h * hd) & (lane < (h + 1) * hd), 1.0, 0.0)
                  for h in range(MOBA_HEADS)]
    q = q_ref[0] * (hd ** -0.5)
    q_bd = jnp.concatenate([q * hm for hm in head_masks], axis=0)
    q_bb = q_bd.astype(BF16)
    col = lax.broadcasted_iota(jnp.int32, (w, LANES), 1)
    blocks = range(nb)

    kp = [kbuf[slot, p] for p in range(n_pages)]
    s_pages = [_dot(q_bb, x.astype(BF16)) for x in kp]
    kmean_t = jnp.zeros((w, LANES), F32)
    for j in blocks:
        ksum = kp[j * ppb]
        for pp in range(1, ppb):
            ksum = ksum + kp[j * ppb + pp]
        kmean_t = jnp.where(col == j, jnp.sum(ksum, axis=1, keepdims=True) * (1.0 / MOBA_BLOCK),
                            kmean_t)
    s_blk = [jnp.concatenate(s_pages[j * ppb:(j + 1) * ppb], axis=-1) for j in blocks]
    m_cols = [jnp.max(s, axis=1, keepdims=True) for s in s_blk]
    e_blk = [jnp.exp(s - m) for s, m in zip(s_blk, m_cols)]
    l_cols = [jnp.sum(e, axis=1, keepdims=True) for e in e_blk]
    accs = []
    for j in blocks:
        eb = e_blk[j].astype(BF16)
        acc = None
        for pp in range(ppb):
            vp = vbuf[slot, j * ppb + pp].astype(BF16)
            part = _dot_nt(eb[:, pp * PAGE_SIZE:(pp + 1) * PAGE_SIZE], vp)
            acc = part if acc is None else acc + part
        accs.append(acc)

    nxt = jnp.minimum(i + ahead, n_seq - 1)
    for cp in page_copies(nxt, lax.rem(i + ahead, PAGE_SLOTS)):
        cp.start()

    gate = _dot(q_bd, kmean_t, HI)[:, :nb]
    sel = _select_blocks(gate, nb, axis=1) > 0.5
    m_all = jnp.concatenate(m_cols, axis=1)
    l_all = jnp.concatenate(l_cols, axis=1)

    s_own = _dot_nt(q_bb, kn_ref[0].astype(BF16))
    qt = lax.broadcasted_iota(jnp.int32, (MOBA_HEADS * r, r), 0) & (r - 1)
    kt = lax.broadcasted_iota(jnp.int32, (MOBA_HEADS * r, r), 1)
    own = (kt >= S_LO) & (kt <= qt)
    m_tot = jnp.maximum(jnp.max(jnp.where(sel, m_all, NEG_INF), axis=1, keepdims=True),
                        jnp.max(jnp.where(own, s_own, NEG_INF), axis=1, keepdims=True))
    wj = jnp.where(sel, jnp.exp(jnp.where(sel, m_all - m_tot, 0.0)), 0.0)
    p_own = jnp.where(own, jnp.exp(jnp.where(own, s_own - m_tot, 0.0)), 0.0)
    l_tot = (jnp.sum(wj * l_all, axis=1, keepdims=True)
             + jnp.sum(p_own, axis=1, keepdims=True))
    acc = _dot(p_own.astype(BF16), vn_ref[0].astype(BF16))
    for j in blocks:
        acc = acc + wj[:, j:j + 1] * accs[j]
    o_bd = acc / l_tot
    o = None
    for h in range(MOBA_HEADS):
        part = o_bd[h * r:(h + 1) * r, :] * head_masks[h]
        o = part if o is None else o + part
    o_ref[0] = o.astype(o_ref.dtype)

    @pl.when(i == n_seq - 1)
    def _():
        for d in range(1, PAGE_SLOTS):
            for cp in page_copies(n_seq - 1, lax.rem(i + d, PAGE_SLOTS)):
                cp.wait()


def _moba_sample(q, k_new, v_new, cache_kt, cache_vt, page_table, base):
    b, r, w = q.shape
    page = cache_kt.shape[2]
    n_pages = page_table.shape[1]
    assert page == PAGE_SIZE and r == S_ROWS and cache_kt.shape[1] == w and b >= PAGE_SLOTS
    tile = pl.BlockSpec((1, r, w), lambda i, pt: (i, 0, 0))
    hbm = pl.BlockSpec(memory_space=pl.ANY)
    grid_spec = pltpu.PrefetchScalarGridSpec(
        num_scalar_prefetch=1,
        grid=(b,),
        in_specs=[tile, tile, tile, hbm, hbm],
        out_specs=tile,
        scratch_shapes=[pltpu.VMEM((PAGE_SLOTS, n_pages, w, page), F32),
                        pltpu.VMEM((PAGE_SLOTS, n_pages, w, page), F32),
                        pltpu.SemaphoreType.DMA((2, PAGE_SLOTS))],
    )
    body = functools.partial(_moba_sample_body, n_pages=n_pages, base=base, n_seq=b)
    return pl.pallas_call(
        body,
        name="moba_sample",
        grid_spec=grid_spec,
        out_shape=jax.ShapeDtypeStruct((b, r, w), BF16),
        compiler_params=_params("arbitrary"),
    )(page_table, q, k_new, v_new, cache_kt, cache_vt)


def _trunk(x, weights, norm_final, *, sample=None):
    b, l, d = x.shape
    depth = len(weights)
    x = x.reshape(b * l, d)
    gr = GROUP_ROWS
    states = []
    gdn_states = None
    for layer, wts in enumerate(weights):
        x = _ffn(x, wts["n_ffn1"], wts["ffn1_g"], wts["ffn1_u"], wts["ffn1_d"])
        if sample is None:
            qkv, z, sc, ba, mk, mq, mv = _inproj(x, wts["n_mix"], wts["w_rows"], wts["w_qv_t"],
                                                 seq_len=l)
            grp = lambda a: a.reshape(b, l, a.shape[-1])
            o_a, o_b, gdn_states, gated = _mixer(
                grp(qkv), grp(z), grp(ba), grp(sc), wts["gdn_conv_w"], wts["gparams"],
                wts["gdn_norm_w"], wts["sc_conv_w"], c=GDN_CHUNK, ng=4 if b % 4 == 0 else 1,
                layer=layer, depth=depth, prev_states=gdn_states)
            o_c = _moba_prompt(mq, mk, mv)
        else:
            qkv, z, sc, ba, mk, mq, mv = _inproj(x, wts["n_mix"], wts["w_rows"], wts["w_qv"])
            ngroups = b * l // gr
            grp = lambda a: a.reshape(ngroups, gr, a.shape[-1])
            state = (grp(sample["gdn_conv"][layer]), grp(sample["sconv"][layer]), sample["gdn"])
            o_a, o_b, gdn_states, gated = _mixer(
                grp(qkv), grp(z), grp(ba), grp(sc), wts["gdn_conv_w"], wts["gparams"],
                wts["gdn_norm_w"], wts["sc_conv_w"], c=l, ng=2 if ngroups % 2 == 0 else 1,
                layer=layer, depth=depth, prev_states=gdn_states, state=state,
                row_lo=S_LO, row_hi=S_HI)
            seq = lambda a: a.reshape(b, l, a.shape[-1])
            o_c = _moba_sample(seq(mq), seq(mk), seq(mv), sample["cache_kt"], sample["cache_vt"],
                               sample["page_table"], layer * sample["n_pool"])
            o_c = o_c.reshape(b * l, -1)
        mix = (o_a.reshape(b * l, -1), o_b.reshape(b * l, -1), o_c, wts["w_out"])
        x = _ffn(x, wts["n_ffn2"], wts["ffn2_g"], wts["ffn2_u"], wts["ffn2_d"], mix=mix,
                 final_w=norm_final if layer == depth - 1 else None)
        states.append(dict(qkv=qkv.reshape(b, l, -1), gated=gated.reshape(b, l, -1),
                           k=mk.reshape(b, l, -1), v=mv))
    return x.reshape(b, l, d), states, gdn_states


def kernel(x_prompt, x_sample, state_gdn, state_gdn_conv, state_sconv, cache_k, cache_v,
           page_table, norm_ffn1, ffn1_w_gate, ffn1_w_up, ffn1_w_down, norm_mix, w_in,
           gdn_conv_w, gdn_a_log, gdn_dt_bias, gdn_norm_w, sc_conv_w, w_out,
           norm_ffn2, ffn2_w_gate, ffn2_w_up, ffn2_w_down, norm_final):
    depth = w_in.shape[0]
    gw, sw, mw = GDN_WIDTH, SC_WIDTH, MOBA_WIDTH
    nh = GDN_HEADS
    mh, md = MOBA_HEADS, MOBA_HEAD_DIM

    weights = []
    for la in range(depth):
        w = w_in[la]
        ba_off = 4 * gw
        sc_off = ba_off + 2 * nh
        mo_off = sc_off + 3 * sw
        w_rows = jnp.concatenate(
            [w[:, :ba_off], w[:, sc_off:mo_off],
             jnp.pad(w[:, ba_off:sc_off], ((0, 0), (0, LANES - 2 * nh))),
             w[:, mo_off + mw:mo_off + 2 * mw]], axis=1).astype(BF16)
        w_qv = jnp.concatenate([w[:, mo_off:mo_off + mw], w[:, mo_off + 2 * mw:mo_off + 3 * mw]],
                               axis=1).astype(BF16)
        gparams = jnp.zeros((SUBLANES, LANES), F32)
        gparams = gparams.at[0, nh:2 * nh].set(gdn_a_log[la].astype(F32))
        gparams = gparams.at[1, nh:2 * nh].set(gdn_dt_bias[la].astype(F32))
        weights.append(dict(
            n_ffn1=norm_ffn1[la], ffn1_g=_to_bf16(ffn1_w_gate, la),
            ffn1_u=_to_bf16(ffn1_w_up, la), ffn1_d=_to_bf16(ffn1_w_down, la),
            n_mix=norm_mix[la], w_rows=w_rows, w_qv=w_qv, w_qv_t=w_qv.T,
            gdn_conv_w=gdn_conv_w[la], gparams=gparams, gdn_norm_w=gdn_norm_w[la],
            sc_conv_w=sc_conv_w[la], w_out=_to_bf16(w_out, la),
            n_ffn2=norm_ffn2[la], ffn2_g=_to_bf16(ffn2_w_gate, la),
            ffn2_u=_to_bf16(ffn2_w_up, la), ffn2_d=_to_bf16(ffn2_w_down, la)))

    pb, pl_, _ = x_prompt.shape
    y_p, st_p, p_gdn = _trunk(x_prompt, weights, norm_final)
    hist = GDN_CONV - 1
    p_conv = jnp.stack([s["qkv"][:, -hist:] for s in st_p])
    p_sconv = jnp.stack([s["gated"][:, -(SC_CONV - 1):] for s in st_p])
    p_k = jnp.stack([s["k"] for s in st_p]).reshape(depth, pb, pl_, mh, md)
    p_v = jnp.stack([s["v"] for s in st_p]).reshape(depth, pb, mh, md, pl_)
    p_v = p_v.transpose(0, 1, 4, 2, 3)

    db, dl, d = x_sample.shape
    assert dl == S_HI - S_LO and (db * S_ROWS) % GROUP_ROWS == 0
    n_pool = cache_k.shape[1]
    tile_pad = lambda a, lo: jnp.pad(a, ((0, 0),) * (a.ndim - 2)
                                     + ((lo, S_ROWS - lo - a.shape[-2]), (0, 0)))
    seqs_per_group = GROUP_ROWS // S_ROWS
    paged_t = lambda c: c.transpose(0, 1, 3, 4, 2).reshape(depth * n_pool, mw, PAGE_SIZE)
    sample = dict(
        gdn=state_gdn.reshape(depth, db // seqs_per_group, seqs_per_group * nh,
                              GDN_HEAD_DIM, GDN_HEAD_DIM),
        gdn_conv=tile_pad(state_gdn_conv, 0),
        sconv=tile_pad(state_sconv, S_LO - (SC_CONV - 1)),
        cache_kt=paged_t(cache_k), cache_vt=paged_t(cache_v), n_pool=n_pool,
        page_table=page_table)
    y_s, st_s, s_gdn = _trunk(tile_pad(x_sample, S_LO), weights, norm_final, sample=sample)
    y_s = y_s[:, S_LO:S_HI]
    s_gdn = s_gdn.reshape(state_gdn.shape)
    s_conv = jnp.stack([s["qkv"][:, S_HI - hist:S_HI] for s in st_s])
    s_sconv = jnp.stack([s["gated"][:, S_HI - (SC_CONV - 1):S_HI] for s in st_s])
    s_k = jnp.stack([s["k"][:, S_LO:S_HI] for s in st_s]).reshape(depth, db, dl, mh, md)
    s_v = jnp.stack([s["v"].reshape(db, S_ROWS, mw)[:, S_LO:S_HI] for s in st_s]).reshape(
        depth, db, dl, mh, md)
    return (y_p, y_s, p_gdn, p_conv, p_sconv, p_k, p_v, s_gdn, s_conv, s_sconv, s_k, s_v)
```

```python
import functools

import jax
import jax.numpy as jnp
from jax import lax
from jax.experimental import pallas as pl
from jax.experimental.pallas import tpu as pltpu

NORM_EPS = 1e-6
NEG_INF = -1e30

GDN_HEADS = 4
GDN_HEAD_DIM = 128
GDN_WIDTH = GDN_HEADS * GDN_HEAD_DIM
GDN_CONV = 4
GDN_CHUNK = 64
SC_WIDTH = 256
SC_CONV = 3
MOBA_HEADS = 4
MOBA_HEAD_DIM = 64
MOBA_WIDTH = MOBA_HEADS * MOBA_HEAD_DIM
MOBA_BLOCK = 256
MOBA_TOPK = 3
PAGE_SIZE = 128

LANES = 128
SUBLANES = 8
VMEM_LIMIT_BYTES = 56 * 1024 * 1024

S_ROWS = SUBLANES
S_LO = GDN_CONV - 1
S_HI = S_LO + 4

GROUP_ROWS = GDN_CHUNK
STACK_ROWS = GDN_HEADS * GROUP_ROWS

F32 = jnp.float32
BF16 = jnp.bfloat16
HI = lax.Precision.HIGHEST


def _dot(a, b, precision=None):
    return jnp.dot(a, b, preferred_element_type=F32, precision=precision)


def _dot_nt(a, b, precision=None):
    return lax.dot_general(a, b, (((1,), (1,)), ((), ())),
                           preferred_element_type=F32, precision=precision)


def _dot_tn(a, b, precision=None):
    return lax.dot_general(a, b, (((0,), (0,)), ((), ())),
                           preferred_element_type=F32, precision=precision)


def _rms(x, w):
    return x * lax.rsqrt(jnp.mean(x * x, axis=-1, keepdims=True) + NORM_EPS) * w


def _silu(x):
    return x * jax.nn.sigmoid(x)


def _softplus(x):
    return jnp.maximum(x, 0.0) + jnp.log1p(jnp.exp(-jnp.abs(x)))


def _resident(shape):
    nd = len(shape)
    return pl.BlockSpec(shape, lambda *_: (0,) * nd, pipeline_mode=pl.Buffered(1))


def _params(*semantics):
    return pltpu.CompilerParams(dimension_semantics=semantics,
                                vmem_limit_bytes=VMEM_LIMIT_BYTES)


def _cast_body(w_ref, o_ref):
    o_ref[...] = w_ref[0].astype(o_ref.dtype)


def _to_bf16(w, layer, rows=256):
    _, r, c = w.shape
    rows = min(rows, r)
    assert r % rows == 0
    return pl.pallas_call(
        _cast_body,
        name="to_bf16",
        grid=(r // rows,),
        in_specs=[pl.BlockSpec((1, rows, c), lambda i: (layer, i, 0))],
        out_specs=pl.BlockSpec((rows, c), lambda i: (i, 0)),
        out_shape=jax.ShapeDtypeStruct((r, c), BF16),
        compiler_params=_params("arbitrary"),
    )(w)


def _ffn_body(*refs, has_mix, oc_transposed, has_final, f_chunk):
    refs = list(refs)
    x_ref = refs.pop(0)
    if has_mix:
        oa_ref, ob_ref, oc_ref, wo_ref = refs[:4]
        refs = refs[4:]
    nw_ref, wg_ref, wu_ref, wd_ref = refs[:4]
    refs = refs[4:]
    if has_final:
        nf_ref = refs.pop(0)
    out_ref, xn_ref, acc_ref = refs

    x = x_ref[...]
    if has_mix:
        ga = oa_ref.shape[1]
        gb = ga + ob_ref.shape[1]
        x = x + _dot(oa_ref[...], wo_ref[0:ga, :]) + _dot(ob_ref[...], wo_ref[ga:gb, :])
        if oc_transposed:
            x = x + _dot_tn(oc_ref[0], wo_ref[gb:, :])
        else:
            x = x + _dot(oc_ref[...], wo_ref[gb:, :])
    xn_ref[...] = _rms(x, nw_ref[...]).astype(BF16)
    d_ff = wg_ref.shape[2]
    for c in range(d_ff // f_chunk):
        lo = c * f_chunk
        xn = xn_ref[...]
        g = _dot(xn, wg_ref[0, :, lo:lo + f_chunk].astype(BF16))
        u = _dot(xn, wu_ref[0, :, lo:lo + f_chunk].astype(BF16))
        h = (_silu(g) * u).astype(BF16)
        part = _dot(h, wd_ref[0, lo:lo + f_chunk, :].astype(BF16))
        if c == 0:
            acc_ref[...] = part
        else:
            acc_ref[...] += part
    y = x + 0.5 * acc_ref[...]
    if has_final:
        y = _rms(y, nf_ref[...])
    out_ref[...] = y


def _ffn(x, nw, wg, wu, wd, layer, mix=None, final_w=None, tm=512, f_chunk=256):
    m, d = x.shape
    d_ff = wg.shape[2]
    of_layer = lambda w: pl.BlockSpec((1,) + w.shape[1:], lambda *_: (layer, 0, 0),
                                      pipeline_mode=pl.Buffered(1))
    tm = min(tm, m)
    assert m % tm == 0 and d_ff % f_chunk == 0
    row = lambda i: (i, 0)
    in_specs = [pl.BlockSpec((tm, d), row)]
    args = [x]
    oc_transposed = False
    if mix is not None:
        oa, ob, oc, wo = mix
        oc_transposed = oc.ndim == 3
        if oc_transposed:
            tiles = oc.shape[2] // tm
            assert oc.shape[2] % tm == 0
            oc_spec = pl.BlockSpec((1, oc.shape[1], tm), lambda i: (i // tiles, 0, i % tiles))
        else:
            oc_spec = pl.BlockSpec((tm, oc.shape[1]), row)
        in_specs += [pl.BlockSpec((tm, oa.shape[1]), row), pl.BlockSpec((tm, ob.shape[1]), row),
                     oc_spec, _resident(wo.shape)]
        args += [oa, ob, oc, wo]
    in_specs += [_resident((1, d)), of_layer(wg), of_layer(wu), of_layer(wd)]
    args += [nw.reshape(1, d), wg, wu, wd]
    if final_w is not None:
        in_specs.append(_resident((1, d)))
        args.append(final_w.reshape(1, d))
    body = functools.partial(_ffn_body, has_mix=mix is not None, oc_transposed=oc_transposed,
                             has_final=final_w is not None, f_chunk=f_chunk)
    return pl.pallas_call(
        body,
        name="ffn_mix" if mix is not None else "ffn",
        grid=(m // tm,),
        in_specs=in_specs,
        out_specs=pl.BlockSpec((tm, d), row),
        out_shape=jax.ShapeDtypeStruct((m, d), F32),
        scratch_shapes=[pltpu.VMEM((tm, d), BF16), pltpu.VMEM((tm, d), F32)],
        compiler_params=_params("arbitrary"),
    )(*args)


_ROW_WIDTHS = (3 * GDN_WIDTH, GDN_WIDTH, 3 * SC_WIDTH, LANES, MOBA_WIDTH)


def _causal_conv(buf, x, w_ref):
    pad = SUBLANES
    n = x.shape[0]
    taps = w_ref.shape[0]
    buf[pad:pad + n, :] = x
    y = None
    for t in range(taps - 1):
        lo = pad - (taps - 1) + t
        term = buf[lo:lo + n, :] * w_ref[t:t + 1, :]
        y = term if y is None else y + term
    y = y + x * w_ref[taps - 1:taps, :]
    tail = buf[n:n + pad, :]
    buf[0:pad, :] = tail
    return y


def _l2_normalize(x):
    return x * lax.rsqrt(jnp.sum(x * x, axis=-1, keepdims=True) + NORM_EPS)


def _gates(ba, gp_ref):
    return jax.nn.sigmoid(ba), -jnp.exp(gp_ref[0:1, :]) * _softplus(ba + gp_ref[1:2, :])


def _inproj_body(x_ref, nw_ref, w_ref, wqv_ref, *out_refs, transposed_qv):
    xn = _rms(x_ref[...], nw_ref[...]).astype(BF16)
    lo = 0
    for ref in out_refs[:len(_ROW_WIDTHS)]:
        width = ref.shape[1]
        ref[...] = _dot(xn, w_ref[:, lo:lo + width])
        lo += width
    q_ref, v_ref = out_refs[len(_ROW_WIDTHS):]
    mw = MOBA_WIDTH
    if transposed_qv:
        q_ref[0] = _dot_nt(wqv_ref[0:mw, :], xn)
        v_ref[0] = _dot_nt(wqv_ref[mw:2 * mw, :], xn)
    else:
        q_ref[...] = _dot(xn, wqv_ref[:, 0:mw])
        v_ref[...] = _dot(xn, wqv_ref[:, mw:2 * mw])


def _inproj(x, nw, w_rows, w_qv, *, seq_len=None, tm=512):
    m, d = x.shape
    tm = min(tm, m)
    assert m % tm == 0 and w_rows.shape[1] == sum(_ROW_WIDTHS)
    row = lambda i: (i, 0)
    out_specs = [pl.BlockSpec((tm, wd), row) for wd in _ROW_WIDTHS]
    out_shape = [jax.ShapeDtypeStruct((m, wd), F32) for wd in _ROW_WIDTHS]
    transposed_qv = seq_len is not None
    if transposed_qv:
        assert seq_len % tm == 0
        tiles = seq_len // tm
        spec = pl.BlockSpec((1, MOBA_WIDTH, tm), lambda i: (i // tiles, 0, i % tiles))
        shape = jax.ShapeDtypeStruct((m // seq_len, MOBA_WIDTH, seq_len), F32)
    else:
        spec = pl.BlockSpec((tm, MOBA_WIDTH), row)
        shape = jax.ShapeDtypeStruct((m, MOBA_WIDTH), F32)
    out_specs += [spec, spec]
    out_shape += [shape, shape]
    return pl.pallas_call(
        functools.partial(_inproj_body, transposed_qv=transposed_qv),
        name="inproj",
        grid=(m // tm,),
        in_specs=[pl.BlockSpec((tm, d), row), _resident((1, d)), _resident(w_rows.shape),
                  _resident(w_qv.shape)],
        out_specs=out_specs,
        out_shape=out_shape,
        compiler_params=_params("arbitrary"),
    )(x, nw.reshape(1, d), w_rows, w_qv)


def _mixer_body(*refs, c, ng, has_state, row_lo, row_hi):
    refs = list(refs)
    qkv_ref, z_ref, ba_ref, sc_ref = refs[:4]
    refs = refs[4:]
    if has_state:
        qh_ref, sh_ref, s0_ref = refs[:3]
        refs = refs[3:]
    cw_ref, gp_ref, nw_ref, scw_ref = refs[:4]
    refs = refs[5:]
    oa_ref, ob_ref, st_ref, gt_ref = refs[:4]
    xbuf, gbuf, s_ref = refs[4:]
    n = pl.program_id(1)
    hd = GDN_HEAD_DIM
    gr = GROUP_ROWS
    sr = STACK_ROWS
    pad = SUBLANES
    seqs = gr // c
    shift = c.bit_length() - 1
    assert 1 << shift == c

    @pl.when(n == 0)
    def _():
        xbuf[:, 0:pad, :] = jnp.zeros((ng, pad, xbuf.shape[2]), F32)
        gbuf[:, 0:pad, :] = jnp.zeros((ng, pad, gbuf.shape[2]), F32)
        if has_state:
            s_ref[...] = s0_ref[0]
        else:
            s_ref[...] = jnp.zeros(s_ref.shape, F32)

    rows = lax.broadcasted_iota(jnp.int32, (gr, 1), 0) & (c - 1)
    is_hist = rows < row_lo
    live = jnp.where((rows >= row_lo) & (rows < row_hi), 1.0, 0.0)
    ri = lax.broadcasted_iota(jnp.int32, (gr, gr), 0)
    ci = lax.broadcasted_iota(jnp.int32, (gr, gr), 1)
    same = (ri >> shift) == (ci >> shift)
    cum_op = jnp.concatenate([jnp.where(same & (ri >= ci), 1.0, 0.0),
                              jnp.where(same, 1.0, 0.0)], axis=0)
    nh = GDN_HEADS
    tiles = sr // LANES
    band_tile = [h * gr // LANES for h in range(nh)]

    def bands(mat):
        return [mat[h * gr:(h + 1) * gr, band_tile[h] * LANES:(band_tile[h] + 1) * LANES]
                for h in range(nh)]

    def unband(pieces):
        zero = jnp.zeros((gr, LANES), BF16)
        return jnp.concatenate(
            [jnp.concatenate([p.astype(BF16) if t == band_tile[h] else zero
                              for t in range(tiles)], axis=1)
             for h, p in enumerate(pieces)], axis=0)

    incl, strict, eye = [], [], []
    for h in range(nh):
        rs = lax.broadcasted_iota(jnp.int32, (gr, LANES), 0) + h * gr
        cs = lax.broadcasted_iota(jnp.int32, (gr, LANES), 1) + band_tile[h] * LANES
        same_s = (rs >> shift) == (cs >> shift)
        incl.append(same_s & (rs >= cs))
        strict.append(same_s & (rs > cs))
        eye.append(jnp.where(rs == cs, 1.0, 0.0))

    def stack_heads(a, lo):
        return jnp.concatenate([a[:, lo + h * hd:lo + (h + 1) * hd] for h in range(GDN_HEADS)],
                               axis=0)

    def stack_cols(a, lo):
        return jnp.concatenate([a[:, lo + h:lo + h + 1] for h in range(GDN_HEADS)], axis=0)

    groups = range(ng)

    def front(g):
        x = qkv_ref[g]
        if has_state:
            x = jnp.where(is_hist, qh_ref[g], x)
        act = _silu(_causal_conv(xbuf.at[g], x, cw_ref))
        q = _l2_normalize(stack_heads(act, 0)) * (hd ** -0.5)
        k = _l2_normalize(stack_heads(act, GDN_WIDTH))
        v = stack_heads(act, 2 * GDN_WIDTH)

        sc = sc_ref[g]
        scw = SC_WIDTH
        gated = sc[:, scw:2 * scw] * sc[:, 0:scw]
        if has_state:
            gated = jnp.where(is_hist, sh_ref[g], gated)
        yb = _causal_conv(gbuf.at[g], gated, scw_ref)
        ob_ref[g] = (sc[:, 2 * scw:3 * scw] * yb).astype(ob_ref.dtype)
        gt_ref[g] = gated

        beta_all, g_all = _gates(ba_ref[g], gp_ref)
        beta_all, g_all = beta_all * live, g_all * live
        cum = _dot(cum_op, g_all, HI)
        gc_all, glast_all = cum[:gr], cum[gr:]
        beta = stack_cols(beta_all, 0)
        gc = stack_cols(gc_all, nh)
        eg = jnp.exp(gc)
        etail = jnp.exp(stack_cols(glast_all, nh) - gc)
        gc_row = jnp.broadcast_to(gc, (sr, LANES)).T[0:1, :]
        decay = []
        for h in range(nh):
            diff = (gc[h * gr:(h + 1) * gr]
                    - gc_row[:, band_tile[h] * LANES:(band_tile[h] + 1) * LANES])
            decay.append(jnp.where(incl[h], jnp.exp(jnp.where(incl[h], diff, 0.0)), 0.0))

        kb = k.astype(BF16)
        kk = bands(_dot_nt(kb, kb))
        m_strict = [jnp.where(strict[h], beta[h * gr:(h + 1) * gr] * kk[h] * decay[h], 0.0)
                    for h in range(nh)]
        return dict(q=q, k=k, v=v, kb=kb, beta=beta, eg=eg, etail=etail, decay=decay,
                    cd_all=jnp.exp(glast_all), m_strict=m_strict)

    fr = [front(g) for g in groups]

    invs = [[e - m for e, m in zip(eye, f["m_strict"])] for f in fr]
    pbs = [unband(f["m_strict"]) for f in fr]
    for _ in range(shift - 1):
        pbs = [unband(bands(_dot(pb, pb))) for pb in pbs]
        invs = [[i + d for i, d in zip(inv, bands(_dot(unband(inv), pb)))]
                for inv, pb in zip(invs, pbs)]

    def solve(f, inv):
        rhs = jnp.concatenate([f["k"] * (f["beta"] * f["eg"]), f["v"] * f["beta"]],
                              axis=-1).astype(BF16)
        sol = _dot(unband(inv), rhs)
        qk = bands(_dot_nt(f["q"].astype(BF16), f["kb"]))
        attn = unband([a * d for a, d in zip(qk, f["decay"])])
        return dict(w_k=sol[:, :hd], u_v=sol[:, hd:], attn=attn, qd=f["q"] * f["eg"],
                    kt=(f["k"] * f["etail"]).astype(BF16))

    sv = [solve(f, inv) for f, inv in zip(fr, invs)]

    pair_ids = [(h, s) for h in range(GDN_HEADS) for s in range(seqs)]

    def read_state(g, t):
        parts = []
        for h, s in pair_ids:
            r0 = h * gr + s * c
            s_old = s_ref[g, s * nh + h]
            lhs = jnp.concatenate([t["w_k"][r0:r0 + c], t["qd"][r0:r0 + c]], axis=0)
            res = _dot(lhs.astype(BF16), s_old.astype(BF16))
            parts.append((s_old, t["u_v"][r0:r0 + c] - res[:c], res[c:]))
        return parts

    rd = [read_state(g, t) for g, t in zip(groups, sv)]
    us = [jnp.concatenate([p[1] for p in parts], axis=0) for parts in rd]
    outs = [jnp.concatenate([p[2] for p in parts], axis=0) + _dot(t["attn"], u.astype(BF16))
            for parts, t, u in zip(rd, sv, us)]
    for g in groups:
        for (h, s), (s_old, u_p, _) in zip(pair_ids, rd[g]):
            r0 = h * gr + s * c
            cd = fr[g]["cd_all"][s * c:s * c + 1, nh + h:nh + h + 1]
            s_ref[g, s * nh + h] = s_old * cd + _dot_tn(sv[g]["kt"][r0:r0 + c],
                                                        u_p.astype(BF16))
    for g in groups:
        zs = stack_heads(z_ref[g], 0)
        o_n = _rms(outs[g], nw_ref[...]) * _silu(zs)
        oa_ref[g] = jnp.concatenate([o_n[h * gr:(h + 1) * gr] for h in range(GDN_HEADS)],
                                    axis=1).astype(oa_ref.dtype)

    @pl.when(n == pl.num_programs(1) - 1)
    def _():
        st_ref[0] = s_ref[...]


def _mixer(qkv, z, ba, sc, conv_w, gparams, norm_w, sc_conv_w, *, c, ng, layer, depth,
           prev_states=None, state=None, row_lo=0, row_hi=None):
    g_total, l, _ = qkv.shape
    gr = GROUP_ROWS
    assert l % gr == 0 and gr % c == 0 and g_total % ng == 0
    nc = l // gr
    pairs = GDN_HEADS * (gr // c)
    row_hi = c if row_hi is None else row_hi
    blk = lambda w: pl.BlockSpec((ng, gr, w), lambda i, n: (i, n, 0))
    in_specs = [blk(qkv.shape[2]), blk(z.shape[2]), blk(ba.shape[2]), blk(sc.shape[2])]
    args = [qkv, z, ba, sc]
    st_shape = (g_total, pairs, GDN_HEAD_DIM, GDN_HEAD_DIM)
    st_spec = pl.BlockSpec((1, ng) + st_shape[1:], lambda i, n: (layer, i, 0, 0, 0))
    if state is not None:
        assert nc == 1
        qh, sh, s0 = state
        assert s0.shape[1:] == st_shape
        in_specs += [pl.BlockSpec((ng, gr, qh.shape[2]), lambda i, n: (i, 0, 0)),
                     pl.BlockSpec((ng, gr, sh.shape[2]), lambda i, n: (i, 0, 0)), st_spec]
        args += [qh, sh, s0]
    const = lambda a: pl.BlockSpec(a.shape, lambda i, n: (0,) * a.ndim)
    norm_w = norm_w.reshape(1, -1)
    in_specs += [const(conv_w), const(gparams), const(norm_w), const(sc_conv_w)]
    args += [conv_w, gparams, norm_w, sc_conv_w]
    if prev_states is None:
        prev_states = jnp.zeros((depth,) + st_shape, F32)
    assert prev_states.shape == (depth,) + st_shape
    aliases = {len(args): 2}
    in_specs.append(pl.BlockSpec(memory_space=pl.ANY))
    args.append(prev_states)
    out_shape = [jax.ShapeDtypeStruct((g_total, l, GDN_WIDTH), BF16),
                 jax.ShapeDtypeStruct((g_total, l, SC_WIDTH), BF16),
                 jax.ShapeDtypeStruct((depth,) + st_shape, F32),
                 jax.ShapeDtypeStruct((g_total, l, SC_WIDTH), F32)]
    out_specs = [blk(GDN_WIDTH), blk(SC_WIDTH), st_spec, blk(SC_WIDTH)]
    body = functools.partial(_mixer_body, c=c, ng=ng, has_state=state is not None,
                             row_lo=row_lo, row_hi=row_hi)
    return pl.pallas_call(
        body,
        name="seq_mixer",
        grid=(g_total // ng, nc),
        in_specs=in_specs,
        out_specs=out_specs,
        out_shape=out_shape,
        input_output_aliases=aliases,
        scratch_shapes=[pltpu.VMEM((ng, gr + SUBLANES, qkv.shape[2]), F32),
                        pltpu.VMEM((ng, gr + SUBLANES, SC_WIDTH), F32),
                        pltpu.VMEM((ng,) + st_shape[1:], F32)],
        compiler_params=_params("arbitrary", "arbitrary"),
    )(*args)


def _select_blocks(gate, n_valid, axis=0):
    nb = gate.shape[axis]
    sub = lax.broadcasted_iota(jnp.int32, gate.shape, axis)
    gate = jnp.where(sub < n_valid, gate, NEG_INF)
    sel = jnp.zeros(gate.shape, F32)
    for j in range(nb):
        gj = gate[j:j + 1, :] if axis == 0 else gate[:, j:j + 1]
        beats = (gate > gj) | ((gate == gj) & (sub < j))
        cnt = jnp.sum(jnp.where(beats, 1.0, 0.0), axis=axis, keepdims=True)
        hit = (cnt < MOBA_TOPK) & (j < n_valid)
        sel = jnp.where((sub == j) & hit, 1.0, sel)
    return sel


_PV_ROWS = MOBA_HEAD_DIM + 16


def _moba_prompt_body(q_ref, k_ref, v_ref, o_ref, kmean_ref, kb_ref, vb_ref, sel_ref, *, nbat):
    i = pl.program_id(1)
    blk = MOBA_BLOCK
    hd = MOBA_HEAD_DIM
    w = MOBA_WIDTH
    l_seq = v_ref.shape[2]
    nb = l_seq // blk
    heads = range(MOBA_HEADS)
    head_rows = [slice(h * hd, (h + 1) * hd) for h in heads]
    chains = [(b, h) for b in range(nbat) for h in heads]

    @pl.when(i == 0)
    def _():
        ones = jnp.ones((_PV_ROWS - hd, blk), BF16)
        for b in range(nbat):
            for j in range(nb):
                kj = k_ref[b * l_seq + j * blk:b * l_seq + (j + 1) * blk, :]
                kmean_ref[b, j:j + 1, :] = jnp.mean(kj, axis=0, keepdims=True)
                kb_ref[b, j] = kj.astype(BF16)
                for h in heads:
                    vb_ref[b, j, h, 0:hd, :] = v_ref[b, head_rows[h],
                                                     j * blk:(j + 1) * blk].astype(BF16)
                    vb_ref[b, j, h, hd:_PV_ROWS, :] = ones

    feat = lax.broadcasted_iota(jnp.int32, (w, blk), 0)
    key_i = lax.broadcasted_iota(jnp.int32, (blk, blk), 0)
    qry_i = lax.broadcasted_iota(jnp.int32, (blk, blk), 1)
    causal = key_i <= qry_i
    qmb = []
    for b in range(nbat):
        qt = q_ref[b] * (hd ** -0.5)
        qms = [jnp.where((feat >= h * hd) & (feat < (h + 1) * hd), qt, 0.0) for h in heads]
        qmb.append(jnp.concatenate(qms, axis=1).astype(BF16))
        for h in heads:
            gate = _dot(kmean_ref[b], qms[h], HI)
            sel_ref[b, h] = _select_blocks(gate, i)

    def scores(j):
        s_all = [_dot(kb_ref[b, j], qmb[b]) for b in range(nbat)]
        return [s_all[b][:, h * blk:(h + 1) * blk] for b, h in chains]

    def attend(j, s_list, m_list):
        ps = [jnp.exp(s - m).astype(BF16) for s, m in zip(s_list, m_list)]
        return [_dot(vb_ref[b, j, h], p) for (b, h), p in zip(chains, ps)]

    s0 = [jnp.where(causal, s, NEG_INF) for s in scores(i)]
    m0 = [jnp.max(s, axis=0, keepdims=True) for s in s0]
    pv0 = attend(i, s0, m0)
    init = []
    for m, pv in zip(m0, pv0):
        init += [m, pv[hd:hd + 1], pv[:hd]]

    def past_block(j, carry):
        s_list = [jnp.where(sel_ref[b, h, pl.ds(j, 1), :] > 0.5, s, NEG_INF)
                  for (b, h), s in zip(chains, scores(j))]
        m_old = carry[0::3]
        m_new = [jnp.maximum(m, jnp.max(s, axis=0, keepdims=True))
                 for m, s in zip(m_old, s_list)]
        pv = attend(j, s_list, m_new)
        new = []
        for c in range(len(chains)):
            alpha = jnp.exp(m_old[c] - m_new[c])
            new += [m_new[c], alpha * carry[3 * c + 1] + pv[c][hd:hd + 1],
                    alpha * carry[3 * c + 2] + pv[c][:hd]]
        return tuple(new)

    final = lax.fori_loop(0, i, past_block, tuple(init))
    for c, (b, h) in enumerate(chains):
        o_ref[b, head_rows[h], :] = (final[3 * c + 2] / final[3 * c + 1]).astype(o_ref.dtype)


def _moba_prompt(qt, k, vt, nbat=2):
    b, w, l = qt.shape
    blk = MOBA_BLOCK
    nbat = nbat if b % nbat == 0 else 1
    assert l % blk == 0
    nb = l // blk
    return pl.pallas_call(
        functools.partial(_moba_prompt_body, nbat=nbat),
        name="moba_prompt",
        grid=(b // nbat, nb),
        in_specs=[pl.BlockSpec((nbat, w, blk), lambda i, t: (i, 0, t)),
                  pl.BlockSpec((nbat * l, w), lambda i, t: (i, 0)),
                  pl.BlockSpec((nbat, w, l), lambda i, t: (i, 0, 0))],
        out_specs=pl.BlockSpec((nbat, w, blk), lambda i, t: (i, 0, t)),
        out_shape=jax.ShapeDtypeStruct((b, w, l), BF16),
        scratch_shapes=[pltpu.VMEM((nbat, nb, w), F32), pltpu.VMEM((nbat, nb, blk, w), BF16),
                        pltpu.VMEM((nbat, nb, MOBA_HEADS, _PV_ROWS, blk), BF16),
                        pltpu.VMEM((nbat, MOBA_HEADS, nb, blk), F32)],
        compiler_params=_params("arbitrary", "arbitrary"),
    )(qt, k, vt)


PAGE_SLOTS = 3


def _moba_sample_body(pt_ref, q_ref, kn_ref, vn_ref, ck_ref, cv_ref, o_ref, kbuf, vbuf, sem,
                      *, n_pages, base, n_seq):
    i = pl.program_id(0)
    hd = MOBA_HEAD_DIM
    r = S_ROWS
    w = MOBA_WIDTH
    ppb = MOBA_BLOCK // PAGE_SIZE
    nb = n_pages // ppb
    ahead = PAGE_SLOTS - 1

    def page_copies(seq, slot):
        cps = []
        for p in range(n_pages):
            row = base + pt_ref[seq, p]
            cps.append(pltpu.make_async_copy(ck_ref.at[row], kbuf.at[slot, p], sem.at[0, slot]))
            cps.append(pltpu.make_async_copy(cv_ref.at[row], vbuf.at[slot, p], sem.at[1, slot]))
        return cps

    def start_all(cps):
        for n, cp in enumerate(cps):
            cp.start(priority=n % 2)

    @pl.when(i == 0)
    def _():
        for d in range(ahead):
            start_all(page_copies(d, d))

    slot = lax.rem(i, PAGE_SLOTS)
    for cp in page_copies(i, slot):
        cp.wait()

    lane = lax.broadcasted_iota(jnp.int32, (r, w), 1)
    head_masks = [jnp.where((lane >= h * hd) & (lane < (h + 1) * hd), 1.0, 0.0)
                  for h in range(MOBA_HEADS)]
    q = q_ref[0] * (hd ** -0.5)
    q_bd = jnp.concatenate([q * hm for hm in head_masks], axis=0)
    q_bb = q_bd.astype(BF16)
    col = lax.broadcasted_iota(jnp.int32, (w, LANES), 1)
    blocks = range(nb)

    kp = [kbuf[slot, p] for p in range(n_pages)]
    s_pages = [_dot(q_bb, x.astype(BF16)) for x in kp]
    kmean_t = jnp.zeros((w, LANES), F32)
    for j in blocks:
        ksum = kp[j * ppb]
        for pp in range(1, ppb):
            ksum = ksum + kp[j * ppb + pp]
        kmean_t = jnp.where(col == j, jnp.sum(ksum, axis=1, keepdims=True) * (1.0 / MOBA_BLOCK),
                            kmean_t)
    s_blk = [jnp.concatenate(s_pages[j * ppb:(j + 1) * ppb], axis=-1) for j in blocks]
    m_cols = [jnp.max(s, axis=1, keepdims=True) for s in s_blk]
    e_blk = [jnp.exp(s - m) for s, m in zip(s_blk, m_cols)]
    l_cols = [jnp.sum(e, axis=1, keepdims=True) for e in e_blk]
    accs = []
    for j in blocks:
        eb = e_blk[j].astype(BF16)
        acc = None
        for pp in range(ppb):
            vp = vbuf[slot, j * ppb + pp].astype(BF16)
            part = _dot_nt(eb[:, pp * PAGE_SIZE:(pp + 1) * PAGE_SIZE], vp)
            acc = part if acc is None else acc + part
        accs.append(acc)

    nxt = jnp.minimum(i + ahead, n_seq - 1)
    start_all(page_copies(nxt, lax.rem(i + ahead, PAGE_SLOTS)))

    gate = _dot(q_bd, kmean_t, HI)[:, :nb]
    sel = _select_blocks(gate, nb, axis=1) > 0.5
    m_all = jnp.concatenate(m_cols, axis=1)
    l_all = jnp.concatenate(l_cols, axis=1)

    s_own = _dot_nt(q_bb, kn_ref[0].astype(BF16))
    qt = lax.broadcasted_iota(jnp.int32, (MOBA_HEADS * r, r), 0) & (r - 1)
    kt = lax.broadcasted_iota(jnp.int32, (MOBA_HEADS * r, r), 1)
    own = (kt >= S_LO) & (kt <= qt)
    m_tot = jnp.maximum(jnp.max(jnp.where(sel, m_all, NEG_INF), axis=1, keepdims=True),
                        jnp.max(jnp.where(own, s_own, NEG_INF), axis=1, keepdims=True))
    wj = jnp.where(sel, jnp.exp(jnp.where(sel, m_all - m_tot, 0.0)), 0.0)
    p_own = jnp.where(own, jnp.exp(jnp.where(own, s_own - m_tot, 0.0)), 0.0)
    l_tot = (jnp.sum(wj * l_all, axis=1, keepdims=True)
             + jnp.sum(p_own, axis=1, keepdims=True))
    acc = _dot(p_own.astype(BF16), vn_ref[0].astype(BF16))
    for j in blocks:
        acc = acc + wj[:, j:j + 1] * accs[j]
    o_bd = acc / l_tot
    o = None
    for h in range(MOBA_HEADS):
        part = o_bd[h * r:(h + 1) * r, :] * head_masks[h]
        o = part if o is None else o + part
    o_ref[0] = o.astype(o_ref.dtype)

    @pl.when(i == n_seq - 1)
    def _():
        for d in range(1, PAGE_SLOTS):
            for cp in page_copies(n_seq - 1, lax.rem(i + d, PAGE_SLOTS)):
                cp.wait()


def _moba_sample(q, k_new, v_new, cache_kt, cache_vt, page_table, base):
    b, r, w = q.shape
    page = cache_kt.shape[2]
    n_pages = page_table.shape[1]
    assert page == PAGE_SIZE and r == S_ROWS and cache_kt.shape[1] == w and b >= PAGE_SLOTS
    tile = pl.BlockSpec((1, r, w), lambda i, pt: (i, 0, 0))
    hbm = pl.BlockSpec(memory_space=pl.ANY)
    grid_spec = pltpu.PrefetchScalarGridSpec(
        num_scalar_prefetch=1,
        grid=(b,),
        in_specs=[tile, tile, tile, hbm, hbm],
        out_specs=tile,
        scratch_shapes=[pltpu.VMEM((PAGE_SLOTS, n_pages, w, page), F32),
                        pltpu.VMEM((PAGE_SLOTS, n_pages, w, page), F32),
                        pltpu.SemaphoreType.DMA((2, PAGE_SLOTS))],
    )
    body = functools.partial(_moba_sample_body, n_pages=n_pages, base=base, n_seq=b)
    return pl.pallas_call(
        body,
        name="moba_sample",
        grid_spec=grid_spec,
        out_shape=jax.ShapeDtypeStruct((b, r, w), BF16),
        compiler_params=_params("arbitrary"),
    )(page_table, q, k_new, v_new, cache_kt, cache_vt)


def _trunk(x, weights, ffn_w, norm_final, *, sample=None):
    b, l, d = x.shape
    depth = len(weights)
    x = x.reshape(b * l, d)
    gr = GROUP_ROWS
    states = []
    gdn_states = None
    for layer, wts in enumerate(weights):
        x = _ffn(x, wts["n_ffn1"], *ffn_w[0], layer)
        if sample is None:
            qkv, z, sc, ba, mk, mq, mv = _inproj(x, wts["n_mix"], wts["w_rows"], wts["w_qv_t"],
                                                 seq_len=l)
            grp = lambda a: a.reshape(b, l, a.shape[-1])
            o_a, o_b, gdn_states, gated = _mixer(
                grp(qkv), grp(z), grp(ba), grp(sc), wts["gdn_conv_w"], wts["gparams"],
                wts["gdn_norm_w"], wts["sc_conv_w"], c=GDN_CHUNK, ng=4 if b % 4 == 0 else 1,
                layer=layer, depth=depth, prev_states=gdn_states)
            o_c = _moba_prompt(mq, mk, mv)
        else:
            qkv, z, sc, ba, mk, mq, mv = _inproj(x, wts["n_mix"], wts["w_rows"], wts["w_qv"])
            ngroups = b * l // gr
            grp = lambda a: a.reshape(ngroups, gr, a.shape[-1])
            state = (grp(sample["gdn_conv"][layer]), grp(sample["sconv"][layer]), sample["gdn"])
            o_a, o_b, gdn_states, gated = _mixer(
                grp(qkv), grp(z), grp(ba), grp(sc), wts["gdn_conv_w"], wts["gparams"],
                wts["gdn_norm_w"], wts["sc_conv_w"], c=l, ng=2 if ngroups % 2 == 0 else 1,
                layer=layer, depth=depth, prev_states=gdn_states, state=state,
                row_lo=S_LO, row_hi=S_HI)
            seq = lambda a: a.reshape(b, l, a.shape[-1])
            o_c = _moba_sample(seq(mq), seq(mk), seq(mv), sample["cache_kt"], sample["cache_vt"],
                               sample["page_table"], layer * sample["n_pool"])
            o_c = o_c.reshape(b * l, -1)
        mix = (o_a.reshape(b * l, -1), o_b.reshape(b * l, -1), o_c, wts["w_out"])
        x = _ffn(x, wts["n_ffn2"], *ffn_w[1], layer, mix=mix,
                 final_w=norm_final if layer == depth - 1 else None)
        states.append(dict(qkv=qkv.reshape(b, l, -1), gated=gated.reshape(b, l, -1),
                           k=mk.reshape(b, l, -1), v=mv))
    return x.reshape(b, l, d), states, gdn_states


def kernel(x_prompt, x_sample, state_gdn, state_gdn_conv, state_sconv, cache_k, cache_v,
           page_table, norm_ffn1, ffn1_w_gate, ffn1_w_up, ffn1_w_down, norm_mix, w_in,
           gdn_conv_w, gdn_a_log, gdn_dt_bias, gdn_norm_w, sc_conv_w, w_out,
           norm_ffn2, ffn2_w_gate, ffn2_w_up, ffn2_w_down, norm_final):
    depth = w_in.shape[0]
    gw, sw, mw = GDN_WIDTH, SC_WIDTH, MOBA_WIDTH
    nh = GDN_HEADS
    mh, md = MOBA_HEADS, MOBA_HEAD_DIM

    weights = []
    for la in range(depth):
        w = w_in[la]
        ba_off = 4 * gw
        sc_off = ba_off + 2 * nh
        mo_off = sc_off + 3 * sw
        w_rows = jnp.concatenate(
            [w[:, :ba_off], w[:, sc_off:mo_off],
             jnp.pad(w[:, ba_off:sc_off], ((0, 0), (0, LANES - 2 * nh))),
             w[:, mo_off + mw:mo_off + 2 * mw]], axis=1).astype(BF16)
        w_qv = jnp.concatenate([w[:, mo_off:mo_off + mw], w[:, mo_off + 2 * mw:mo_off + 3 * mw]],
                               axis=1).astype(BF16)
        gparams = jnp.zeros((SUBLANES, LANES), F32)
        gparams = gparams.at[0, nh:2 * nh].set(gdn_a_log[la].astype(F32))
        gparams = gparams.at[1, nh:2 * nh].set(gdn_dt_bias[la].astype(F32))
        weights.append(dict(
            n_ffn1=norm_ffn1[la],
            n_mix=norm_mix[la], w_rows=w_rows, w_qv=w_qv, w_qv_t=w_qv.T,
            gdn_conv_w=gdn_conv_w[la], gparams=gparams, gdn_norm_w=gdn_norm_w[la],
            sc_conv_w=sc_conv_w[la], w_out=_to_bf16(w_out, la),
            n_ffn2=norm_ffn2[la]))
    ffn_w = ((ffn1_w_gate, ffn1_w_up, ffn1_w_down), (ffn2_w_gate, ffn2_w_up, ffn2_w_down))

    pb, pl_, _ = x_prompt.shape
    y_p, st_p, p_gdn = _trunk(x_prompt, weights, ffn_w, norm_final)
    hist = GDN_CONV - 1
    p_conv = jnp.stack([s["qkv"][:, -hist:] for s in st_p])
    p_sconv = jnp.stack([s["gated"][:, -(SC_CONV - 1):] for s in st_p])
    p_k = jnp.stack([s["k"] for s in st_p]).reshape(depth, pb, pl_, mh, md)
    p_v = jnp.stack([s["v"] for s in st_p]).reshape(depth, pb, mh, md, pl_)
    p_v = p_v.transpose(0, 1, 4, 2, 3)

    db, dl, d = x_sample.shape
    assert dl == S_HI - S_LO and (db * S_ROWS) % GROUP_ROWS == 0
    n_pool = cache_k.shape[1]
    tile_pad = lambda a, lo: jnp.pad(a, ((0, 0),) * (a.ndim - 2)
                                     + ((lo, S_ROWS - lo - a.shape[-2]), (0, 0)))
    seqs_per_group = GROUP_ROWS // S_ROWS
    paged_t = lambda c: c.transpose(0, 1, 3, 4, 2).reshape(depth * n_pool, mw, PAGE_SIZE)
    sample = dict(
        gdn=state_gdn.reshape(depth, db // seqs_per_group, seqs_per_group * nh,
                              GDN_HEAD_DIM, GDN_HEAD_DIM),
        gdn_conv=tile_pad(state_gdn_conv, 0),
        sconv=tile_pad(state_sconv, S_LO - (SC_CONV - 1)),
        cache_kt=paged_t(cache_k), cache_vt=paged_t(cache_v), n_pool=n_pool,
        page_table=page_table)
    y_s, st_s, s_gdn = _trunk(tile_pad(x_sample, S_LO), weights, ffn_w, norm_final,
                              sample=sample)
    y_s = y_s[:, S_LO:S_HI]
    s_gdn = s_gdn.reshape(state_gdn.shape)
    s_conv = jnp.stack([s["qkv"][:, S_HI - hist:S_HI] for s in st_s])
    s_sconv = jnp.stack([s["gated"][:, S_HI - (SC_CONV - 1):S_HI] for s in st_s])
    s_k = jnp.stack([s["k"][:, S_LO:S_HI] for s in st_s]).reshape(depth, db, dl, mh, md)
    s_v = jnp.stack([s["v"].reshape(db, S_ROWS, mw)[:, S_LO:S_HI] for s in st_s]).reshape(
        depth, db, dl, mh, md)
    return (y_p, y_s, p_gdn, p_conv, p_sconv, p_k, p_v, s_gdn, s_conv, s_sconv, s_k, s_v)
```

```python
import functools

import jax
import jax.numpy as jnp
from jax import lax
from jax.experimental import pallas as pl
from jax.experimental.pallas import tpu as pltpu

NORM_EPS = 1e-6
NEG_INF = -1e30

GDN_HEADS = 4
GDN_HEAD_DIM = 128
GDN_WIDTH = GDN_HEADS * GDN_HEAD_DIM
GDN_CONV = 4
GDN_CHUNK = 64
SC_WIDTH = 256
SC_CONV = 3
MOBA_HEADS = 4
MOBA_HEAD_DIM = 64
MOBA_WIDTH = MOBA_HEADS * MOBA_HEAD_DIM
MOBA_BLOCK = 256
MOBA_TOPK = 3
PAGE_SIZE = 128

LANES = 128
SUBLANES = 8
VMEM_LIMIT_BYTES = 56 * 1024 * 1024

S_ROWS = SUBLANES
S_LO = GDN_CONV - 1
S_HI = S_LO + 4

GROUP_ROWS = GDN_CHUNK
STACK_ROWS = GDN_HEADS * GROUP_ROWS

F32 = jnp.float32
BF16 = jnp.bfloat16
HI = lax.Precision.HIGHEST


def _dot(a, b, precision=None):
    return jnp.dot(a, b, preferred_element_type=F32, precision=precision)


def _dot_nt(a, b, precision=None):
    return lax.dot_general(a, b, (((1,), (1,)), ((), ())),
                           preferred_element_type=F32, precision=precision)


def _dot_tn(a, b, precision=None):
    return lax.dot_general(a, b, (((0,), (0,)), ((), ())),
                           preferred_element_type=F32, precision=precision)


def _rms(x, w):
    return x * lax.rsqrt(jnp.mean(x * x, axis=-1, keepdims=True) + NORM_EPS) * w


def _silu(x):
    return x * jax.nn.sigmoid(x)


def _softplus(x):
    return jnp.maximum(x, 0.0) + jnp.log1p(jnp.exp(-jnp.abs(x)))


def _resident(shape):
    nd = len(shape)
    return pl.BlockSpec(shape, lambda *_: (0,) * nd, pipeline_mode=pl.Buffered(1))


def _params(*semantics):
    return pltpu.CompilerParams(dimension_semantics=semantics,
                                vmem_limit_bytes=VMEM_LIMIT_BYTES)


def _cast_body(w_ref, o_ref):
    o_ref[...] = w_ref[0].astype(o_ref.dtype)


def _to_bf16(w, layer, rows=256):
    _, r, c = w.shape
    rows = min(rows, r)
    assert r % rows == 0
    return pl.pallas_call(
        _cast_body,
        name="to_bf16",
        grid=(r // rows,),
        in_specs=[pl.BlockSpec((1, rows, c), lambda i: (layer, i, 0))],
        out_specs=pl.BlockSpec((rows, c), lambda i: (i, 0)),
        out_shape=jax.ShapeDtypeStruct((r, c), BF16),
        compiler_params=_params("arbitrary"),
    )(w)


def _ffn_body(*refs, has_mix, oc_transposed, has_final, f_chunk):
    refs = list(refs)
    x_ref = refs.pop(0)
    if has_mix:
        oa_ref, ob_ref, oc_ref, wo_ref = refs[:4]
        refs = refs[4:]
    nw_ref, wg_ref, wu_ref, wd_ref = refs[:4]
    refs = refs[4:]
    if has_final:
        nf_ref = refs.pop(0)
    out_ref, xn_ref, acc_ref = refs

    x = x_ref[...]
    if has_mix:
        ga = oa_ref.shape[1]
        gb = ga + ob_ref.shape[1]
        x = x + _dot(oa_ref[...], wo_ref[0:ga, :]) + _dot(ob_ref[...], wo_ref[ga:gb, :])
        if oc_transposed:
            x = x + _dot_tn(oc_ref[0], wo_ref[gb:, :])
        else:
            x = x + _dot(oc_ref[...], wo_ref[gb:, :])
    xn_ref[...] = _rms(x, nw_ref[...]).astype(BF16)
    d_ff = wg_ref.shape[2]
    for c in range(d_ff // f_chunk):
        lo = c * f_chunk
        xn = xn_ref[...]
        g = _dot(xn, wg_ref[0, :, lo:lo + f_chunk].astype(BF16))
        u = _dot(xn, wu_ref[0, :, lo:lo + f_chunk].astype(BF16))
        h = (_silu(g) * u).astype(BF16)
        part = _dot(h, wd_ref[0, lo:lo + f_chunk, :].astype(BF16))
        if c == 0:
            acc_ref[...] = part
        else:
            acc_ref[...] += part
    y = x + 0.5 * acc_ref[...]
    if has_final:
        y = _rms(y, nf_ref[...])
    out_ref[...] = y


def _ffn(x, nw, wg, wu, wd, layer, mix=None, final_w=None, tm=512, f_chunk=256):
    m, d = x.shape
    d_ff = wg.shape[2]
    of_layer = lambda w: pl.BlockSpec((1,) + w.shape[1:], lambda *_: (layer, 0, 0),
                                      pipeline_mode=pl.Buffered(1))
    tm = min(tm, m)
    assert m % tm == 0 and d_ff % f_chunk == 0
    row = lambda i: (i, 0)
    in_specs = [pl.BlockSpec((tm, d), row)]
    args = [x]
    oc_transposed = False
    if mix is not None:
        oa, ob, oc, wo = mix
        oc_transposed = oc.ndim == 3
        if oc_transposed:
            tiles = oc.shape[2] // tm
            assert oc.shape[2] % tm == 0
            oc_spec = pl.BlockSpec((1, oc.shape[1], tm), lambda i: (i // tiles, 0, i % tiles))
        else:
            oc_spec = pl.BlockSpec((tm, oc.shape[1]), row)
        in_specs += [pl.BlockSpec((tm, oa.shape[1]), row), pl.BlockSpec((tm, ob.shape[1]), row),
                     oc_spec, _resident(wo.shape)]
        args += [oa, ob, oc, wo]
    in_specs += [_resident((1, d)), of_layer(wg), of_layer(wu), of_layer(wd)]
    args += [nw.reshape(1, d), wg, wu, wd]
    if final_w is not None:
        in_specs.append(_resident((1, d)))
        args.append(final_w.reshape(1, d))
    body = functools.partial(_ffn_body, has_mix=mix is not None, oc_transposed=oc_transposed,
                             has_final=final_w is not None, f_chunk=f_chunk)
    return pl.pallas_call(
        body,
        name="ffn_mix" if mix is not None else "ffn",
        grid=(m // tm,),
        in_specs=in_specs,
        out_specs=pl.BlockSpec((tm, d), row),
        out_shape=jax.ShapeDtypeStruct((m, d), F32),
        scratch_shapes=[pltpu.VMEM((tm, d), BF16), pltpu.VMEM((tm, d), F32)],
        compiler_params=_params("arbitrary"),
    )(*args)


_ROW_WIDTHS = (3 * GDN_WIDTH, GDN_WIDTH, 3 * SC_WIDTH, LANES, MOBA_WIDTH)


def _causal_conv(buf, x, w_ref):
    pad = SUBLANES
    n = x.shape[0]
    taps = w_ref.shape[0]
    buf[pad:pad + n, :] = x
    y = None
    for t in range(taps - 1):
        lo = pad - (taps - 1) + t
        term = buf[lo:lo + n, :] * w_ref[t:t + 1, :]
        y = term if y is None else y + term
    y = y + x * w_ref[taps - 1:taps, :]
    tail = buf[n:n + pad, :]
    buf[0:pad, :] = tail
    return y


def _l2_normalize(x):
    return x * lax.rsqrt(jnp.sum(x * x, axis=-1, keepdims=True) + NORM_EPS)


def _gates(ba, gp_ref):
    return jax.nn.sigmoid(ba), -jnp.exp(gp_ref[0:1, :]) * _softplus(ba + gp_ref[1:2, :])


def _inproj_body(x_ref, nw_ref, w_ref, wqv_ref, *out_refs, transposed_qv):
    xn = _rms(x_ref[...], nw_ref[...]).astype(BF16)
    lo = 0
    for ref in out_refs[:len(_ROW_WIDTHS)]:
        width = ref.shape[1]
        ref[...] = _dot(xn, w_ref[:, lo:lo + width])
        lo += width
    q_ref, v_ref = out_refs[len(_ROW_WIDTHS):]
    mw = MOBA_WIDTH
    if transposed_qv:
        q_ref[0] = _dot_nt(wqv_ref[0:mw, :], xn)
        v_ref[0] = _dot_nt(wqv_ref[mw:2 * mw, :], xn)
    else:
        q_ref[...] = _dot(xn, wqv_ref[:, 0:mw])
        v_ref[...] = _dot(xn, wqv_ref[:, mw:2 * mw])


def _inproj(x, nw, w_rows, w_qv, *, seq_len=None, tm=512):
    m, d = x.shape
    tm = min(tm, m)
    assert m % tm == 0 and w_rows.shape[1] == sum(_ROW_WIDTHS)
    row = lambda i: (i, 0)
    out_specs = [pl.BlockSpec((tm, wd), row) for wd in _ROW_WIDTHS]
    out_shape = [jax.ShapeDtypeStruct((m, wd), F32) for wd in _ROW_WIDTHS]
    transposed_qv = seq_len is not None
    if transposed_qv:
        assert seq_len % tm == 0
        tiles = seq_len // tm
        spec = pl.BlockSpec((1, MOBA_WIDTH, tm), lambda i: (i // tiles, 0, i % tiles))
        shape = jax.ShapeDtypeStruct((m // seq_len, MOBA_WIDTH, seq_len), F32)
    else:
        spec = pl.BlockSpec((tm, MOBA_WIDTH), row)
        shape = jax.ShapeDtypeStruct((m, MOBA_WIDTH), F32)
    out_specs += [spec, spec]
    out_shape += [shape, shape]
    return pl.pallas_call(
        functools.partial(_inproj_body, transposed_qv=transposed_qv),
        name="inproj",
        grid=(m // tm,),
        in_specs=[pl.BlockSpec((tm, d), row), _resident((1, d)), _resident(w_rows.shape),
                  _resident(w_qv.shape)],
        out_specs=out_specs,
        out_shape=out_shape,
        compiler_params=_params("arbitrary"),
    )(x, nw.reshape(1, d), w_rows, w_qv)


def _mixer_body(*refs, c, ng, has_state, row_lo, row_hi):
    refs = list(refs)
    qkv_ref, z_ref, ba_ref, sc_ref = refs[:4]
    refs = refs[4:]
    if has_state:
        qh_ref, sh_ref, s0_ref = refs[:3]
        refs = refs[3:]
    cw_ref, gp_ref, nw_ref, scw_ref = refs[:4]
    refs = refs[5:]
    oa_ref, ob_ref, st_ref, gt_ref = refs[:4]
    xbuf, gbuf, s_ref = refs[4:]
    n = pl.program_id(1)
    hd = GDN_HEAD_DIM
    gr = GROUP_ROWS
    sr = STACK_ROWS
    pad = SUBLANES
    seqs = gr // c
    shift = c.bit_length() - 1
    assert 1 << shift == c

    @pl.when(n == 0)
    def _():
        xbuf[:, 0:pad, :] = jnp.zeros((ng, pad, xbuf.shape[2]), F32)
        gbuf[:, 0:pad, :] = jnp.zeros((ng, pad, gbuf.shape[2]), F32)
        if has_state:
            s_ref[...] = s0_ref[0]
        else:
            s_ref[...] = jnp.zeros(s_ref.shape, F32)

    rows = lax.broadcasted_iota(jnp.int32, (gr, 1), 0) & (c - 1)
    is_hist = rows < row_lo
    live = jnp.where((rows >= row_lo) & (rows < row_hi), 1.0, 0.0)
    ri = lax.broadcasted_iota(jnp.int32, (gr, gr), 0)
    ci = lax.broadcasted_iota(jnp.int32, (gr, gr), 1)
    same = (ri >> shift) == (ci >> shift)
    cum_op = jnp.concatenate([jnp.where(same & (ri >= ci), 1.0, 0.0),
                              jnp.where(same, 1.0, 0.0)], axis=0)
    nh = GDN_HEADS
    tiles = sr // LANES
    band_tile = [h * gr // LANES for h in range(nh)]

    def bands(mat):
        return [mat[h * gr:(h + 1) * gr, band_tile[h] * LANES:(band_tile[h] + 1) * LANES]
                for h in range(nh)]

    def unband(pieces):
        zero = jnp.zeros((gr, LANES), BF16)
        return jnp.concatenate(
            [jnp.concatenate([p.astype(BF16) if t == band_tile[h] else zero
                              for t in range(tiles)], axis=1)
             for h, p in enumerate(pieces)], axis=0)

    incl, strict, eye = [], [], []
    for h in range(nh):
        rs = lax.broadcasted_iota(jnp.int32, (gr, LANES), 0) + h * gr
        cs = lax.broadcasted_iota(jnp.int32, (gr, LANES), 1) + band_tile[h] * LANES
        same_s = (rs >> shift) == (cs >> shift)
        incl.append(same_s & (rs >= cs))
        strict.append(same_s & (rs > cs))
        eye.append(jnp.where(rs == cs, 1.0, 0.0))

    def stack_heads(a, lo):
        return jnp.concatenate([a[:, lo + h * hd:lo + (h + 1) * hd] for h in range(GDN_HEADS)],
                               axis=0)

    def stack_cols(a, lo):
        return jnp.concatenate([a[:, lo + h:lo + h + 1] for h in range(GDN_HEADS)], axis=0)

    groups = range(ng)

    def front(g):
        x = qkv_ref[g]
        if has_state:
            x = jnp.where(is_hist, qh_ref[g], x)
        act = _silu(_causal_conv(xbuf.at[g], x, cw_ref))
        q = _l2_normalize(stack_heads(act, 0)) * (hd ** -0.5)
        k = _l2_normalize(stack_heads(act, GDN_WIDTH))
        v = stack_heads(act, 2 * GDN_WIDTH)

        sc = sc_ref[g]
        scw = SC_WIDTH
        gated = sc[:, scw:2 * scw] * sc[:, 0:scw]
        if has_state:
            gated = jnp.where(is_hist, sh_ref[g], gated)
        yb = _causal_conv(gbuf.at[g], gated, scw_ref)
        ob_ref[g] = (sc[:, 2 * scw:3 * scw] * yb).astype(ob_ref.dtype)
        gt_ref[g] = gated

        beta_all, g_all = _gates(ba_ref[g], gp_ref)
        beta_all, g_all = beta_all * live, g_all * live
        cum = _dot(cum_op, g_all, HI)
        gc_all, glast_all = cum[:gr], cum[gr:]
        beta = stack_cols(beta_all, 0)
        gc = stack_cols(gc_all, nh)
        eg = jnp.exp(gc)
        etail = jnp.exp(stack_cols(glast_all, nh) - gc)
        gc_row = jnp.broadcast_to(gc, (sr, LANES)).T[0:1, :]
        decay = []
        for h in range(nh):
            diff = (gc[h * gr:(h + 1) * gr]
                    - gc_row[:, band_tile[h] * LANES:(band_tile[h] + 1) * LANES])
            decay.append(jnp.where(incl[h], jnp.exp(jnp.where(incl[h], diff, 0.0)), 0.0))

        kb = k.astype(BF16)
        kk = bands(_dot_nt(kb, kb))
        m_strict = [jnp.where(strict[h], beta[h * gr:(h + 1) * gr] * kk[h] * decay[h], 0.0)
                    for h in range(nh)]
        return dict(q=q, k=k, v=v, kb=kb, beta=beta, eg=eg, etail=etail, decay=decay,
                    cd_all=jnp.exp(glast_all), m_strict=m_strict)

    fr = [front(g) for g in groups]

    invs = [[e - m for e, m in zip(eye, f["m_strict"])] for f in fr]
    pbs = [unband(f["m_strict"]) for f in fr]
    for _ in range(shift - 1):
        pbs = [unband(bands(_dot(pb, pb))) for pb in pbs]
        invs = [[i + d for i, d in zip(inv, bands(_dot(unband(inv), pb)))]
                for inv, pb in zip(invs, pbs)]

    def solve(f, inv):
        rhs = jnp.concatenate([f["k"] * (f["beta"] * f["eg"]), f["v"] * f["beta"]],
                              axis=-1).astype(BF16)
        sol = _dot(unband(inv), rhs)
        qk = bands(_dot_nt(f["q"].astype(BF16), f["kb"]))
        attn = unband([a * d for a, d in zip(qk, f["decay"])])
        return dict(w_k=sol[:, :hd], u_v=sol[:, hd:], attn=attn, qd=f["q"] * f["eg"],
                    kt=(f["k"] * f["etail"]).astype(BF16))

    sv = [solve(f, inv) for f, inv in zip(fr, invs)]

    pair_ids = [(h, s) for h in range(GDN_HEADS) for s in range(seqs)]

    def read_state(g, t):
        parts = []
        for h, s in pair_ids:
            r0 = h * gr + s * c
            s_old = s_ref[g, s * nh + h]
            lhs = jnp.concatenate([t["w_k"][r0:r0 + c], t["qd"][r0:r0 + c]], axis=0)
            res = _dot(lhs.astype(BF16), s_old.astype(BF16))
            parts.append((s_old, t["u_v"][r0:r0 + c] - res[:c], res[c:]))
        return parts

    rd = [read_state(g, t) for g, t in zip(groups, sv)]
    us = [jnp.concatenate([p[1] for p in parts], axis=0) for parts in rd]
    outs = [jnp.concatenate([p[2] for p in parts], axis=0) + _dot(t["attn"], u.astype(BF16))
            for parts, t, u in zip(rd, sv, us)]
    for g in groups:
        for (h, s), (s_old, u_p, _) in zip(pair_ids, rd[g]):
            r0 = h * gr + s * c
            cd = fr[g]["cd_all"][s * c:s * c + 1, nh + h:nh + h + 1]
            s_ref[g, s * nh + h] = s_old * cd + _dot_tn(sv[g]["kt"][r0:r0 + c],
                                                        u_p.astype(BF16))
    for g in groups:
        zs = stack_heads(z_ref[g], 0)
        o_n = _rms(outs[g], nw_ref[...]) * _silu(zs)
        oa_ref[g] = jnp.concatenate([o_n[h * gr:(h + 1) * gr] for h in range(GDN_HEADS)],
                                    axis=1).astype(oa_ref.dtype)

    @pl.when(n == pl.num_programs(1) - 1)
    def _():
        st_ref[0] = s_ref[...]


def _mixer(qkv, z, ba, sc, conv_w, gparams, norm_w, sc_conv_w, *, c, ng, layer, depth,
           prev_states=None, state=None, row_lo=0, row_hi=None):
    g_total, l, _ = qkv.shape
    gr = GROUP_ROWS
    assert l % gr == 0 and gr % c == 0 and g_total % ng == 0
    nc = l // gr
    pairs = GDN_HEADS * (gr // c)
    row_hi = c if row_hi is None else row_hi
    blk = lambda w: pl.BlockSpec((ng, gr, w), lambda i, n: (i, n, 0))
    in_specs = [blk(qkv.shape[2]), blk(z.shape[2]), blk(ba.shape[2]), blk(sc.shape[2])]
    args = [qkv, z, ba, sc]
    st_shape = (g_total, pairs, GDN_HEAD_DIM, GDN_HEAD_DIM)
    st_spec = pl.BlockSpec((1, ng) + st_shape[1:], lambda i, n: (layer, i, 0, 0, 0))
    if state is not None:
        assert nc == 1
        qh, sh, s0 = state
        assert s0.shape[1:] == st_shape
        in_specs += [pl.BlockSpec((ng, gr, qh.shape[2]), lambda i, n: (i, 0, 0)),
                     pl.BlockSpec((ng, gr, sh.shape[2]), lambda i, n: (i, 0, 0)), st_spec]
        args += [qh, sh, s0]
    const = lambda a: pl.BlockSpec(a.shape, lambda i, n: (0,) * a.ndim)
    norm_w = norm_w.reshape(1, -1)
    in_specs += [const(conv_w), const(gparams), const(norm_w), const(sc_conv_w)]
    args += [conv_w, gparams, norm_w, sc_conv_w]
    if prev_states is None:
        prev_states = jnp.zeros((depth,) + st_shape, F32)
    assert prev_states.shape == (depth,) + st_shape
    aliases = {len(args): 2}
    in_specs.append(pl.BlockSpec(memory_space=pl.ANY))
    args.append(prev_states)
    out_shape = [jax.ShapeDtypeStruct((g_total, l, GDN_WIDTH), BF16),
                 jax.ShapeDtypeStruct((g_total, l, SC_WIDTH), BF16),
                 jax.ShapeDtypeStruct((depth,) + st_shape, F32),
                 jax.ShapeDtypeStruct((g_total, l, SC_WIDTH), F32)]
    out_specs = [blk(GDN_WIDTH), blk(SC_WIDTH), st_spec, blk(SC_WIDTH)]
    body = functools.partial(_mixer_body, c=c, ng=ng, has_state=state is not None,
                             row_lo=row_lo, row_hi=row_hi)
    return pl.pallas_call(
        body,
        name="seq_mixer",
        grid=(g_total // ng, nc),
        in_specs=in_specs,
        out_specs=out_specs,
        out_shape=out_shape,
        input_output_aliases=aliases,
        scratch_shapes=[pltpu.VMEM((ng, gr + SUBLANES, qkv.shape[2]), F32),
                        pltpu.VMEM((ng, gr + SUBLANES, SC_WIDTH), F32),
                        pltpu.VMEM((ng,) + st_shape[1:], F32)],
        compiler_params=_params("arbitrary", "arbitrary"),
    )(*args)


def _select_blocks(gate, n_valid, axis=0):
    nb = gate.shape[axis]
    sub = lax.broadcasted_iota(jnp.int32, gate.shape, axis)
    gate = jnp.where(sub < n_valid, gate, NEG_INF)
    sel = jnp.zeros(gate.shape, F32)
    for j in range(nb):
        gj = gate[j:j + 1, :] if axis == 0 else gate[:, j:j + 1]
        beats = (gate > gj) | ((gate == gj) & (sub < j))
        cnt = jnp.sum(jnp.where(beats, 1.0, 0.0), axis=axis, keepdims=True)
        hit = (cnt < MOBA_TOPK) & (j < n_valid)
        sel = jnp.where((sub == j) & hit, 1.0, sel)
    return sel


_PV_ROWS = MOBA_HEAD_DIM + 16


def _moba_prompt_body(q_ref, k_ref, v_ref, o_ref, kmean_ref, kb_ref, vb_ref, sel_ref, *, nbat):
    i = pl.program_id(1)
    blk = MOBA_BLOCK
    hd = MOBA_HEAD_DIM
    w = MOBA_WIDTH
    l_seq = v_ref.shape[2]
    nb = l_seq // blk
    heads = range(MOBA_HEADS)
    head_rows = [slice(h * hd, (h + 1) * hd) for h in heads]
    chains = [(b, h) for b in range(nbat) for h in heads]

    @pl.when(i == 0)
    def _():
        ones = jnp.ones((_PV_ROWS - hd, blk), BF16)
        for b in range(nbat):
            for j in range(nb):
                kj = k_ref[b * l_seq + j * blk:b * l_seq + (j + 1) * blk, :]
                kmean_ref[b, j:j + 1, :] = jnp.mean(kj, axis=0, keepdims=True)
                kb_ref[b, j] = kj.astype(BF16)
                for h in heads:
                    vb_ref[b, j, h, 0:hd, :] = v_ref[b, head_rows[h],
                                                     j * blk:(j + 1) * blk].astype(BF16)
                    vb_ref[b, j, h, hd:_PV_ROWS, :] = ones

    feat = lax.broadcasted_iota(jnp.int32, (w, blk), 0)
    key_i = lax.broadcasted_iota(jnp.int32, (blk, blk), 0)
    qry_i = lax.broadcasted_iota(jnp.int32, (blk, blk), 1)
    causal = key_i <= qry_i
    qmb = []
    for b in range(nbat):
        qt = q_ref[b] * (hd ** -0.5)
        qms = [jnp.where((feat >= h * hd) & (feat < (h + 1) * hd), qt, 0.0) for h in heads]
        qmb.append(jnp.concatenate(qms, axis=1).astype(BF16))
        for h in heads:
            gate = _dot(kmean_ref[b], qms[h], HI)
            sel_ref[b, h] = _select_blocks(gate, i)

    def scores(j):
        s_all = [_dot(kb_ref[b, j], qmb[b]) for b in range(nbat)]
        return [s_all[b][:, h * blk:(h + 1) * blk] for b, h in chains]

    def attend(j, s_list, m_list):
        ps = [jnp.exp(s - m).astype(BF16) for s, m in zip(s_list, m_list)]
        return [_dot(vb_ref[b, j, h], p) for (b, h), p in zip(chains, ps)]

    s0 = [jnp.where(causal, s, NEG_INF) for s in scores(i)]
    m0 = [jnp.max(s, axis=0, keepdims=True) for s in s0]
    pv0 = attend(i, s0, m0)
    init = []
    for m, pv in zip(m0, pv0):
        init += [m, pv[hd:hd + 1], pv[:hd]]

    def past_block(j, carry):
        s_list = [jnp.where(sel_ref[b, h, pl.ds(j, 1), :] > 0.5, s, NEG_INF)
                  for (b, h), s in zip(chains, scores(j))]
        m_old = carry[0::3]
        m_new = [jnp.maximum(m, jnp.max(s, axis=0, keepdims=True))
                 for m, s in zip(m_old, s_list)]
        pv = attend(j, s_list, m_new)
        new = []
        for c in range(len(chains)):
            alpha = jnp.exp(m_old[c] - m_new[c])
            new += [m_new[c], alpha * carry[3 * c + 1] + pv[c][hd:hd + 1],
                    alpha * carry[3 * c + 2] + pv[c][:hd]]
        return tuple(new)

    final = lax.fori_loop(0, i, past_block, tuple(init))
    for c, (b, h) in enumerate(chains):
        o_ref[b, head_rows[h], :] = (final[3 * c + 2] / final[3 * c + 1]).astype(o_ref.dtype)


def _moba_prompt(qt, k, vt, nbat=2):
    b, w, l = qt.shape
    blk = MOBA_BLOCK
    nbat = nbat if b % nbat == 0 else 1
    assert l % blk == 0
    nb = l // blk
    return pl.pallas_call(
        functools.partial(_moba_prompt_body, nbat=nbat),
        name="moba_prompt",
        grid=(b // nbat, nb),
        in_specs=[pl.BlockSpec((nbat, w, blk), lambda i, t: (i, 0, t)),
                  pl.BlockSpec((nbat * l, w), lambda i, t: (i, 0)),
                  pl.BlockSpec((nbat, w, l), lambda i, t: (i, 0, 0))],
        out_specs=pl.BlockSpec((nbat, w, blk), lambda i, t: (i, 0, t)),
        out_shape=jax.ShapeDtypeStruct((b, w, l), BF16),
        scratch_shapes=[pltpu.VMEM((nbat, nb, w), F32), pltpu.VMEM((nbat, nb, blk, w), BF16),
                        pltpu.VMEM((nbat, nb, MOBA_HEADS, _PV_ROWS, blk), BF16),
                        pltpu.VMEM((nbat, MOBA_HEADS, nb, blk), F32)],
        compiler_params=_params("arbitrary", "arbitrary"),
    )(qt, k, vt)


PAGE_SLOTS = 3
SEQS_PER_STEP = 2


def _moba_sample_body(pt_ref, q_ref, kn_ref, vn_ref, ck_ref, cv_ref, o_ref, kbuf, vbuf, sem,
                      *, n_pages, base, n_steps, ns):
    i = pl.program_id(0)
    hd = MOBA_HEAD_DIM
    r = S_ROWS
    w = MOBA_WIDTH
    ppb = MOBA_BLOCK // PAGE_SIZE
    nb = n_pages // ppb
    ahead = PAGE_SLOTS - 1
    seqs = range(ns)
    blocks = range(nb)

    def page_copies(step, slot):
        cps = []
        for s in seqs:
            for p in range(n_pages):
                row = base + pt_ref[step * ns + s, p]
                cps.append(pltpu.make_async_copy(ck_ref.at[row], kbuf.at[slot, s, p],
                                                 sem.at[0, slot]))
                cps.append(pltpu.make_async_copy(cv_ref.at[row], vbuf.at[slot, s, p],
                                                 sem.at[1, slot]))
        return cps

    def start_all(cps):
        for n, cp in enumerate(cps):
            cp.start(priority=n % 2)

    @pl.when(i == 0)
    def _():
        for d in range(ahead):
            start_all(page_copies(d, d))

    slot = lax.rem(i, PAGE_SLOTS)
    for cp in page_copies(i, slot):
        cp.wait()

    lane = lax.broadcasted_iota(jnp.int32, (r, w), 1)
    head_masks = [jnp.where((lane >= h * hd) & (lane < (h + 1) * hd), 1.0, 0.0)
                  for h in range(MOBA_HEADS)]
    col = lax.broadcasted_iota(jnp.int32, (w, LANES), 1)
    q_bd = [jnp.concatenate([q_ref[s] * (hd ** -0.5) * hm for hm in head_masks], axis=0)
            for s in seqs]
    q_bb = [x.astype(BF16) for x in q_bd]

    kp = [[kbuf[slot, s, p] for p in range(n_pages)] for s in seqs]
    s_pages = [[_dot(q_bb[s], x.astype(BF16)) for x in kp[s]] for s in seqs]
    kmean_t = []
    for s in seqs:
        km = jnp.zeros((w, LANES), F32)
        for j in blocks:
            ksum = kp[s][j * ppb]
            for pp in range(1, ppb):
                ksum = ksum + kp[s][j * ppb + pp]
            km = jnp.where(col == j, jnp.sum(ksum, axis=1, keepdims=True) * (1.0 / MOBA_BLOCK),
                           km)
        kmean_t.append(km)
    s_blk = [[jnp.concatenate(s_pages[s][j * ppb:(j + 1) * ppb], axis=-1) for j in blocks]
             for s in seqs]
    m_cols = [[jnp.max(x, axis=1, keepdims=True) for x in s_blk[s]] for s in seqs]
    e_blk = [[jnp.exp(x - m) for x, m in zip(s_blk[s], m_cols[s])] for s in seqs]
    l_cols = [[jnp.sum(e, axis=1, keepdims=True) for e in e_blk[s]] for s in seqs]
    accs = []
    for s in seqs:
        row_accs = []
        for j in blocks:
            eb = e_blk[s][j].astype(BF16)
            acc = None
            for pp in range(ppb):
                vp = vbuf[slot, s, j * ppb + pp].astype(BF16)
                part = _dot_nt(eb[:, pp * PAGE_SIZE:(pp + 1) * PAGE_SIZE], vp)
                acc = part if acc is None else acc + part
            row_accs.append(acc)
        accs.append(row_accs)

    nxt = jnp.minimum(i + ahead, n_steps - 1)
    start_all(page_copies(nxt, lax.rem(i + ahead, PAGE_SLOTS)))

    gates = [_dot(q_bd[s], kmean_t[s], HI)[:, :nb] for s in seqs]
    sels = [_select_blocks(g, nb, axis=1) > 0.5 for g in gates]
    qt = lax.broadcasted_iota(jnp.int32, (MOBA_HEADS * r, r), 0) & (r - 1)
    kt = lax.broadcasted_iota(jnp.int32, (MOBA_HEADS * r, r), 1)
    own = (kt >= S_LO) & (kt <= qt)
    for s in seqs:
        sel = sels[s]
        m_all = jnp.concatenate(m_cols[s], axis=1)
        l_all = jnp.concatenate(l_cols[s], axis=1)
        s_own = _dot_nt(q_bb[s], kn_ref[s].astype(BF16))
        m_tot = jnp.maximum(jnp.max(jnp.where(sel, m_all, NEG_INF), axis=1, keepdims=True),
                            jnp.max(jnp.where(own, s_own, NEG_INF), axis=1, keepdims=True))
        wj = jnp.where(sel, jnp.exp(jnp.where(sel, m_all - m_tot, 0.0)), 0.0)
        p_own = jnp.where(own, jnp.exp(jnp.where(own, s_own - m_tot, 0.0)), 0.0)
        l_tot = (jnp.sum(wj * l_all, axis=1, keepdims=True)
                 + jnp.sum(p_own, axis=1, keepdims=True))
        acc = _dot(p_own.astype(BF16), vn_ref[s].astype(BF16))
        for j in blocks:
            acc = acc + wj[:, j:j + 1] * accs[s][j]
        o_bd = acc / l_tot
        o = None
        for h in range(MOBA_HEADS):
            part = o_bd[h * r:(h + 1) * r, :] * head_masks[h]
            o = part if o is None else o + part
        o_ref[s] = o.astype(o_ref.dtype)

    @pl.when(i == n_steps - 1)
    def _():
        for d in range(1, PAGE_SLOTS):
            for cp in page_copies(n_steps - 1, lax.rem(i + d, PAGE_SLOTS)):
                cp.wait()


def _moba_sample(q, k_new, v_new, cache_kt, cache_vt, page_table, base):
    b, r, w = q.shape
    page = cache_kt.shape[2]
    n_pages = page_table.shape[1]
    ns = SEQS_PER_STEP if b % SEQS_PER_STEP == 0 else 1
    n_steps = b // ns
    assert page == PAGE_SIZE and r == S_ROWS and cache_kt.shape[1] == w
    assert n_steps >= PAGE_SLOTS
    tile = pl.BlockSpec((ns, r, w), lambda i, pt: (i, 0, 0))
    hbm = pl.BlockSpec(memory_space=pl.ANY)
    grid_spec = pltpu.PrefetchScalarGridSpec(
        num_scalar_prefetch=1,
        grid=(n_steps,),
        in_specs=[tile, tile, tile, hbm, hbm],
        out_specs=tile,
        scratch_shapes=[pltpu.VMEM((PAGE_SLOTS, ns, n_pages, w, page), F32),
                        pltpu.VMEM((PAGE_SLOTS, ns, n_pages, w, page), F32),
                        pltpu.SemaphoreType.DMA((2, PAGE_SLOTS))],
    )
    body = functools.partial(_moba_sample_body, n_pages=n_pages, base=base, n_steps=n_steps,
                             ns=ns)
    return pl.pallas_call(
        body,
        name="moba_sample",
        grid_spec=grid_spec,
        out_shape=jax.ShapeDtypeStruct((b, r, w), BF16),
        compiler_params=_params("arbitrary"),
    )(page_table, q, k_new, v_new, cache_kt, cache_vt)


def _trunk(x, weights, ffn_w, norm_final, *, sample=None):
    b, l, d = x.shape
    depth = len(weights)
    x = x.reshape(b * l, d)
    gr = GROUP_ROWS
    states = []
    gdn_states = None
    for layer, wts in enumerate(weights):
        x = _ffn(x, wts["n_ffn1"], *ffn_w[0], layer)
        if sample is None:
            qkv, z, sc, ba, mk, mq, mv = _inproj(x, wts["n_mix"], wts["w_rows"], wts["w_qv_t"],
                                                 seq_len=l)
            grp = lambda a: a.reshape(b, l, a.shape[-1])
            o_a, o_b, gdn_states, gated = _mixer(
                grp(qkv), grp(z), grp(ba), grp(sc), wts["gdn_conv_w"], wts["gparams"],
                wts["gdn_norm_w"], wts["sc_conv_w"], c=GDN_CHUNK, ng=4 if b % 4 == 0 else 1,
                layer=layer, depth=depth, prev_states=gdn_states)
            o_c = _moba_prompt(mq, mk, mv)
        else:
            qkv, z, sc, ba, mk, mq, mv = _inproj(x, wts["n_mix"], wts["w_rows"], wts["w_qv"])
            ngroups = b * l // gr
            grp = lambda a: a.reshape(ngroups, gr, a.shape[-1])
            state = (grp(sample["gdn_conv"][layer]), grp(sample["sconv"][layer]), sample["gdn"])
            o_a, o_b, gdn_states, gated = _mixer(
                grp(qkv), grp(z), grp(ba), grp(sc), wts["gdn_conv_w"], wts["gparams"],
                wts["gdn_norm_w"], wts["sc_conv_w"], c=l, ng=2 if ngroups % 2 == 0 else 1,
                layer=layer, depth=depth, prev_states=gdn_states, state=state,
                row_lo=S_LO, row_hi=S_HI)
            seq = lambda a: a.reshape(b, l, a.shape[-1])
            o_c = _moba_sample(seq(mq), seq(mk), seq(mv), sample["cache_kt"], sample["cache_vt"],
                               sample["page_table"], layer * sample["n_pool"])
            o_c = o_c.reshape(b * l, -1)
        mix = (o_a.reshape(b * l, -1), o_b.reshape(b * l, -1), o_c, wts["w_out"])
        x = _ffn(x, wts["n_ffn2"], *ffn_w[1], layer, mix=mix,
                 final_w=norm_final if layer == depth - 1 else None)
        states.append(dict(qkv=qkv.reshape(b, l, -1), gated=gated.reshape(b, l, -1),
                           k=mk.reshape(b, l, -1), v=mv))
    return x.reshape(b, l, d), states, gdn_states


def kernel(x_prompt, x_sample, state_gdn, state_gdn_conv, state_sconv, cache_k, cache_v,
           page_table, norm_ffn1, ffn1_w_gate, ffn1_w_up, ffn1_w_down, norm_mix, w_in,
           gdn_conv_w, gdn_a_log, gdn_dt_bias, gdn_norm_w, sc_conv_w, w_out,
           norm_ffn2, ffn2_w_gate, ffn2_w_up, ffn2_w_down, norm_final):
    depth = w_in.shape[0]
    gw, sw, mw = GDN_WIDTH, SC_WIDTH, MOBA_WIDTH
    nh = GDN_HEADS
    mh, md = MOBA_HEADS, MOBA_HEAD_DIM

    weights = []
    for la in range(depth):
        w = w_in[la]
        ba_off = 4 * gw
        sc_off = ba_off + 2 * nh
        mo_off = sc_off + 3 * sw
        w_rows = jnp.concatenate(
            [w[:, :ba_off], w[:, sc_off:mo_off],
             jnp.pad(w[:, ba_off:sc_off], ((0, 0), (0, LANES - 2 * nh))),
             w[:, mo_off + mw:mo_off + 2 * mw]], axis=1).astype(BF16)
        w_qv = jnp.concatenate([w[:, mo_off:mo_off + mw], w[:, mo_off + 2 * mw:mo_off + 3 * mw]],
                               axis=1).astype(BF16)
        gparams = jnp.zeros((SUBLANES, LANES), F32)
        gparams = gparams.at[0, nh:2 * nh].set(gdn_a_log[la].astype(F32))
        gparams = gparams.at[1, nh:2 * nh].set(gdn_dt_bias[la].astype(F32))
        weights.append(dict(
            n_ffn1=norm_ffn1[la],
            n_mix=norm_mix[la], w_rows=w_rows, w_qv=w_qv, w_qv_t=w_qv.T,
            gdn_conv_w=gdn_conv_w[la], gparams=gparams, gdn_norm_w=gdn_norm_w[la],
            sc_conv_w=sc_conv_w[la], w_out=_to_bf16(w_out, la),
            n_ffn2=norm_ffn2[la]))
    ffn_w = ((ffn1_w_gate, ffn1_w_up, ffn1_w_down), (ffn2_w_gate, ffn2_w_up, ffn2_w_down))

    pb, pl_, _ = x_prompt.shape
    y_p, st_p, p_gdn = _trunk(x_prompt, weights, ffn_w, norm_final)
    hist = GDN_CONV - 1
    p_conv = jnp.stack([s["qkv"][:, -hist:] for s in st_p])
    p_sconv = jnp.stack([s["gated"][:, -(SC_CONV - 1):] for s in st_p])
    p_k = jnp.stack([s["k"] for s in st_p]).reshape(depth, pb, pl_, mh, md)
    p_v = jnp.stack([s["v"] for s in st_p]).reshape(depth, pb, mh, md, pl_)
    p_v = p_v.transpose(0, 1, 4, 2, 3)

    db, dl, d = x_sample.shape
    assert dl == S_HI - S_LO and (db * S_ROWS) % GROUP_ROWS == 0
    n_pool = cache_k.shape[1]
    tile_pad = lambda a, lo: jnp.pad(a, ((0, 0),) * (a.ndim - 2)
                                     + ((lo, S_ROWS - lo - a.shape[-2]), (0, 0)))
    seqs_per_group = GROUP_ROWS // S_ROWS
    paged_t = lambda c: c.transpose(0, 1, 3, 4, 2).reshape(depth * n_pool, mw, PAGE_SIZE)
    sample = dict(
        gdn=state_gdn.reshape(depth, db // seqs_per_group, seqs_per_group * nh,
                              GDN_HEAD_DIM, GDN_HEAD_DIM),
        gdn_conv=tile_pad(state_gdn_conv, 0),
        sconv=tile_pad(state_sconv, S_LO - (SC_CONV - 1)),
        cache_kt=paged_t(cache_k), cache_vt=paged_t(cache_v), n_pool=n_pool,
        page_table=page_table)
    y_s, st_s, s_gdn = _trunk(tile_pad(x_sample, S_LO), weights, ffn_w, norm_final,
                              sample=sample)
    y_s = y_s[:, S_LO:S_HI]
    s_gdn = s_gdn.reshape(state_gdn.shape)
    s_conv = jnp.stack([s["qkv"][:, S_HI - hist:S_HI] for s in st_s])
    s_sconv = jnp.stack([s["gated"][:, S_HI - (SC_CONV - 1):S_HI] for s in st_s])
    s_k = jnp.stack([s["k"][:, S_LO:S_HI] for s in st_s]).reshape(depth, db, dl, mh, md)
    s_v = jnp.stack([s["v"].reshape(db, S_ROWS, mw)[:, S_LO:S_HI] for s in st_s]).reshape(
        depth, db, dl, mh, md)
    return (y_p, y_s, p_gdn, p_conv, p_sconv, p_k, p_v, s_gdn, s_conv, s_sconv, s_k, s_v)
```

```python
import functools

import jax
import jax.numpy as jnp
from jax import lax
from jax.experimental import pallas as pl
from jax.experimental.pallas import tpu as pltpu

NORM_EPS = 1e-6
NEG_INF = -1e30

GDN_HEADS = 4
GDN_HEAD_DIM = 128
GDN_WIDTH = GDN_HEADS * GDN_HEAD_DIM
GDN_CONV = 4
GDN_CHUNK = 64
SC_WIDTH = 256
SC_CONV = 3
MOBA_HEADS = 4
MOBA_HEAD_DIM = 64
MOBA_WIDTH = MOBA_HEADS * MOBA_HEAD_DIM
MOBA_BLOCK = 256
MOBA_TOPK = 3
PAGE_SIZE = 128

LANES = 128
SUBLANES = 8
VMEM_LIMIT_BYTES = 56 * 1024 * 1024

S_ROWS = SUBLANES
S_LO = GDN_CONV - 1
S_HI = S_LO + 4

GROUP_ROWS = GDN_CHUNK
STACK_ROWS = GDN_HEADS * GROUP_ROWS

F32 = jnp.float32
BF16 = jnp.bfloat16
HI = lax.Precision.HIGHEST


def _dot(a, b, precision=None):
    return jnp.dot(a, b, preferred_element_type=F32, precision=precision)


def _dot_nt(a, b, precision=None):
    return lax.dot_general(a, b, (((1,), (1,)), ((), ())),
                           preferred_element_type=F32, precision=precision)


def _dot_tn(a, b, precision=None):
    return lax.dot_general(a, b, (((0,), (0,)), ((), ())),
                           preferred_element_type=F32, precision=precision)


def _rms(x, w):
    return x * lax.rsqrt(jnp.mean(x * x, axis=-1, keepdims=True) + NORM_EPS) * w


def _silu(x):
    return x * jax.nn.sigmoid(x)


def _softplus(x):
    return jnp.maximum(x, 0.0) + jnp.log1p(jnp.exp(-jnp.abs(x)))


def _resident(shape):
    nd = len(shape)
    return pl.BlockSpec(shape, lambda *_: (0,) * nd, pipeline_mode=pl.Buffered(1))


def _params(*semantics):
    return pltpu.CompilerParams(dimension_semantics=semantics,
                                vmem_limit_bytes=VMEM_LIMIT_BYTES)


def _cast_body(w_ref, o_ref):
    o_ref[...] = w_ref[0].astype(o_ref.dtype)


def _to_bf16(w, layer, rows=256):
    _, r, c = w.shape
    rows = min(rows, r)
    assert r % rows == 0
    return pl.pallas_call(
        _cast_body,
        name="to_bf16",
        grid=(r // rows,),
        in_specs=[pl.BlockSpec((1, rows, c), lambda i: (layer, i, 0))],
        out_specs=pl.BlockSpec((rows, c), lambda i: (i, 0)),
        out_shape=jax.ShapeDtypeStruct((r, c), BF16),
        compiler_params=_params("arbitrary"),
    )(w)


def _ffn_body(*refs, has_mix, oc_transposed, has_final, f_chunk, layer):
    refs = list(refs)
    x_ref = refs.pop(0)
    if has_mix:
        oa_ref, ob_ref, oc_ref, wo_ref = refs[:4]
        refs = refs[4:]
    nw_ref, wg_hbm, wu_hbm, wd_hbm = refs[:4]
    refs = refs[4:]
    if has_final:
        nf_ref = refs.pop(0)
    out_ref, xn_ref, acc_ref, wg_ref, wu_ref, wd_ref, wsem = refs
    i = pl.program_id(0)
    chunks = range(wg_ref.shape[1] // f_chunk)

    def chunk_copies(c):
        cols = pl.ds(c * f_chunk, f_chunk)
        return [pltpu.make_async_copy(wg_hbm.at[layer, :, cols], wg_ref.at[:, cols], wsem.at[0, c]),
                pltpu.make_async_copy(wu_hbm.at[layer, :, cols], wu_ref.at[:, cols], wsem.at[1, c]),
                pltpu.make_async_copy(wd_hbm.at[layer, cols, :], wd_ref.at[cols, :], wsem.at[2, c])]

    @pl.when(i == 0)
    def _():
        for c in chunks:
            for cp in chunk_copies(c):
                cp.start()

    def run(first_step):
        x = x_ref[...]
        if has_mix:
            ga = oa_ref.shape[1]
            gb = ga + ob_ref.shape[1]
            x = x + _dot(oa_ref[...], wo_ref[0:ga, :]) + _dot(ob_ref[...], wo_ref[ga:gb, :])
            if oc_transposed:
                x = x + _dot_tn(oc_ref[0], wo_ref[gb:, :])
            else:
                x = x + _dot(oc_ref[...], wo_ref[gb:, :])
        xn_ref[...] = _rms(x, nw_ref[...]).astype(BF16)
        for c in chunks:
            lo = c * f_chunk
            if first_step:
                for cp in chunk_copies(c):
                    cp.wait()
            xn = xn_ref[...]
            g = _dot(xn, wg_ref[:, lo:lo + f_chunk].astype(BF16))
            u = _dot(xn, wu_ref[:, lo:lo + f_chunk].astype(BF16))
            h = (_silu(g) * u).astype(BF16)
            part = _dot(h, wd_ref[lo:lo + f_chunk, :].astype(BF16))
            if c == 0:
                acc_ref[...] = part
            else:
                acc_ref[...] += part
        y = x + 0.5 * acc_ref[...]
        if has_final:
            y = _rms(y, nf_ref[...])
        out_ref[...] = y

    @pl.when(i == 0)
    def _():
        run(True)

    @pl.when(i != 0)
    def _():
        run(False)


def _ffn(x, nw, wg, wu, wd, layer, mix=None, final_w=None, tm=512, f_chunk=256):
    m, d = x.shape
    d_ff = wg.shape[2]
    hbm = pl.BlockSpec(memory_space=pl.ANY)
    tm = min(tm, m)
    assert m % tm == 0 and d_ff % f_chunk == 0
    row = lambda i: (i, 0)
    in_specs = [pl.BlockSpec((tm, d), row)]
    args = [x]
    oc_transposed = False
    if mix is not None:
        oa, ob, oc, wo = mix
        oc_transposed = oc.ndim == 3
        if oc_transposed:
            tiles = oc.shape[2] // tm
            assert oc.shape[2] % tm == 0
            oc_spec = pl.BlockSpec((1, oc.shape[1], tm), lambda i: (i // tiles, 0, i % tiles))
        else:
            oc_spec = pl.BlockSpec((tm, oc.shape[1]), row)
        in_specs += [pl.BlockSpec((tm, oa.shape[1]), row), pl.BlockSpec((tm, ob.shape[1]), row),
                     oc_spec, _resident(wo.shape)]
        args += [oa, ob, oc, wo]
    in_specs += [_resident((1, d)), hbm, hbm, hbm]
    args += [nw.reshape(1, d), wg, wu, wd]
    if final_w is not None:
        in_specs.append(_resident((1, d)))
        args.append(final_w.reshape(1, d))
    body = functools.partial(_ffn_body, has_mix=mix is not None, oc_transposed=oc_transposed,
                             has_final=final_w is not None, f_chunk=f_chunk, layer=layer)
    return pl.pallas_call(
        body,
        name="ffn_mix" if mix is not None else "ffn",
        grid=(m // tm,),
        in_specs=in_specs,
        out_specs=pl.BlockSpec((tm, d), row),
        out_shape=jax.ShapeDtypeStruct((m, d), F32),
        scratch_shapes=[pltpu.VMEM((tm, d), BF16), pltpu.VMEM((tm, d), F32),
                        pltpu.VMEM(wg.shape[1:], F32), pltpu.VMEM(wu.shape[1:], F32),
                        pltpu.VMEM(wd.shape[1:], F32),
                        pltpu.SemaphoreType.DMA((3, d_ff // f_chunk))],
        compiler_params=_params("arbitrary"),
    )(*args)


_ROW_WIDTHS = (3 * GDN_WIDTH, GDN_WIDTH, 3 * SC_WIDTH, LANES, MOBA_WIDTH)


def _causal_conv(buf, x, w_ref):
    pad = SUBLANES
    n = x.shape[0]
    taps = w_ref.shape[0]
    buf[pad:pad + n, :] = x
    y = None
    for t in range(taps - 1):
        lo = pad - (taps - 1) + t
        term = buf[lo:lo + n, :] * w_ref[t:t + 1, :]
        y = term if y is None else y + term
    y = y + x * w_ref[taps - 1:taps, :]
    tail = buf[n:n + pad, :]
    buf[0:pad, :] = tail
    return y


def _l2_normalize(x):
    return x * lax.rsqrt(jnp.sum(x * x, axis=-1, keepdims=True) + NORM_EPS)


def _gates(ba, gp_ref):
    return jax.nn.sigmoid(ba), -jnp.exp(gp_ref[0:1, :]) * _softplus(ba + gp_ref[1:2, :])


def _inproj_body(x_ref, nw_ref, w_ref, wqv_ref, *out_refs, transposed_qv):
    xn = _rms(x_ref[...], nw_ref[...]).astype(BF16)
    lo = 0
    for ref in out_refs[:len(_ROW_WIDTHS)]:
        width = ref.shape[1]
        ref[...] = _dot(xn, w_ref[:, lo:lo + width])
        lo += width
    q_ref, v_ref = out_refs[len(_ROW_WIDTHS):]
    mw = MOBA_WIDTH
    if transposed_qv:
        q_ref[0] = _dot_nt(wqv_ref[0:mw, :], xn)
        v_ref[0] = _dot_nt(wqv_ref[mw:2 * mw, :], xn)
    else:
        q_ref[...] = _dot(xn, wqv_ref[:, 0:mw])
        v_ref[...] = _dot(xn, wqv_ref[:, mw:2 * mw])


def _inproj(x, nw, w_rows, w_qv, *, seq_len=None, tm=512):
    m, d = x.shape
    tm = min(tm, m)
    assert m % tm == 0 and w_rows.shape[1] == sum(_ROW_WIDTHS)
    row = lambda i: (i, 0)
    out_specs = [pl.BlockSpec((tm, wd), row) for wd in _ROW_WIDTHS]
    out_shape = [jax.ShapeDtypeStruct((m, wd), F32) for wd in _ROW_WIDTHS]
    transposed_qv = seq_len is not None
    if transposed_qv:
        assert seq_len % tm == 0
        tiles = seq_len // tm
        spec = pl.BlockSpec((1, MOBA_WIDTH, tm), lambda i: (i // tiles, 0, i % tiles))
        shape = jax.ShapeDtypeStruct((m // seq_len, MOBA_WIDTH, seq_len), F32)
    else:
        spec = pl.BlockSpec((tm, MOBA_WIDTH), row)
        shape = jax.ShapeDtypeStruct((m, MOBA_WIDTH), F32)
    out_specs += [spec, spec]
    out_shape += [shape, shape]
    return pl.pallas_call(
        functools.partial(_inproj_body, transposed_qv=transposed_qv),
        name="inproj",
        grid=(m // tm,),
        in_specs=[pl.BlockSpec((tm, d), row), _resident((1, d)), _resident(w_rows.shape),
                  _resident(w_qv.shape)],
        out_specs=out_specs,
        out_shape=out_shape,
        compiler_params=_params("arbitrary"),
    )(x, nw.reshape(1, d), w_rows, w_qv)


def _mixer_body(*refs, c, ng, has_state, row_lo, row_hi):
    refs = list(refs)
    qkv_ref, z_ref, ba_ref, sc_ref = refs[:4]
    refs = refs[4:]
    if has_state:
        qh_ref, sh_ref, s0_ref = refs[:3]
        refs = refs[3:]
    cw_ref, gp_ref, nw_ref, scw_ref = refs[:4]
    refs = refs[5:]
    oa_ref, ob_ref, st_ref, gt_ref = refs[:4]
    xbuf, gbuf, s_ref = refs[4:]
    n = pl.program_id(1)
    hd = GDN_HEAD_DIM
    gr = GROUP_ROWS
    sr = STACK_ROWS
    pad = SUBLANES
    seqs = gr // c
    shift = c.bit_length() - 1
    assert 1 << shift == c

    @pl.when(n == 0)
    def _():
        xbuf[:, 0:pad, :] = jnp.zeros((ng, pad, xbuf.shape[2]), F32)
        gbuf[:, 0:pad, :] = jnp.zeros((ng, pad, gbuf.shape[2]), F32)
        if has_state:
            s_ref[...] = s0_ref[0]
        else:
            s_ref[...] = jnp.zeros(s_ref.shape, F32)

    rows = lax.broadcasted_iota(jnp.int32, (gr, 1), 0) & (c - 1)
    is_hist = rows < row_lo
    live = jnp.where((rows >= row_lo) & (rows < row_hi), 1.0, 0.0)
    ri = lax.broadcasted_iota(jnp.int32, (gr, gr), 0)
    ci = lax.broadcasted_iota(jnp.int32, (gr, gr), 1)
    same = (ri >> shift) == (ci >> shift)
    cum_op = jnp.concatenate([jnp.where(same & (ri >= ci), 1.0, 0.0),
                              jnp.where(same, 1.0, 0.0)], axis=0)
    nh = GDN_HEADS
    tiles = sr // LANES
    band_tile = [h * gr // LANES for h in range(nh)]

    def bands(mat):
        return [mat[h * gr:(h + 1) * gr, band_tile[h] * LANES:(band_tile[h] + 1) * LANES]
                for h in range(nh)]

    def unband(pieces):
        zero = jnp.zeros((gr, LANES), BF16)
        return jnp.concatenate(
            [jnp.concatenate([p.astype(BF16) if t == band_tile[h] else zero
                              for t in range(tiles)], axis=1)
             for h, p in enumerate(pieces)], axis=0)

    incl, strict, eye = [], [], []
    for h in range(nh):
        rs = lax.broadcasted_iota(jnp.int32, (gr, LANES), 0) + h * gr
        cs = lax.broadcasted_iota(jnp.int32, (gr, LANES), 1) + band_tile[h] * LANES
        same_s = (rs >> shift) == (cs >> shift)
        incl.append(same_s & (rs >= cs))
        strict.append(same_s & (rs > cs))
        eye.append(jnp.where(rs == cs, 1.0, 0.0))

    def stack_heads(a, lo):
        return jnp.concatenate([a[:, lo + h * hd:lo + (h + 1) * hd] for h in range(GDN_HEADS)],
                               axis=0)

    def stack_cols(a, lo):
        return jnp.concatenate([a[:, lo + h:lo + h + 1] for h in range(GDN_HEADS)], axis=0)

    groups = range(ng)

    def front(g):
        x = qkv_ref[g]
        if has_state:
            x = jnp.where(is_hist, qh_ref[g], x)
        act = _silu(_causal_conv(xbuf.at[g], x, cw_ref))
        q = _l2_normalize(stack_heads(act, 0)) * (hd ** -0.5)
        k = _l2_normalize(stack_heads(act, GDN_WIDTH))
        v = stack_heads(act, 2 * GDN_WIDTH)

        sc = sc_ref[g]
        scw = SC_WIDTH
        gated = sc[:, scw:2 * scw] * sc[:, 0:scw]
        if has_state:
            gated = jnp.where(is_hist, sh_ref[g], gated)
        yb = _causal_conv(gbuf.at[g], gated, scw_ref)
        ob_ref[g] = (sc[:, 2 * scw:3 * scw] * yb).astype(ob_ref.dtype)
        gt_ref[g] = gated

        beta_all, g_all = _gates(ba_ref[g], gp_ref)
        beta_all, g_all = beta_all * live, g_all * live
        cum = _dot(cum_op, g_all, HI)
        gc_all, glast_all = cum[:gr], cum[gr:]
        beta = stack_cols(beta_all, 0)
        gc = stack_cols(gc_all, nh)
        eg = jnp.exp(gc)
        etail = jnp.exp(stack_cols(glast_all, nh) - gc)
        gc_row = jnp.broadcast_to(gc, (sr, LANES)).T[0:1, :]
        decay = []
        for h in range(nh):
            diff = (gc[h * gr:(h + 1) * gr]
                    - gc_row[:, band_tile[h] * LANES:(band_tile[h] + 1) * LANES])
            decay.append(jnp.where(incl[h], jnp.exp(jnp.where(incl[h], diff, 0.0)), 0.0))

        kb = k.astype(BF16)
        kk = bands(_dot_nt(kb, kb))
        m_strict = [jnp.where(strict[h], beta[h * gr:(h + 1) * gr] * kk[h] * decay[h], 0.0)
                    for h in range(nh)]
        return dict(q=q, k=k, v=v, kb=kb, beta=beta, eg=eg, etail=etail, decay=decay,
                    cd_all=jnp.exp(glast_all), m_strict=m_strict)

    fr = [front(g) for g in groups]

    invs = [[e - m for e, m in zip(eye, f["m_strict"])] for f in fr]
    pbs = [unband(f["m_strict"]) for f in fr]
    for _ in range(shift - 1):
        pbs = [unband(bands(_dot(pb, pb))) for pb in pbs]
        invs = [[i + d for i, d in zip(inv, bands(_dot(unband(inv), pb)))]
                for inv, pb in zip(invs, pbs)]

    def solve(f, inv):
        rhs = jnp.concatenate([f["k"] * (f["beta"] * f["eg"]), f["v"] * f["beta"]],
                              axis=-1).astype(BF16)
        sol = _dot(unband(inv), rhs)
        qk = bands(_dot_nt(f["q"].astype(BF16), f["kb"]))
        attn = unband([a * d for a, d in zip(qk, f["decay"])])
        return dict(w_k=sol[:, :hd], u_v=sol[:, hd:], attn=attn, qd=f["q"] * f["eg"],
                    kt=(f["k"] * f["etail"]).astype(BF16))

    sv = [solve(f, inv) for f, inv in zip(fr, invs)]

    pair_ids = [(h, s) for h in range(GDN_HEADS) for s in range(seqs)]

    def read_state(g, t):
        parts = []
        for h, s in pair_ids:
            r0 = h * gr + s * c
            s_old = s_ref[g, s * nh + h]
            lhs = jnp.concatenate([t["w_k"][r0:r0 + c], t["qd"][r0:r0 + c]], axis=0)
            res = _dot(lhs.astype(BF16), s_old.astype(BF16))
            parts.append((s_old, t["u_v"][r0:r0 + c] - res[:c], res[c:]))
        return parts

    rd = [read_state(g, t) for g, t in zip(groups, sv)]
    us = [jnp.concatenate([p[1] for p in parts], axis=0) for parts in rd]
    outs = [jnp.concatenate([p[2] for p in parts], axis=0) + _dot(t["attn"], u.astype(BF16))
            for parts, t, u in zip(rd, sv, us)]
    for g in groups:
        for (h, s), (s_old, u_p, _) in zip(pair_ids, rd[g]):
            r0 = h * gr + s * c
            cd = fr[g]["cd_all"][s * c:s * c + 1, nh + h:nh + h + 1]
            s_ref[g, s * nh + h] = s_old * cd + _dot_tn(sv[g]["kt"][r0:r0 + c],
                                                        u_p.astype(BF16))
    for g in groups:
        zs = stack_heads(z_ref[g], 0)
        o_n = _rms(outs[g], nw_ref[...]) * _silu(zs)
        oa_ref[g] = jnp.concatenate([o_n[h * gr:(h + 1) * gr] for h in range(GDN_HEADS)],
                                    axis=1).astype(oa_ref.dtype)

    @pl.when(n == pl.num_programs(1) - 1)
    def _():
        st_ref[0] = s_ref[...]


def _mixer(qkv, z, ba, sc, conv_w, gparams, norm_w, sc_conv_w, *, c, ng, layer, depth,
           prev_states=None, state=None, row_lo=0, row_hi=None):
    g_total, l, _ = qkv.shape
    gr = GROUP_ROWS
    assert l % gr == 0 and gr % c == 0 and g_total % ng == 0
    nc = l // gr
    pairs = GDN_HEADS * (gr // c)
    row_hi = c if row_hi is None else row_hi
    blk = lambda w: pl.BlockSpec((ng, gr, w), lambda i, n: (i, n, 0))
    in_specs = [blk(qkv.shape[2]), blk(z.shape[2]), blk(ba.shape[2]), blk(sc.shape[2])]
    args = [qkv, z, ba, sc]
    st_shape = (g_total, pairs, GDN_HEAD_DIM, GDN_HEAD_DIM)
    st_spec = pl.BlockSpec((1, ng) + st_shape[1:], lambda i, n: (layer, i, 0, 0, 0))
    if state is not None:
        assert nc == 1
        qh, sh, s0 = state
        assert s0.shape[1:] == st_shape
        in_specs += [pl.BlockSpec((ng, gr, qh.shape[2]), lambda i, n: (i, 0, 0)),
                     pl.BlockSpec((ng, gr, sh.shape[2]), lambda i, n: (i, 0, 0)), st_spec]
        args += [qh, sh, s0]
    const = lambda a: pl.BlockSpec(a.shape, lambda i, n: (0,) * a.ndim)
    norm_w = norm_w.reshape(1, -1)
    in_specs += [const(conv_w), const(gparams), const(norm_w), const(sc_conv_w)]
    args += [conv_w, gparams, norm_w, sc_conv_w]
    if prev_states is None:
        prev_states = jnp.zeros((depth,) + st_shape, F32)
    assert prev_states.shape == (depth,) + st_shape
    aliases = {len(args): 2}
    in_specs.append(pl.BlockSpec(memory_space=pl.ANY))
    args.append(prev_states)
    out_shape = [jax.ShapeDtypeStruct((g_total, l, GDN_WIDTH), BF16),
                 jax.ShapeDtypeStruct((g_total, l, SC_WIDTH), BF16),
                 jax.ShapeDtypeStruct((depth,) + st_shape, F32),
                 jax.ShapeDtypeStruct((g_total, l, SC_WIDTH), F32)]
    out_specs = [blk(GDN_WIDTH), blk(SC_WIDTH), st_spec, blk(SC_WIDTH)]
    body = functools.partial(_mixer_body, c=c, ng=ng, has_state=state is not None,
                             row_lo=row_lo, row_hi=row_hi)
    return pl.pallas_call(
        body,
        name="seq_mixer",
        grid=(g_total // ng, nc),
        in_specs=in_specs,
        out_specs=out_specs,
        out_shape=out_shape,
        input_output_aliases=aliases,
        scratch_shapes=[pltpu.VMEM((ng, gr + SUBLANES, qkv.shape[2]), F32),
                        pltpu.VMEM((ng, gr + SUBLANES, SC_WIDTH), F32),
                        pltpu.VMEM((ng,) + st_shape[1:], F32)],
        compiler_params=_params("arbitrary", "arbitrary"),
    )(*args)


def _select_blocks(gate, n_valid, axis=0):
    nb = gate.shape[axis]
    sub = lax.broadcasted_iota(jnp.int32, gate.shape, axis)
    gate = jnp.where(sub < n_valid, gate, NEG_INF)
    sel = jnp.zeros(gate.shape, F32)
    for j in range(nb):
        gj = gate[j:j + 1, :] if axis == 0 else gate[:, j:j + 1]
        beats = (gate > gj) | ((gate == gj) & (sub < j))
        cnt = jnp.sum(jnp.where(beats, 1.0, 0.0), axis=axis, keepdims=True)
        hit = (cnt < MOBA_TOPK) & (j < n_valid)
        sel = jnp.where((sub == j) & hit, 1.0, sel)
    return sel


_PV_ROWS = MOBA_HEAD_DIM + 16


def _moba_prompt_body(q_ref, k_ref, v_ref, o_ref, kmean_ref, kb_ref, vb_ref, sel_ref, *, nbat):
    i = pl.program_id(1)
    blk = MOBA_BLOCK
    hd = MOBA_HEAD_DIM
    w = MOBA_WIDTH
    l_seq = v_ref.shape[2]
    nb = l_seq // blk
    heads = range(MOBA_HEADS)
    head_rows = [slice(h * hd, (h + 1) * hd) for h in heads]
    chains = [(b, h) for b in range(nbat) for h in heads]

    @pl.when(i == 0)
    def _():
        ones = jnp.ones((_PV_ROWS - hd, blk), BF16)
        for b in range(nbat):
            for j in range(nb):
                kj = k_ref[b * l_seq + j * blk:b * l_seq + (j + 1) * blk, :]
                kmean_ref[b, j:j + 1, :] = jnp.mean(kj, axis=0, keepdims=True)
                kb_ref[b, j] = kj.astype(BF16)
                for h in heads:
                    vb_ref[b, j, h, 0:hd, :] = v_ref[b, head_rows[h],
                                                     j * blk:(j + 1) * blk].astype(BF16)
                    vb_ref[b, j, h, hd:_PV_ROWS, :] = ones

    feat = lax.broadcasted_iota(jnp.int32, (w, blk), 0)
    key_i = lax.broadcasted_iota(jnp.int32, (blk, blk), 0)
    qry_i = lax.broadcasted_iota(jnp.int32, (blk, blk), 1)
    causal = key_i <= qry_i
    qmb = []
    for b in range(nbat):
        qt = q_ref[b] * (hd ** -0.5)
        qms = [jnp.where((feat >= h * hd) & (feat < (h + 1) * hd), qt, 0.0) for h in heads]
        qmb.append(jnp.concatenate(qms, axis=1).astype(BF16))
        for h in heads:
            gate = _dot(kmean_ref[b], qms[h], HI)
            sel_ref[b, h] = _select_blocks(gate, i)

    def scores(j):
        s_all = [_dot(kb_ref[b, j], qmb[b]) for b in range(nbat)]
        return [s_all[b][:, h * blk:(h + 1) * blk] for b, h in chains]

    def attend(j, s_list, m_list):
        ps = [jnp.exp(s - m).astype(BF16) for s, m in zip(s_list, m_list)]
        return [_dot(vb_ref[b, j, h], p) for (b, h), p in zip(chains, ps)]

    s0 = [jnp.where(causal, s, NEG_INF) for s in scores(i)]
    m0 = [jnp.max(s, axis=0, keepdims=True) for s in s0]
    pv0 = attend(i, s0, m0)
    init = []
    for m, pv in zip(m0, pv0):
        init += [m, pv[hd:hd + 1], pv[:hd]]

    def past_block(j, carry):
        s_list = [jnp.where(sel_ref[b, h, pl.ds(j, 1), :] > 0.5, s, NEG_INF)
                  for (b, h), s in zip(chains, scores(j))]
        m_old = carry[0::3]
        m_new = [jnp.maximum(m, jnp.max(s, axis=0, keepdims=True))
                 for m, s in zip(m_old, s_list)]
        pv = attend(j, s_list, m_new)
        new = []
        for c in range(len(chains)):
            alpha = jnp.exp(m_old[c] - m_new[c])
            new += [m_new[c], alpha * carry[3 * c + 1] + pv[c][hd:hd + 1],
                    alpha * carry[3 * c + 2] + pv[c][:hd]]
        return tuple(new)

    final = lax.fori_loop(0, i, past_block, tuple(init))
    for c, (b, h) in enumerate(chains):
        o_ref[b, head_rows[h], :] = (final[3 * c + 2] / final[3 * c + 1]).astype(o_ref.dtype)


def _moba_prompt(qt, k, vt, nbat=2):
    b, w, l = qt.shape
    blk = MOBA_BLOCK
    nbat = nbat if b % nbat == 0 else 1
    assert l % blk == 0
    nb = l // blk
    return pl.pallas_call(
        functools.partial(_moba_prompt_body, nbat=nbat),
        name="moba_prompt",
        grid=(b // nbat, nb),
        in_specs=[pl.BlockSpec((nbat, w, blk), lambda i, t: (i, 0, t)),
                  pl.BlockSpec((nbat * l, w), lambda i, t: (i, 0)),
                  pl.BlockSpec((nbat, w, l), lambda i, t: (i, 0, 0))],
        out_specs=pl.BlockSpec((nbat, w, blk), lambda i, t: (i, 0, t)),
        out_shape=jax.ShapeDtypeStruct((b, w, l), BF16),
        scratch_shapes=[pltpu.VMEM((nbat, nb, w), F32), pltpu.VMEM((nbat, nb, blk, w), BF16),
                        pltpu.VMEM((nbat, nb, MOBA_HEADS, _PV_ROWS, blk), BF16),
                        pltpu.VMEM((nbat, MOBA_HEADS, nb, blk), F32)],
        compiler_params=_params("arbitrary", "arbitrary"),
    )(qt, k, vt)


PAGE_SLOTS = 3
SEQS_PER_STEP = 2


def _moba_sample_body(pt_ref, q_ref, kn_ref, vn_ref, ck_ref, cv_ref, o_ref, kbuf, vbuf, sem,
                      *, n_pages, base, n_steps, ns):
    i = pl.program_id(0)
    hd = MOBA_HEAD_DIM
    r = S_ROWS
    w = MOBA_WIDTH
    ppb = MOBA_BLOCK // PAGE_SIZE
    nb = n_pages // ppb
    ahead = PAGE_SLOTS - 1
    seqs = range(ns)
    blocks = range(nb)

    def page_copies(step, slot):
        cps = []
        for s in seqs:
            for p in range(n_pages):
                row = base + pt_ref[step * ns + s, p]
                cps.append(pltpu.make_async_copy(ck_ref.at[row], kbuf.at[slot, s, p],
                                                 sem.at[0, slot]))
                cps.append(pltpu.make_async_copy(cv_ref.at[row], vbuf.at[slot, s, p],
                                                 sem.at[1, slot]))
        return cps

    def start_all(cps):
        for n, cp in enumerate(cps):
            cp.start(priority=n % 2)

    @pl.when(i == 0)
    def _():
        for d in range(ahead):
            start_all(page_copies(d, d))

    slot = lax.rem(i, PAGE_SLOTS)
    for cp in page_copies(i, slot):
        cp.wait()

    lane = lax.broadcasted_iota(jnp.int32, (r, w), 1)
    head_masks = [jnp.where((lane >= h * hd) & (lane < (h + 1) * hd), 1.0, 0.0)
                  for h in range(MOBA_HEADS)]
    col = lax.broadcasted_iota(jnp.int32, (w, LANES), 1)
    q_bd = [jnp.concatenate([q_ref[s] * (hd ** -0.5) * hm for hm in head_masks], axis=0)
            for s in seqs]
    q_bb = [x.astype(BF16) for x in q_bd]

    kp = [[kbuf[slot, s, p] for p in range(n_pages)] for s in seqs]
    s_pages = [[_dot(q_bb[s], x.astype(BF16)) for x in kp[s]] for s in seqs]
    kmean_t = []
    for s in seqs:
        km = jnp.zeros((w, LANES), F32)
        for j in blocks:
            ksum = kp[s][j * ppb]
            for pp in range(1, ppb):
                ksum = ksum + kp[s][j * ppb + pp]
            km = jnp.where(col == j, jnp.sum(ksum, axis=1, keepdims=True) * (1.0 / MOBA_BLOCK),
                           km)
        kmean_t.append(km)
    s_blk = [[jnp.concatenate(s_pages[s][j * ppb:(j + 1) * ppb], axis=-1) for j in blocks]
             for s in seqs]
    m_cols = [[jnp.max(x, axis=1, keepdims=True) for x in s_blk[s]] for s in seqs]
    e_blk = [[jnp.exp(x - m) for x, m in zip(s_blk[s], m_cols[s])] for s in seqs]
    l_cols = [[jnp.sum(e, axis=1, keepdims=True) for e in e_blk[s]] for s in seqs]
    accs = []
    for s in seqs:
        row_accs = []
        for j in blocks:
            eb = e_blk[s][j].astype(BF16)
            acc = None
            for pp in range(ppb):
                vp = vbuf[slot, s, j * ppb + pp].astype(BF16)
                part = _dot_nt(eb[:, pp * PAGE_SIZE:(pp + 1) * PAGE_SIZE], vp)
                acc = part if acc is None else acc + part
            row_accs.append(acc)
        accs.append(row_accs)

    nxt = jnp.minimum(i + ahead, n_steps - 1)
    start_all(page_copies(nxt, lax.rem(i + ahead, PAGE_SLOTS)))

    gates = [_dot(q_bd[s], kmean_t[s], HI)[:, :nb] for s in seqs]
    sels = [_select_blocks(g, nb, axis=1) > 0.5 for g in gates]
    qt = lax.broadcasted_iota(jnp.int32, (MOBA_HEADS * r, r), 0) & (r - 1)
    kt = lax.broadcasted_iota(jnp.int32, (MOBA_HEADS * r, r), 1)
    own = (kt >= S_LO) & (kt <= qt)
    for s in seqs:
        sel = sels[s]
        m_all = jnp.concatenate(m_cols[s], axis=1)
        l_all = jnp.concatenate(l_cols[s], axis=1)
        s_own = _dot_nt(q_bb[s], kn_ref[s].astype(BF16))
        m_tot = jnp.maximum(jnp.max(jnp.where(sel, m_all, NEG_INF), axis=1, keepdims=True),
                            jnp.max(jnp.where(own, s_own, NEG_INF), axis=1, keepdims=True))
        wj = jnp.where(sel, jnp.exp(jnp.where(sel, m_all - m_tot, 0.0)), 0.0)
        p_own = jnp.where(own, jnp.exp(jnp.where(own, s_own - m_tot, 0.0)), 0.0)
        l_tot = (jnp.sum(wj * l_all, axis=1, keepdims=True)
                 + jnp.sum(p_own, axis=1, keepdims=True))
        acc = _dot(p_own.astype(BF16), vn_ref[s].astype(BF16))
        for j in blocks:
            acc = acc + wj[:, j:j + 1] * accs[s][j]
        o_bd = acc / l_tot
        o = None
        for h in range(MOBA_HEADS):
            part = o_bd[h * r:(h + 1) * r, :] * head_masks[h]
            o = part if o is None else o + part
        o_ref[s] = o.astype(o_ref.dtype)

    @pl.when(i == n_steps - 1)
    def _():
        for d in range(1, PAGE_SLOTS):
            for cp in page_copies(n_steps - 1, lax.rem(i + d, PAGE_SLOTS)):
                cp.wait()


def _moba_sample(q, k_new, v_new, cache_kt, cache_vt, page_table, base):
    b, r, w = q.shape
    page = cache_kt.shape[2]
    n_pages = page_table.shape[1]
    ns = SEQS_PER_STEP if b % SEQS_PER_STEP == 0 else 1
    n_steps = b // ns
    assert page == PAGE_SIZE and r == S_ROWS and cache_kt.shape[1] == w
    assert n_steps >= PAGE_SLOTS
    tile = pl.BlockSpec((ns, r, w), lambda i, pt: (i, 0, 0))
    hbm = pl.BlockSpec(memory_space=pl.ANY)
    grid_spec = pltpu.PrefetchScalarGridSpec(
        num_scalar_prefetch=1,
        grid=(n_steps,),
        in_specs=[tile, tile, tile, hbm, hbm],
        out_specs=tile,
        scratch_shapes=[pltpu.VMEM((PAGE_SLOTS, ns, n_pages, w, page), F32),
                        pltpu.VMEM((PAGE_SLOTS, ns, n_pages, w, page), F32),
                        pltpu.SemaphoreType.DMA((2, PAGE_SLOTS))],
    )
    body = functools.partial(_moba_sample_body, n_pages=n_pages, base=base, n_steps=n_steps,
                             ns=ns)
    return pl.pallas_call(
        body,
        name="moba_sample",
        grid_spec=grid_spec,
        out_shape=jax.ShapeDtypeStruct((b, r, w), BF16),
        compiler_params=_params("arbitrary"),
    )(page_table, q, k_new, v_new, cache_kt, cache_vt)


def _trunk(x, weights, ffn_w, norm_final, *, sample=None):
    b, l, d = x.shape
    depth = len(weights)
    x = x.reshape(b * l, d)
    gr = GROUP_ROWS
    states = []
    gdn_states = None
    for layer, wts in enumerate(weights):
        x = _ffn(x, wts["n_ffn1"], *ffn_w[0], layer)
        if sample is None:
            qkv, z, sc, ba, mk, mq, mv = _inproj(x, wts["n_mix"], wts["w_rows"], wts["w_qv_t"],
                                                 seq_len=l)
            grp = lambda a: a.reshape(b, l, a.shape[-1])
            o_a, o_b, gdn_states, gated = _mixer(
                grp(qkv), grp(z), grp(ba), grp(sc), wts["gdn_conv_w"], wts["gparams"],
                wts["gdn_norm_w"], wts["sc_conv_w"], c=GDN_CHUNK, ng=4 if b % 4 == 0 else 1,
                layer=layer, depth=depth, prev_states=gdn_states)
            o_c = _moba_prompt(mq, mk, mv)
        else:
            qkv, z, sc, ba, mk, mq, mv = _inproj(x, wts["n_mix"], wts["w_rows"], wts["w_qv"])
            ngroups = b * l // gr
            grp = lambda a: a.reshape(ngroups, gr, a.shape[-1])
            state = (grp(sample["gdn_conv"][layer]), grp(sample["sconv"][layer]), sample["gdn"])
            o_a, o_b, gdn_states, gated = _mixer(
                grp(qkv), grp(z), grp(ba), grp(sc), wts["gdn_conv_w"], wts["gparams"],
                wts["gdn_norm_w"], wts["sc_conv_w"], c=l, ng=2 if ngroups % 2 == 0 else 1,
                layer=layer, depth=depth, prev_states=gdn_states, state=state,
                row_lo=S_LO, row_hi=S_HI)
            seq = lambda a: a.reshape(b, l, a.shape[-1])
            o_c = _moba_sample(seq(mq), seq(mk), seq(mv), sample["cache_kt"], sample["cache_vt"],
                               sample["page_table"], layer * sample["n_pool"])
            o_c = o_c.reshape(b * l, -1)
        mix = (o_a.reshape(b * l, -1), o_b.reshape(b * l, -1), o_c, wts["w_out"])
        x = _ffn(x, wts["n_ffn2"], *ffn_w[1], layer, mix=mix,
                 final_w=norm_final if layer == depth - 1 else None)
        states.append(dict(qkv=qkv.reshape(b, l, -1), gated=gated.reshape(b, l, -1),
                           k=mk.reshape(b, l, -1), v=mv))
    return x.reshape(b, l, d), states, gdn_states


def kernel(x_prompt, x_sample, state_gdn, state_gdn_conv, state_sconv, cache_k, cache_v,
           page_table, norm_ffn1, ffn1_w_gate, ffn1_w_up, ffn1_w_down, norm_mix, w_in,
           gdn_conv_w, gdn_a_log, gdn_dt_bias, gdn_norm_w, sc_conv_w, w_out,
           norm_ffn2, ffn2_w_gate, ffn2_w_up, ffn2_w_down, norm_final):
    depth = w_in.shape[0]
    gw, sw, mw = GDN_WIDTH, SC_WIDTH, MOBA_WIDTH
    nh = GDN_HEADS
    mh, md = MOBA_HEADS, MOBA_HEAD_DIM

    weights = []
    for la in range(depth):
        w = w_in[la]
        ba_off = 4 * gw
        sc_off = ba_off + 2 * nh
        mo_off = sc_off + 3 * sw
        w_rows = jnp.concatenate(
            [w[:, :ba_off], w[:, sc_off:mo_off],
             jnp.pad(w[:, ba_off:sc_off], ((0, 0), (0, LANES - 2 * nh))),
             w[:, mo_off + mw:mo_off + 2 * mw]], axis=1).astype(BF16)
        w_qv = jnp.concatenate([w[:, mo_off:mo_off + mw], w[:, mo_off + 2 * mw:mo_off + 3 * mw]],
                               axis=1).astype(BF16)
        gparams = jnp.zeros((SUBLANES, LANES), F32)
        gparams = gparams.at[0, nh:2 * nh].set(gdn_a_log[la].astype(F32))
        gparams = gparams.at[1, nh:2 * nh].set(gdn_dt_bias[la].astype(F32))
        weights.append(dict(
            n_ffn1=norm_ffn1[la],
            n_mix=norm_mix[la], w_rows=w_rows, w_qv=w_qv, w_qv_t=w_qv.T,
            gdn_conv_w=gdn_conv_w[la], gparams=gparams, gdn_norm_w=gdn_norm_w[la],
            sc_conv_w=sc_conv_w[la], w_out=_to_bf16(w_out, la),
            n_ffn2=norm_ffn2[la]))
    ffn_w = ((ffn1_w_gate, ffn1_w_up, ffn1_w_down), (ffn2_w_gate, ffn2_w_up, ffn2_w_down))

    pb, pl_, _ = x_prompt.shape
    y_p, st_p, p_gdn = _trunk(x_prompt, weights, ffn_w, norm_final)
    hist = GDN_CONV - 1
    p_conv = jnp.stack([s["qkv"][:, -hist:] for s in st_p])
    p_sconv = jnp.stack([s["gated"][:, -(SC_CONV - 1):] for s in st_p])
    p_k = jnp.stack([s["k"] for s in st_p]).reshape(depth, pb, pl_, mh, md)
    p_v = jnp.stack([s["v"] for s in st_p]).reshape(depth, pb, mh, md, pl_)
    p_v = p_v.transpose(0, 1, 4, 2, 3)

    db, dl, d = x_sample.shape
    assert dl == S_HI - S_LO and (db * S_ROWS) % GROUP_ROWS == 0
    n_pool = cache_k.shape[1]
    tile_pad = lambda a, lo: jnp.pad(a, ((0, 0),) * (a.ndim - 2)
                                     + ((lo, S_ROWS - lo - a.shape[-2]), (0, 0)))
    seqs_per_group = GROUP_ROWS // S_ROWS
    paged_t = lambda c: c.transpose(0, 1, 3, 4, 2).reshape(depth * n_pool, mw, PAGE_SIZE)
    sample = dict(
        gdn=state_gdn.reshape(depth, db // seqs_per_group, seqs_per_group * nh,
                              GDN_HEAD_DIM, GDN_HEAD_DIM),
        gdn_conv=tile_pad(state_gdn_conv, 0),
        sconv=tile_pad(state_sconv, S_LO - (SC_CONV - 1)),
        cache_kt=paged_t(cache_k), cache_vt=paged_t(cache_v), n_pool=n_pool,
        page_table=page_table)
    y_s, st_s, s_gdn = _trunk(tile_pad(x_sample, S_LO), weights, ffn_w, norm_final,
                              sample=sample)
    y_s = y_s[:, S_LO:S_HI]
    s_gdn = s_gdn.reshape(state_gdn.shape)
    s_conv = jnp.stack([s["qkv"][:, S_HI - hist:S_HI] for s in st_s])
    s_sconv = jnp.stack([s["gated"][:, S_HI - (SC_CONV - 1):S_HI] for s in st_s])
    s_k = jnp.stack([s["k"][:, S_LO:S_HI] for s in st_s]).reshape(depth, db, dl, mh, md)
    s_v = jnp.stack([s["v"].reshape(db, S_ROWS, mw)[:, S_LO:S_HI] for s in st_s]).reshape(
        depth, db, dl, mh, md)
    return (y_p, y_s, p_gdn, p_conv, p_sconv, p_k, p_v, s_gdn, s_conv, s_sconv, s_k, s_v)
```

```python
import functools

import jax
import jax.numpy as jnp
from jax import lax
from jax.experimental import pallas as pl
from jax.experimental.pallas import tpu as pltpu

NORM_EPS = 1e-6
NEG_INF = -1e30

GDN_HEADS = 4
GDN_HEAD_DIM = 128
GDN_WIDTH = GDN_HEADS * GDN_HEAD_DIM
GDN_CONV = 4
GDN_CHUNK = 64
SC_WIDTH = 256
SC_CONV = 3
MOBA_HEADS = 4
MOBA_HEAD_DIM = 64
MOBA_WIDTH = MOBA_HEADS * MOBA_HEAD_DIM
MOBA_BLOCK = 256
MOBA_TOPK = 3
PAGE_SIZE = 128

LANES = 128
SUBLANES = 8
VMEM_LIMIT_BYTES = 56 * 1024 * 1024

S_ROWS = SUBLANES
S_LO = GDN_CONV - 1
S_HI = S_LO + 4

GROUP_ROWS = GDN_CHUNK
STACK_ROWS = GDN_HEADS * GROUP_ROWS

F32 = jnp.float32
BF16 = jnp.bfloat16
HI = lax.Precision.HIGHEST


def _dot(a, b, precision=None):
    return jnp.dot(a, b, preferred_element_type=F32, precision=precision)


def _dot_nt(a, b, precision=None):
    return lax.dot_general(a, b, (((1,), (1,)), ((), ())),
                           preferred_element_type=F32, precision=precision)


def _dot_tn(a, b, precision=None):
    return lax.dot_general(a, b, (((0,), (0,)), ((), ())),
                           preferred_element_type=F32, precision=precision)


def _rms(x, w):
    return x * lax.rsqrt(jnp.mean(x * x, axis=-1, keepdims=True) + NORM_EPS) * w


def _silu(x):
    return x * jax.nn.sigmoid(x)


def _softplus(x):
    return jnp.maximum(x, 0.0) + jnp.log1p(jnp.exp(-jnp.abs(x)))


def _resident(shape):
    nd = len(shape)
    return pl.BlockSpec(shape, lambda *_: (0,) * nd, pipeline_mode=pl.Buffered(1))


def _params(*semantics):
    return pltpu.CompilerParams(dimension_semantics=semantics,
                                vmem_limit_bytes=VMEM_LIMIT_BYTES)


def _cast_body(w_ref, o_ref):
    o_ref[...] = w_ref[0].astype(o_ref.dtype)


def _to_bf16(w, layer, rows=256):
    _, r, c = w.shape
    rows = min(rows, r)
    assert r % rows == 0
    return pl.pallas_call(
        _cast_body,
        name="to_bf16",
        grid=(r // rows,),
        in_specs=[pl.BlockSpec((1, rows, c), lambda i: (layer, i, 0))],
        out_specs=pl.BlockSpec((rows, c), lambda i: (i, 0)),
        out_shape=jax.ShapeDtypeStruct((r, c), BF16),
        compiler_params=_params("arbitrary"),
    )(w)


def _ffn_body(*refs, has_mix, oc_transposed, has_final, f_chunk, layer):
    refs = list(refs)
    x_ref = refs.pop(0)
    if has_mix:
        oa_ref, ob_ref, oc_ref, wo_ref = refs[:4]
        refs = refs[4:]
    nw_ref, wg_hbm, wu_hbm, wd_hbm = refs[:4]
    refs = refs[4:]
    if has_final:
        nf_ref = refs.pop(0)
    out_ref, xn_ref, acc_ref, wg_ref, wu_ref, wd_ref, wsem = refs
    i = pl.program_id(0)
    chunks = range(wg_ref.shape[1] // f_chunk)

    def chunk_copies(c):
        cols = pl.ds(c * f_chunk, f_chunk)
        return [pltpu.make_async_copy(wg_hbm.at[layer, :, cols], wg_ref.at[:, cols], wsem.at[0, c]),
                pltpu.make_async_copy(wu_hbm.at[layer, :, cols], wu_ref.at[:, cols], wsem.at[1, c]),
                pltpu.make_async_copy(wd_hbm.at[layer, cols, :], wd_ref.at[cols, :], wsem.at[2, c])]

    @pl.when(i == 0)
    def _():
        for c in chunks:
            for cp in chunk_copies(c):
                cp.start()

    def run(first_step):
        x = x_ref[...]
        if has_mix:
            ga = oa_ref.shape[1]
            gb = ga + ob_ref.shape[1]
            x = x + _dot(oa_ref[...], wo_ref[0:ga, :]) + _dot(ob_ref[...], wo_ref[ga:gb, :])
            if oc_transposed:
                x = x + _dot_tn(oc_ref[0], wo_ref[gb:, :])
            else:
                x = x + _dot(oc_ref[...], wo_ref[gb:, :])
        xn_ref[...] = _rms(x, nw_ref[...]).astype(BF16)
        for c in chunks:
            lo = c * f_chunk
            if first_step:
                for cp in chunk_copies(c):
                    cp.wait()
            xn = xn_ref[...]
            g = _dot(xn, wg_ref[:, lo:lo + f_chunk].astype(BF16))
            u = _dot(xn, wu_ref[:, lo:lo + f_chunk].astype(BF16))
            h = (_silu(g) * u).astype(BF16)
            part = _dot(h, wd_ref[lo:lo + f_chunk, :].astype(BF16))
            if c == 0:
                acc_ref[...] = part
            else:
                acc_ref[...] += part
        y = x + 0.5 * acc_ref[...]
        if has_final:
            y = _rms(y, nf_ref[...])
        out_ref[...] = y

    @pl.when(i == 0)
    def _():
        run(True)

    @pl.when(i != 0)
    def _():
        run(False)


def _ffn(x, nw, wg, wu, wd, layer, mix=None, final_w=None, tm=512, f_chunk=256):
    m, d = x.shape
    d_ff = wg.shape[2]
    hbm = pl.BlockSpec(memory_space=pl.ANY)
    tm = min(tm, m)
    assert m % tm == 0 and d_ff % f_chunk == 0
    row = lambda i: (i, 0)
    in_specs = [pl.BlockSpec((tm, d), row)]
    args = [x]
    oc_transposed = False
    if mix is not None:
        oa, ob, oc, wo = mix
        oc_transposed = oc.ndim == 3
        if oc_transposed:
            tiles = oc.shape[2] // tm
            assert oc.shape[2] % tm == 0
            oc_spec = pl.BlockSpec((1, oc.shape[1], tm), lambda i: (i // tiles, 0, i % tiles))
        else:
            oc_spec = pl.BlockSpec((tm, oc.shape[1]), row)
        in_specs += [pl.BlockSpec((tm, oa.shape[1]), row), pl.BlockSpec((tm, ob.shape[1]), row),
                     oc_spec, _resident(wo.shape)]
        args += [oa, ob, oc, wo]
    in_specs += [_resident((1, d)), hbm, hbm, hbm]
    args += [nw.reshape(1, d), wg, wu, wd]
    if final_w is not None:
        in_specs.append(_resident((1, d)))
        args.append(final_w.reshape(1, d))
    body = functools.partial(_ffn_body, has_mix=mix is not None, oc_transposed=oc_transposed,
                             has_final=final_w is not None, f_chunk=f_chunk, layer=layer)
    return pl.pallas_call(
        body,
        name="ffn_mix" if mix is not None else "ffn",
        grid=(m // tm,),
        in_specs=in_specs,
        out_specs=pl.BlockSpec((tm, d), row),
        out_shape=jax.ShapeDtypeStruct((m, d), F32),
        scratch_shapes=[pltpu.VMEM((tm, d), BF16), pltpu.VMEM((tm, d), F32),
                        pltpu.VMEM(wg.shape[1:], F32), pltpu.VMEM(wu.shape[1:], F32),
                        pltpu.VMEM(wd.shape[1:], F32),
                        pltpu.SemaphoreType.DMA((3, d_ff // f_chunk))],
        compiler_params=_params("arbitrary"),
    )(*args)


_ROW_WIDTHS = (3 * GDN_WIDTH, GDN_WIDTH, 3 * SC_WIDTH, LANES, MOBA_WIDTH)


def _causal_conv(buf, x, w_ref):
    pad = SUBLANES
    n = x.shape[0]
    taps = w_ref.shape[0]
    buf[pad:pad + n, :] = x
    y = None
    for t in range(taps - 1):
        lo = pad - (taps - 1) + t
        term = buf[lo:lo + n, :] * w_ref[t:t + 1, :]
        y = term if y is None else y + term
    y = y + x * w_ref[taps - 1:taps, :]
    tail = buf[n:n + pad, :]
    buf[0:pad, :] = tail
    return y


def _l2_normalize(x):
    return x * lax.rsqrt(jnp.sum(x * x, axis=-1, keepdims=True) + NORM_EPS)


def _gates(ba, gp_ref):
    return jax.nn.sigmoid(ba), -jnp.exp(gp_ref[0:1, :]) * _softplus(ba + gp_ref[1:2, :])


def _inproj_body(x_ref, nw_ref, w_ref, wqv_ref, *out_refs, transposed_qv):
    xn = _rms(x_ref[...], nw_ref[...]).astype(BF16)
    lo = 0
    for ref in out_refs[:len(_ROW_WIDTHS)]:
        width = ref.shape[1]
        ref[...] = _dot(xn, w_ref[:, lo:lo + width])
        lo += width
    q_ref, v_ref = out_refs[len(_ROW_WIDTHS):]
    mw = MOBA_WIDTH
    if transposed_qv:
        q_ref[0] = _dot_nt(wqv_ref[0:mw, :], xn)
        v_ref[0] = _dot_nt(wqv_ref[mw:2 * mw, :], xn)
    else:
        q_ref[...] = _dot(xn, wqv_ref[:, 0:mw])
        v_ref[...] = _dot(xn, wqv_ref[:, mw:2 * mw])


def _inproj(x, nw, w_rows, w_qv, *, seq_len=None, tm=1024):
    m, d = x.shape
    tm = min(tm, m if seq_len is None else seq_len)
    assert m % tm == 0 and w_rows.shape[1] == sum(_ROW_WIDTHS)
    row = lambda i: (i, 0)
    out_specs = [pl.BlockSpec((tm, wd), row) for wd in _ROW_WIDTHS]
    out_shape = [jax.ShapeDtypeStruct((m, wd), F32) for wd in _ROW_WIDTHS]
    transposed_qv = seq_len is not None
    if transposed_qv:
        assert seq_len % tm == 0
        tiles = seq_len // tm
        spec = pl.BlockSpec((1, MOBA_WIDTH, tm), lambda i: (i // tiles, 0, i % tiles))
        shape = jax.ShapeDtypeStruct((m // seq_len, MOBA_WIDTH, seq_len), F32)
    else:
        spec = pl.BlockSpec((tm, MOBA_WIDTH), row)
        shape = jax.ShapeDtypeStruct((m, MOBA_WIDTH), F32)
    out_specs += [spec, spec]
    out_shape += [shape, shape]
    return pl.pallas_call(
        functools.partial(_inproj_body, transposed_qv=transposed_qv),
        name="inproj",
        grid=(m // tm,),
        in_specs=[pl.BlockSpec((tm, d), row), _resident((1, d)), _resident(w_rows.shape),
                  _resident(w_qv.shape)],
        out_specs=out_specs,
        out_shape=out_shape,
        compiler_params=_params("arbitrary"),
    )(x, nw.reshape(1, d), w_rows, w_qv)


def _mixer_body(*refs, c, ng, has_state, row_lo, row_hi):
    refs = list(refs)
    qkv_ref, z_ref, ba_ref, sc_ref = refs[:4]
    refs = refs[4:]
    if has_state:
        qh_ref, sh_ref, s0_ref = refs[:3]
        refs = refs[3:]
    cw_ref, gp_ref, nw_ref, scw_ref = refs[:4]
    refs = refs[5:]
    oa_ref, ob_ref, st_ref, gt_ref = refs[:4]
    xbuf, gbuf, s_ref = refs[4:]
    n = pl.program_id(1)
    hd = GDN_HEAD_DIM
    gr = GROUP_ROWS
    sr = STACK_ROWS
    pad = SUBLANES
    seqs = gr // c
    shift = c.bit_length() - 1
    assert 1 << shift == c

    @pl.when(n == 0)
    def _():
        xbuf[:, 0:pad, :] = jnp.zeros((ng, pad, xbuf.shape[2]), F32)
        gbuf[:, 0:pad, :] = jnp.zeros((ng, pad, gbuf.shape[2]), F32)
        if has_state:
            s_ref[...] = s0_ref[0]
        else:
            s_ref[...] = jnp.zeros(s_ref.shape, F32)

    rows = lax.broadcasted_iota(jnp.int32, (gr, 1), 0) & (c - 1)
    is_hist = rows < row_lo
    live = jnp.where((rows >= row_lo) & (rows < row_hi), 1.0, 0.0)
    ri = lax.broadcasted_iota(jnp.int32, (gr, gr), 0)
    ci = lax.broadcasted_iota(jnp.int32, (gr, gr), 1)
    same = (ri >> shift) == (ci >> shift)
    cum_op = jnp.concatenate([jnp.where(same & (ri >= ci), 1.0, 0.0),
                              jnp.where(same, 1.0, 0.0)], axis=0)
    nh = GDN_HEADS
    tiles = sr // LANES
    band_tile = [h * gr // LANES for h in range(nh)]

    def bands(mat):
        return [mat[h * gr:(h + 1) * gr, band_tile[h] * LANES:(band_tile[h] + 1) * LANES]
                for h in range(nh)]

    def unband(pieces):
        zero = jnp.zeros((gr, LANES), BF16)
        return jnp.concatenate(
            [jnp.concatenate([p.astype(BF16) if t == band_tile[h] else zero
                              for t in range(tiles)], axis=1)
             for h, p in enumerate(pieces)], axis=0)

    incl, strict, eye = [], [], []
    for h in range(nh):
        rs = lax.broadcasted_iota(jnp.int32, (gr, LANES), 0) + h * gr
        cs = lax.broadcasted_iota(jnp.int32, (gr, LANES), 1) + band_tile[h] * LANES
        same_s = (rs >> shift) == (cs >> shift)
        incl.append(same_s & (rs >= cs))
        strict.append(same_s & (rs > cs))
        eye.append(jnp.where(rs == cs, 1.0, 0.0))

    def stack_heads(a, lo):
        return jnp.concatenate([a[:, lo + h * hd:lo + (h + 1) * hd] for h in range(GDN_HEADS)],
                               axis=0)

    def stack_cols(a, lo):
        return jnp.concatenate([a[:, lo + h:lo + h + 1] for h in range(GDN_HEADS)], axis=0)

    groups = range(ng)

    def front(g):
        x = qkv_ref[g]
        if has_state:
            x = jnp.where(is_hist, qh_ref[g], x)
        act = _silu(_causal_conv(xbuf.at[g], x, cw_ref))
        q = _l2_normalize(stack_heads(act, 0)) * (hd ** -0.5)
        k = _l2_normalize(stack_heads(act, GDN_WIDTH))
        v = stack_heads(act, 2 * GDN_WIDTH)

        sc = sc_ref[g]
        scw = SC_WIDTH
        gated = sc[:, scw:2 * scw] * sc[:, 0:scw]
        if has_state:
            gated = jnp.where(is_hist, sh_ref[g], gated)
        yb = _causal_conv(gbuf.at[g], gated, scw_ref)
        ob_ref[g] = (sc[:, 2 * scw:3 * scw] * yb).astype(ob_ref.dtype)
        gt_ref[g] = gated

        beta_all, g_all = _gates(ba_ref[g], gp_ref)
        beta_all, g_all = beta_all * live, g_all * live
        cum = _dot(cum_op, g_all, HI)
        gc_all, glast_all = cum[:gr], cum[gr:]
        beta = stack_cols(beta_all, 0)
        gc = stack_cols(gc_all, nh)
        eg = jnp.exp(gc)
        etail = jnp.exp(stack_cols(glast_all, nh) - gc)
        gc_row = jnp.broadcast_to(gc, (sr, LANES)).T[0:1, :]
        decay = []
        for h in range(nh):
            diff = (gc[h * gr:(h + 1) * gr]
                    - gc_row[:, band_tile[h] * LANES:(band_tile[h] + 1) * LANES])
            decay.append(jnp.where(incl[h], jnp.exp(jnp.where(incl[h], diff, 0.0)), 0.0))

        kb = k.astype(BF16)
        kk = bands(_dot_nt(kb, kb))
        m_strict = [jnp.where(strict[h], beta[h * gr:(h + 1) * gr] * kk[h] * decay[h], 0.0)
                    for h in range(nh)]
        return dict(q=q, k=k, v=v, kb=kb, beta=beta, eg=eg, etail=etail, decay=decay,
                    cd_all=jnp.exp(glast_all), m_strict=m_strict)

    fr = [front(g) for g in groups]

    invs = [[e - m for e, m in zip(eye, f["m_strict"])] for f in fr]
    pbs = [unband(f["m_strict"]) for f in fr]
    for _ in range(shift - 1):
        pbs = [unband(bands(_dot(pb, pb))) for pb in pbs]
        invs = [[i + d for i, d in zip(inv, bands(_dot(unband(inv), pb)))]
                for inv, pb in zip(invs, pbs)]

    def solve(f, inv):
        rhs = jnp.concatenate([f["k"] * (f["beta"] * f["eg"]), f["v"] * f["beta"]],
                              axis=-1).astype(BF16)
        sol = _dot(unband(inv), rhs)
        qk = bands(_dot_nt(f["q"].astype(BF16), f["kb"]))
        attn = unband([a * d for a, d in zip(qk, f["decay"])])
        return dict(w_k=sol[:, :hd], u_v=sol[:, hd:], attn=attn, qd=f["q"] * f["eg"],
                    kt=(f["k"] * f["etail"]).astype(BF16))

    sv = [solve(f, inv) for f, inv in zip(fr, invs)]

    pair_ids = [(h, s) for h in range(GDN_HEADS) for s in range(seqs)]

    def read_state(g, t):
        parts = []
        for h, s in pair_ids:
            r0 = h * gr + s * c
            s_old = s_ref[g, s * nh + h]
            lhs = jnp.concatenate([t["w_k"][r0:r0 + c], t["qd"][r0:r0 + c]], axis=0)
            res = _dot(lhs.astype(BF16), s_old.astype(BF16))
            parts.append((s_old, t["u_v"][r0:r0 + c] - res[:c], res[c:]))
        return parts

    rd = [read_state(g, t) for g, t in zip(groups, sv)]
    us = [jnp.concatenate([p[1] for p in parts], axis=0) for parts in rd]
    outs = [jnp.concatenate([p[2] for p in parts], axis=0) + _dot(t["attn"], u.astype(BF16))
            for parts, t, u in zip(rd, sv, us)]
    for g in groups:
        for (h, s), (s_old, u_p, _) in zip(pair_ids, rd[g]):
            r0 = h * gr + s * c
            cd = fr[g]["cd_all"][s * c:s * c + 1, nh + h:nh + h + 1]
            s_ref[g, s * nh + h] = s_old * cd + _dot_tn(sv[g]["kt"][r0:r0 + c],
                                                        u_p.astype(BF16))
    for g in groups:
        zs = stack_heads(z_ref[g], 0)
        o_n = _rms(outs[g], nw_ref[...]) * _silu(zs)
        oa_ref[g] = jnp.concatenate([o_n[h * gr:(h + 1) * gr] for h in range(GDN_HEADS)],
                                    axis=1).astype(oa_ref.dtype)

    @pl.when(n == pl.num_programs(1) - 1)
    def _():
        st_ref[0] = s_ref[...]


def _mixer(qkv, z, ba, sc, conv_w, gparams, norm_w, sc_conv_w, *, c, ng, layer, depth,
           prev_states=None, state=None, row_lo=0, row_hi=None):
    g_total, l, _ = qkv.shape
    gr = GROUP_ROWS
    assert l % gr == 0 and gr % c == 0 and g_total % ng == 0
    nc = l // gr
    pairs = GDN_HEADS * (gr // c)
    row_hi = c if row_hi is None else row_hi
    blk = lambda w: pl.BlockSpec((ng, gr, w), lambda i, n: (i, n, 0))
    in_specs = [blk(qkv.shape[2]), blk(z.shape[2]), blk(ba.shape[2]), blk(sc.shape[2])]
    args = [qkv, z, ba, sc]
    st_shape = (g_total, pairs, GDN_HEAD_DIM, GDN_HEAD_DIM)
    st_spec = pl.BlockSpec((1, ng) + st_shape[1:], lambda i, n: (layer, i, 0, 0, 0))
    if state is not None:
        assert nc == 1
        qh, sh, s0 = state
        assert s0.shape[1:] == st_shape
        in_specs += [pl.BlockSpec((ng, gr, qh.shape[2]), lambda i, n: (i, 0, 0)),
                     pl.BlockSpec((ng, gr, sh.shape[2]), lambda i, n: (i, 0, 0)), st_spec]
        args += [qh, sh, s0]
    const = lambda a: pl.BlockSpec(a.shape, lambda i, n: (0,) * a.ndim)
    norm_w = norm_w.reshape(1, -1)
    in_specs += [const(conv_w), const(gparams), const(norm_w), const(sc_conv_w)]
    args += [conv_w, gparams, norm_w, sc_conv_w]
    if prev_states is None:
        prev_states = jnp.zeros((depth,) + st_shape, F32)
    assert prev_states.shape == (depth,) + st_shape
    aliases = {len(args): 2}
    in_specs.append(pl.BlockSpec(memory_space=pl.ANY))
    args.append(prev_states)
    out_shape = [jax.ShapeDtypeStruct((g_total, l, GDN_WIDTH), BF16),
                 jax.ShapeDtypeStruct((g_total, l, SC_WIDTH), BF16),
                 jax.ShapeDtypeStruct((depth,) + st_shape, F32),
                 jax.ShapeDtypeStruct((g_total, l, SC_WIDTH), F32)]
    out_specs = [blk(GDN_WIDTH), blk(SC_WIDTH), st_spec, blk(SC_WIDTH)]
    body = functools.partial(_mixer_body, c=c, ng=ng, has_state=state is not None,
                             row_lo=row_lo, row_hi=row_hi)
    return pl.pallas_call(
        body,
        name="seq_mixer",
        grid=(g_total // ng, nc),
        in_specs=in_specs,
        out_specs=out_specs,
        out_shape=out_shape,
        input_output_aliases=aliases,
        scratch_shapes=[pltpu.VMEM((ng, gr + SUBLANES, qkv.shape[2]), F32),
                        pltpu.VMEM((ng, gr + SUBLANES, SC_WIDTH), F32),
                        pltpu.VMEM((ng,) + st_shape[1:], F32)],
        compiler_params=_params("arbitrary", "arbitrary"),
    )(*args)


def _select_blocks(gate, n_valid, axis=0):
    nb = gate.shape[axis]
    sub = lax.broadcasted_iota(jnp.int32, gate.shape, axis)
    gate = jnp.where(sub < n_valid, gate, NEG_INF)
    sel = jnp.zeros(gate.shape, F32)
    for j in range(nb):
        gj = gate[j:j + 1, :] if axis == 0 else gate[:, j:j + 1]
        beats = (gate > gj) | ((gate == gj) & (sub < j))
        cnt = jnp.sum(jnp.where(beats, 1.0, 0.0), axis=axis, keepdims=True)
        hit = (cnt < MOBA_TOPK) & (j < n_valid)
        sel = jnp.where((sub == j) & hit, 1.0, sel)
    return sel


_PV_ROWS = MOBA_HEAD_DIM + 16


def _moba_prompt_body(q_ref, k_ref, v_ref, o_ref, kmean_ref, kb_ref, vb_ref, sel_ref, *, nbat):
    i = pl.program_id(1)
    blk = MOBA_BLOCK
    hd = MOBA_HEAD_DIM
    w = MOBA_WIDTH
    l_seq = v_ref.shape[2]
    nb = l_seq // blk
    heads = range(MOBA_HEADS)
    head_rows = [slice(h * hd, (h + 1) * hd) for h in heads]
    chains = [(b, h) for b in range(nbat) for h in heads]

    @pl.when(i == 0)
    def _():
        ones = jnp.ones((_PV_ROWS - hd, blk), BF16)
        for b in range(nbat):
            for j in range(nb):
                kj = k_ref[b * l_seq + j * blk:b * l_seq + (j + 1) * blk, :]
                kmean_ref[b, j:j + 1, :] = jnp.mean(kj, axis=0, keepdims=True)
                kb_ref[b, j] = kj.astype(BF16)
                for h in heads:
                    vb_ref[b, j, h, 0:hd, :] = v_ref[b, head_rows[h],
                                                     j * blk:(j + 1) * blk].astype(BF16)
                    vb_ref[b, j, h, hd:_PV_ROWS, :] = ones

    feat = lax.broadcasted_iota(jnp.int32, (w, blk), 0)
    key_i = lax.broadcasted_iota(jnp.int32, (blk, blk), 0)
    qry_i = lax.broadcasted_iota(jnp.int32, (blk, blk), 1)
    causal = key_i <= qry_i
    qmb = []
    for b in range(nbat):
        qt = q_ref[b] * (hd ** -0.5)
        qms = [jnp.where((feat >= h * hd) & (feat < (h + 1) * hd), qt, 0.0) for h in heads]
        qmb.append(jnp.concatenate(qms, axis=1).astype(BF16))
        for h in heads:
            gate = _dot(kmean_ref[b], qms[h], HI)
            sel_ref[b, h] = _select_blocks(gate, i)

    def scores(j):
        s_all = [_dot(kb_ref[b, j], qmb[b]) for b in range(nbat)]
        return [s_all[b][:, h * blk:(h + 1) * blk] for b, h in chains]

    def attend(j, s_list, m_list):
        ps = [jnp.exp(s - m).astype(BF16) for s, m in zip(s_list, m_list)]
        return [_dot(vb_ref[b, j, h], p) for (b, h), p in zip(chains, ps)]

    s0 = [jnp.where(causal, s, NEG_INF) for s in scores(i)]
    m0 = [jnp.max(s, axis=0, keepdims=True) for s in s0]
    pv0 = attend(i, s0, m0)
    init = []
    for m, pv in zip(m0, pv0):
        init += [m, pv[hd:hd + 1], pv[:hd]]

    def past_block(j, carry):
        s_list = [jnp.where(sel_ref[b, h, pl.ds(j, 1), :] > 0.5, s, NEG_INF)
                  for (b, h), s in zip(chains, scores(j))]
        m_old = carry[0::3]
        m_new = [jnp.maximum(m, jnp.max(s, axis=0, keepdims=True))
                 for m, s in zip(m_old, s_list)]
        pv = attend(j, s_list, m_new)
        new = []
        for c in range(len(chains)):
            alpha = jnp.exp(m_old[c] - m_new[c])
            new += [m_new[c], alpha * carry[3 * c + 1] + pv[c][hd:hd + 1],
                    alpha * carry[3 * c + 2] + pv[c][:hd]]
        return tuple(new)

    final = lax.fori_loop(0, i, past_block, tuple(init))
    for c, (b, h) in enumerate(chains):
        o_ref[b, head_rows[h], :] = (final[3 * c + 2] / final[3 * c + 1]).astype(o_ref.dtype)


def _moba_prompt(qt, k, vt, nbat=4):
    b, w, l = qt.shape
    blk = MOBA_BLOCK
    nbat = nbat if b % nbat == 0 else 1
    assert l % blk == 0
    nb = l // blk
    return pl.pallas_call(
        functools.partial(_moba_prompt_body, nbat=nbat),
        name="moba_prompt",
        grid=(b // nbat, nb),
        in_specs=[pl.BlockSpec((nbat, w, blk), lambda i, t: (i, 0, t)),
                  pl.BlockSpec((nbat * l, w), lambda i, t: (i, 0)),
                  pl.BlockSpec((nbat, w, l), lambda i, t: (i, 0, 0))],
        out_specs=pl.BlockSpec((nbat, w, blk), lambda i, t: (i, 0, t)),
        out_shape=jax.ShapeDtypeStruct((b, w, l), BF16),
        scratch_shapes=[pltpu.VMEM((nbat, nb, w), F32), pltpu.VMEM((nbat, nb, blk, w), BF16),
                        pltpu.VMEM((nbat, nb, MOBA_HEADS, _PV_ROWS, blk), BF16),
                        pltpu.VMEM((nbat, MOBA_HEADS, nb, blk), F32)],
        compiler_params=_params("arbitrary", "arbitrary"),
    )(qt, k, vt)


PAGE_SLOTS = 3
SEQS_PER_STEP = 2


def _moba_sample_body(pt_ref, q_ref, kn_ref, vn_ref, ck_ref, cv_ref, o_ref, kbuf, vbuf, sem,
                      *, n_pages, base, n_steps, ns):
    i = pl.program_id(0)
    hd = MOBA_HEAD_DIM
    r = S_ROWS
    w = MOBA_WIDTH
    ppb = MOBA_BLOCK // PAGE_SIZE
    nb = n_pages // ppb
    ahead = PAGE_SLOTS - 1
    seqs = range(ns)
    blocks = range(nb)

    def page_copies(step, slot):
        cps = []
        for s in seqs:
            for p in range(n_pages):
                row = base + pt_ref[step * ns + s, p]
                cps.append(pltpu.make_async_copy(ck_ref.at[row], kbuf.at[slot, s, p],
                                                 sem.at[0, slot]))
                cps.append(pltpu.make_async_copy(cv_ref.at[row], vbuf.at[slot, s, p],
                                                 sem.at[1, slot]))
        return cps

    def start_all(cps):
        for n, cp in enumerate(cps):
            cp.start(priority=n % 2)

    @pl.when(i == 0)
    def _():
        for d in range(ahead):
            start_all(page_copies(d, d))

    slot = lax.rem(i, PAGE_SLOTS)
    for cp in page_copies(i, slot):
        cp.wait()

    lane = lax.broadcasted_iota(jnp.int32, (r, w), 1)
    head_masks = [jnp.where((lane >= h * hd) & (lane < (h + 1) * hd), 1.0, 0.0)
                  for h in range(MOBA_HEADS)]
    col = lax.broadcasted_iota(jnp.int32, (w, LANES), 1)
    q_bd = [jnp.concatenate([q_ref[s] * (hd ** -0.5) * hm for hm in head_masks], axis=0)
            for s in seqs]
    q_bb = [x.astype(BF16) for x in q_bd]

    kp = [[kbuf[slot, s, p] for p in range(n_pages)] for s in seqs]
    s_pages = [[_dot(q_bb[s], x.astype(BF16)) for x in kp[s]] for s in seqs]
    kmean_t = []
    for s in seqs:
        km = jnp.zeros((w, LANES), F32)
        for j in blocks:
            ksum = kp[s][j * ppb]
            for pp in range(1, ppb):
                ksum = ksum + kp[s][j * ppb + pp]
            km = jnp.where(col == j, jnp.sum(ksum, axis=1, keepdims=True) * (1.0 / MOBA_BLOCK),
                           km)
        kmean_t.append(km)
    s_blk = [[jnp.concatenate(s_pages[s][j * ppb:(j + 1) * ppb], axis=-1) for j in blocks]
             for s in seqs]
    m_cols = [[jnp.max(x, axis=1, keepdims=True) for x in s_blk[s]] for s in seqs]
    e_blk = [[jnp.exp(x - m) for x, m in zip(s_blk[s], m_cols[s])] for s in seqs]
    l_cols = [[jnp.sum(e, axis=1, keepdims=True) for e in e_blk[s]] for s in seqs]
    accs = []
    for s in seqs:
        row_accs = []
        for j in blocks:
            eb = e_blk[s][j].astype(BF16)
            acc = None
            for pp in range(ppb):
                vp = vbuf[slot, s, j * ppb + pp].astype(BF16)
                part = _dot_nt(eb[:, pp * PAGE_SIZE:(pp + 1) * PAGE_SIZE], vp)
                acc = part if acc is None else acc + part
            row_accs.append(acc)
        accs.append(row_accs)

    nxt = jnp.minimum(i + ahead, n_steps - 1)
    start_all(page_copies(nxt, lax.rem(i + ahead, PAGE_SLOTS)))

    gates = [_dot(q_bd[s], kmean_t[s], HI)[:, :nb] for s in seqs]
    sels = [_select_blocks(g, nb, axis=1) > 0.5 for g in gates]
    qt = lax.broadcasted_iota(jnp.int32, (MOBA_HEADS * r, r), 0) & (r - 1)
    kt = lax.broadcasted_iota(jnp.int32, (MOBA_HEADS * r, r), 1)
    own = (kt >= S_LO) & (kt <= qt)
    for s in seqs:
        sel = sels[s]
        m_all = jnp.concatenate(m_cols[s], axis=1)
        l_all = jnp.concatenate(l_cols[s], axis=1)
        s_own = _dot_nt(q_bb[s], kn_ref[s].astype(BF16))
        m_tot = jnp.maximum(jnp.max(jnp.where(sel, m_all, NEG_INF), axis=1, keepdims=True),
                            jnp.max(jnp.where(own, s_own, NEG_INF), axis=1, keepdims=True))
        wj = jnp.where(sel, jnp.exp(jnp.where(sel, m_all - m_tot, 0.0)), 0.0)
        p_own = jnp.where(own, jnp.exp(jnp.where(own, s_own - m_tot, 0.0)), 0.0)
        l_tot = (jnp.sum(wj * l_all, axis=1, keepdims=True)
                 + jnp.sum(p_own, axis=1, keepdims=True))
        acc = _dot(p_own.astype(BF16), vn_ref[s].astype(BF16))
        for j in blocks:
            acc = acc + wj[:, j:j + 1] * accs[s][j]
        o_bd = acc / l_tot
        o = None
        for h in range(MOBA_HEADS):
            part = o_bd[h * r:(h + 1) * r, :] * head_masks[h]
            o = part if o is None else o + part
        o_ref[s] = o.astype(o_ref.dtype)

    @pl.when(i == n_steps - 1)
    def _():
        for d in range(1, PAGE_SLOTS):
            for cp in page_copies(n_steps - 1, lax.rem(i + d, PAGE_SLOTS)):
                cp.wait()


def _moba_sample(q, k_new, v_new, cache_kt, cache_vt, page_table, base):
    b, r, w = q.shape
    page = cache_kt.shape[2]
    n_pages = page_table.shape[1]
    ns = SEQS_PER_STEP if b % SEQS_PER_STEP == 0 else 1
    n_steps = b // ns
    assert page == PAGE_SIZE and r == S_ROWS and cache_kt.shape[1] == w
    assert n_steps >= PAGE_SLOTS
    tile = pl.BlockSpec((ns, r, w), lambda i, pt: (i, 0, 0))
    hbm = pl.BlockSpec(memory_space=pl.ANY)
    grid_spec = pltpu.PrefetchScalarGridSpec(
        num_scalar_prefetch=1,
        grid=(n_steps,),
        in_specs=[tile, tile, tile, hbm, hbm],
        out_specs=tile,
        scratch_shapes=[pltpu.VMEM((PAGE_SLOTS, ns, n_pages, w, page), F32),
                        pltpu.VMEM((PAGE_SLOTS, ns, n_pages, w, page), F32),
                        pltpu.SemaphoreType.DMA((2, PAGE_SLOTS))],
    )
    body = functools.partial(_moba_sample_body, n_pages=n_pages, base=base, n_steps=n_steps,
                             ns=ns)
    return pl.pallas_call(
        body,
        name="moba_sample",
        grid_spec=grid_spec,
        out_shape=jax.ShapeDtypeStruct((b, r, w), BF16),
        compiler_params=_params("arbitrary"),
    )(page_table, q, k_new, v_new, cache_kt, cache_vt)


def _trunk(x, weights, ffn_w, norm_final, *, sample=None):
    b, l, d = x.shape
    depth = len(weights)
    x = x.reshape(b * l, d)
    gr = GROUP_ROWS
    states = []
    gdn_states = None
    for layer, wts in enumerate(weights):
        x = _ffn(x, wts["n_ffn1"], *ffn_w[0], layer)
        if sample is None:
            qkv, z, sc, ba, mk, mq, mv = _inproj(x, wts["n_mix"], wts["w_rows"], wts["w_qv_t"],
                                                 seq_len=l)
            grp = lambda a: a.reshape(b, l, a.shape[-1])
            o_a, o_b, gdn_states, gated = _mixer(
                grp(qkv), grp(z), grp(ba), grp(sc), wts["gdn_conv_w"], wts["gparams"],
                wts["gdn_norm_w"], wts["sc_conv_w"], c=GDN_CHUNK, ng=8 if b % 8 == 0 else 1,
                layer=layer, depth=depth, prev_states=gdn_states)
            o_c = _moba_prompt(mq, mk, mv)
        else:
            qkv, z, sc, ba, mk, mq, mv = _inproj(x, wts["n_mix"], wts["w_rows"], wts["w_qv"])
            ngroups = b * l // gr
            grp = lambda a: a.reshape(ngroups, gr, a.shape[-1])
            state = (grp(sample["gdn_conv"][layer]), grp(sample["sconv"][layer]), sample["gdn"])
            o_a, o_b, gdn_states, gated = _mixer(
                grp(qkv), grp(z), grp(ba), grp(sc), wts["gdn_conv_w"], wts["gparams"],
                wts["gdn_norm_w"], wts["sc_conv_w"], c=l, ng=2 if ngroups % 2 == 0 else 1,
                layer=layer, depth=depth, prev_states=gdn_states, state=state,
                row_lo=S_LO, row_hi=S_HI)
            seq = lambda a: a.reshape(b, l, a.shape[-1])
            o_c = _moba_sample(seq(mq), seq(mk), seq(mv), sample["cache_kt"], sample["cache_vt"],
                               sample["page_table"], layer * sample["n_pool"])
            o_c = o_c.reshape(b * l, -1)
        mix = (o_a.reshape(b * l, -1), o_b.reshape(b * l, -1), o_c, wts["w_out"])
        x = _ffn(x, wts["n_ffn2"], *ffn_w[1], layer, mix=mix,
                 final_w=norm_final if layer == depth - 1 else None)
        states.append(dict(qkv=qkv.reshape(b, l, -1), gated=gated.reshape(b, l, -1),
                           k=mk.reshape(b, l, -1), v=mv))
    return x.reshape(b, l, d), states, gdn_states


def kernel(x_prompt, x_sample, state_gdn, state_gdn_conv, state_sconv, cache_k, cache_v,
           page_table, norm_ffn1, ffn1_w_gate, ffn1_w_up, ffn1_w_down, norm_mix, w_in,
           gdn_conv_w, gdn_a_log, gdn_dt_bias, gdn_norm_w, sc_conv_w, w_out,
           norm_ffn2, ffn2_w_gate, ffn2_w_up, ffn2_w_down, norm_final):
    depth = w_in.shape[0]
    gw, sw, mw = GDN_WIDTH, SC_WIDTH, MOBA_WIDTH
    nh = GDN_HEADS
    mh, md = MOBA_HEADS, MOBA_HEAD_DIM

    weights = []
    for la in range(depth):
        w = w_in[la]
        ba_off = 4 * gw
        sc_off = ba_off + 2 * nh
        mo_off = sc_off + 3 * sw
        w_rows = jnp.concatenate(
            [w[:, :ba_off], w[:, sc_off:mo_off],
             jnp.pad(w[:, ba_off:sc_off], ((0, 0), (0, LANES - 2 * nh))),
             w[:, mo_off + mw:mo_off + 2 * mw]], axis=1).astype(BF16)
        w_qv = jnp.concatenate([w[:, mo_off:mo_off + mw], w[:, mo_off + 2 * mw:mo_off + 3 * mw]],
                               axis=1).astype(BF16)
        gparams = jnp.zeros((SUBLANES, LANES), F32)
        gparams = gparams.at[0, nh:2 * nh].set(gdn_a_log[la].astype(F32))
        gparams = gparams.at[1, nh:2 * nh].set(gdn_dt_bias[la].astype(F32))
        weights.append(dict(
            n_ffn1=norm_ffn1[la],
            n_mix=norm_mix[la], w_rows=w_rows, w_qv=w_qv, w_qv_t=w_qv.T,
            gdn_conv_w=gdn_conv_w[la], gparams=gparams, gdn_norm_w=gdn_norm_w[la],
            sc_conv_w=sc_conv_w[la], w_out=_to_bf16(w_out, la),
            n_ffn2=norm_ffn2[la]))
    ffn_w = ((ffn1_w_gate, ffn1_w_up, ffn1_w_down), (ffn2_w_gate, ffn2_w_up, ffn2_w_down))

    pb, pl_, _ = x_prompt.shape
    y_p, st_p, p_gdn = _trunk(x_prompt, weights, ffn_w, norm_final)
    hist = GDN_CONV - 1
    p_conv = jnp.stack([s["qkv"][:, -hist:] for s in st_p])
    p_sconv = jnp.stack([s["gated"][:, -(SC_CONV - 1):] for s in st_p])
    p_k = jnp.stack([s["k"] for s in st_p]).reshape(depth, pb, pl_, mh, md)
    p_v = jnp.stack([s["v"] for s in st_p]).reshape(depth, pb, mh, md, pl_)
    p_v = p_v.transpose(0, 1, 4, 2, 3)

    db, dl, d = x_sample.shape
    assert dl == S_HI - S_LO and (db * S_ROWS) % GROUP_ROWS == 0
    n_pool = cache_k.shape[1]
    tile_pad = lambda a, lo: jnp.pad(a, ((0, 0),) * (a.ndim - 2)
                                     + ((lo, S_ROWS - lo - a.shape[-2]), (0, 0)))
    seqs_per_group = GROUP_ROWS // S_ROWS
    paged_t = lambda c: c.transpose(0, 1, 3, 4, 2).reshape(depth * n_pool, mw, PAGE_SIZE)
    sample = dict(
        gdn=state_gdn.reshape(depth, db // seqs_per_group, seqs_per_group * nh,
                              GDN_HEAD_DIM, GDN_HEAD_DIM),
        gdn_conv=tile_pad(state_gdn_conv, 0),
        sconv=tile_pad(state_sconv, S_LO - (SC_CONV - 1)),
        cache_kt=paged_t(cache_k), cache_vt=paged_t(cache_v), n_pool=n_pool,
        page_table=page_table)
    y_s, st_s, s_gdn = _trunk(tile_pad(x_sample, S_LO), weights, ffn_w, norm_final,
                              sample=sample)
    y_s = y_s[:, S_LO:S_HI]
    s_gdn = s_gdn.reshape(state_gdn.shape)
    s_conv = jnp.stack([s["qkv"][:, S_HI - hist:S_HI] for s in st_s])
    s_sconv = jnp.stack([s["gated"][:, S_HI - (SC_CONV - 1):S_HI] for s in st_s])
    s_k = jnp.stack([s["k"][:, S_LO:S_HI] for s in st_s]).reshape(depth, db, dl, mh, md)
    s_v = jnp.stack([s["v"].reshape(db, S_ROWS, mw)[:, S_LO:S_HI] for s in st_s]).reshape(
        depth, db, dl, mh, md)
    return (y_p, y_s, p_gdn, p_conv, p_sconv, p_k, p_v, s_gdn, s_conv, s_sconv, s_k, s_v)
```

```python
import functools

import jax
import jax.numpy as jnp
from jax import lax
from jax.experimental import pallas as pl
from jax.experimental.pallas import tpu as pltpu

NORM_EPS = 1e-6
NEG_INF = -1e30

GDN_HEADS = 4
GDN_HEAD_DIM = 128
GDN_WIDTH = GDN_HEADS * GDN_HEAD_DIM
GDN_CONV = 4
GDN_CHUNK = 64
SC_WIDTH = 256
SC_CONV = 3
MOBA_HEADS = 4
MOBA_HEAD_DIM = 64
MOBA_WIDTH = MOBA_HEADS * MOBA_HEAD_DIM
MOBA_BLOCK = 256
MOBA_TOPK = 3
PAGE_SIZE = 128

LANES = 128
SUBLANES = 8
VMEM_LIMIT_BYTES = 56 * 1024 * 1024

S_ROWS = SUBLANES
S_LO = GDN_CONV - 1
S_HI = S_LO + 4

GROUP_ROWS = GDN_CHUNK
STACK_ROWS = GDN_HEADS * GROUP_ROWS

F32 = jnp.float32
BF16 = jnp.bfloat16
HI = lax.Precision.HIGHEST


def _dot(a, b, precision=None):
    return jnp.dot(a, b, preferred_element_type=F32, precision=precision)


def _dot_nt(a, b, precision=None):
    return lax.dot_general(a, b, (((1,), (1,)), ((), ())),
                           preferred_element_type=F32, precision=precision)


def _dot_tn(a, b, precision=None):
    return lax.dot_general(a, b, (((0,), (0,)), ((), ())),
                           preferred_element_type=F32, precision=precision)


def _rms(x, w):
    return x * lax.rsqrt(jnp.mean(x * x, axis=-1, keepdims=True) + NORM_EPS) * w


def _silu(x):
    return x * jax.nn.sigmoid(x)


def _softplus(x):
    return jnp.maximum(x, 0.0) + jnp.log1p(jnp.exp(-jnp.abs(x)))


def _resident(shape):
    nd = len(shape)
    return pl.BlockSpec(shape, lambda *_: (0,) * nd, pipeline_mode=pl.Buffered(1))


def _params(*semantics):
    return pltpu.CompilerParams(dimension_semantics=semantics,
                                vmem_limit_bytes=VMEM_LIMIT_BYTES)


def _ffn_body(*refs, has_mix, oc_transposed, has_final, f_chunk, layer):
    refs = list(refs)
    x_ref = refs.pop(0)
    if has_mix:
        oa_ref, ob_ref, oc_ref, wo_ref = refs[:4]
        refs = refs[4:]
    nw_ref, wg_hbm, wu_hbm, wd_hbm = refs[:4]
    refs = refs[4:]
    if has_final:
        nf_ref = refs.pop(0)
    out_ref, xn_ref, acc_ref, wg_ref, wu_ref, wd_ref, wsem = refs
    i = pl.program_id(0)
    chunks = range(wg_ref.shape[1] // f_chunk)

    def chunk_copies(c):
        cols = pl.ds(c * f_chunk, f_chunk)
        return [pltpu.make_async_copy(wg_hbm.at[layer, :, cols], wg_ref.at[:, cols], wsem.at[0, c]),
                pltpu.make_async_copy(wu_hbm.at[layer, :, cols], wu_ref.at[:, cols], wsem.at[1, c]),
                pltpu.make_async_copy(wd_hbm.at[layer, cols, :], wd_ref.at[cols, :], wsem.at[2, c])]

    @pl.when(i == 0)
    def _():
        for c in chunks:
            for cp in chunk_copies(c):
                cp.start()

    def run(first_step):
        x = x_ref[...]
        if has_mix:
            ga = oa_ref.shape[1]
            gb = ga + ob_ref.shape[1]
            wo = lambda lo, hi: wo_ref[0, lo:hi, :].astype(BF16)
            x = x + _dot(oa_ref[...], wo(0, ga)) + _dot(ob_ref[...], wo(ga, gb))
            if oc_transposed:
                x = x + _dot_tn(oc_ref[0], wo(gb, wo_ref.shape[1]))
            else:
                x = x + _dot(oc_ref[...], wo(gb, wo_ref.shape[1]))
        xn_ref[...] = _rms(x, nw_ref[...]).astype(BF16)
        for c in chunks:
            lo = c * f_chunk
            if first_step:
                for cp in chunk_copies(c):
                    cp.wait()
            xn = xn_ref[...]
            g = _dot(xn, wg_ref[:, lo:lo + f_chunk].astype(BF16))
            u = _dot(xn, wu_ref[:, lo:lo + f_chunk].astype(BF16))
            h = (_silu(g) * u).astype(BF16)
            part = _dot(h, wd_ref[lo:lo + f_chunk, :].astype(BF16))
            if c == 0:
                acc_ref[...] = part
            else:
                acc_ref[...] += part
        y = x + 0.5 * acc_ref[...]
        if has_final:
            y = _rms(y, nf_ref[...])
        out_ref[...] = y

    @pl.when(i == 0)
    def _():
        run(True)

    @pl.when(i != 0)
    def _():
        run(False)


def _ffn(x, nw, wg, wu, wd, layer, mix=None, final_w=None, tm=512, f_chunk=256):
    m, d = x.shape
    d_ff = wg.shape[2]
    hbm = pl.BlockSpec(memory_space=pl.ANY)
    tm = min(tm, m)
    assert m % tm == 0 and d_ff % f_chunk == 0
    row = lambda i: (i, 0)
    in_specs = [pl.BlockSpec((tm, d), row)]
    args = [x]
    oc_transposed = False
    if mix is not None:
        oa, ob, oc, wo = mix
        oc_transposed = oc.ndim == 3
        if oc_transposed:
            tiles = oc.shape[2] // tm
            assert oc.shape[2] % tm == 0
            oc_spec = pl.BlockSpec((1, oc.shape[1], tm), lambda i: (i // tiles, 0, i % tiles))
        else:
            oc_spec = pl.BlockSpec((tm, oc.shape[1]), row)
        in_specs += [pl.BlockSpec((tm, oa.shape[1]), row), pl.BlockSpec((tm, ob.shape[1]), row),
                     oc_spec, pl.BlockSpec((1,) + wo.shape[1:], lambda *_: (layer, 0, 0),
                                           pipeline_mode=pl.Buffered(1))]
        args += [oa, ob, oc, wo]
    in_specs += [_resident((1, d)), hbm, hbm, hbm]
    args += [nw.reshape(1, d), wg, wu, wd]
    if final_w is not None:
        in_specs.append(_resident((1, d)))
        args.append(final_w.reshape(1, d))
    body = functools.partial(_ffn_body, has_mix=mix is not None, oc_transposed=oc_transposed,
                             has_final=final_w is not None, f_chunk=f_chunk, layer=layer)
    return pl.pallas_call(
        body,
        name="ffn_mix" if mix is not None else "ffn",
        grid=(m // tm,),
        in_specs=in_specs,
        out_specs=pl.BlockSpec((tm, d), row),
        out_shape=jax.ShapeDtypeStruct((m, d), F32),
        scratch_shapes=[pltpu.VMEM((tm, d), BF16), pltpu.VMEM((tm, d), F32),
                        pltpu.VMEM(wg.shape[1:], F32), pltpu.VMEM(wu.shape[1:], F32),
                        pltpu.VMEM(wd.shape[1:], F32),
                        pltpu.SemaphoreType.DMA((3, d_ff // f_chunk))],
        compiler_params=_params("arbitrary"),
    )(*args)


_ROW_WIDTHS = (3 * GDN_WIDTH, GDN_WIDTH, 3 * SC_WIDTH, LANES, MOBA_WIDTH)


def _causal_conv(buf, x, w_ref):
    pad = SUBLANES
    n = x.shape[0]
    taps = w_ref.shape[0]
    buf[pad:pad + n, :] = x
    y = None
    for t in range(taps - 1):
        lo = pad - (taps - 1) + t
        term = buf[lo:lo + n, :] * w_ref[t:t + 1, :]
        y = term if y is None else y + term
    y = y + x * w_ref[taps - 1:taps, :]
    tail = buf[n:n + pad, :]
    buf[0:pad, :] = tail
    return y


def _l2_normalize(x):
    return x * lax.rsqrt(jnp.sum(x * x, axis=-1, keepdims=True) + NORM_EPS)


def _gates(ba, gp_ref):
    return jax.nn.sigmoid(ba), -jnp.exp(gp_ref[0:1, :]) * _softplus(ba + gp_ref[1:2, :])


def _inproj_body(x_ref, nw_ref, w_ref, wqv_ref, *out_refs, transposed_qv):
    xn = _rms(x_ref[...], nw_ref[...]).astype(BF16)
    lo = 0
    for ref in out_refs[:len(_ROW_WIDTHS)]:
        width = ref.shape[1]
        ref[...] = _dot(xn, w_ref[:, lo:lo + width])
        lo += width
    q_ref, v_ref = out_refs[len(_ROW_WIDTHS):]
    mw = MOBA_WIDTH
    if transposed_qv:
        q_ref[0] = _dot_nt(wqv_ref[0:mw, :], xn)
        v_ref[0] = _dot_nt(wqv_ref[mw:2 * mw, :], xn)
    else:
        q_ref[...] = _dot(xn, wqv_ref[:, 0:mw])
        v_ref[...] = _dot(xn, wqv_ref[:, mw:2 * mw])


def _inproj(x, nw, w_rows, w_qv, *, seq_len=None, tm=1024):
    m, d = x.shape
    tm = min(tm, m if seq_len is None else seq_len)
    assert m % tm == 0 and w_rows.shape[1] == sum(_ROW_WIDTHS)
    row = lambda i: (i, 0)
    out_specs = [pl.BlockSpec((tm, wd), row) for wd in _ROW_WIDTHS]
    out_shape = [jax.ShapeDtypeStruct((m, wd), F32) for wd in _ROW_WIDTHS]
    transposed_qv = seq_len is not None
    if transposed_qv:
        assert seq_len % tm == 0
        tiles = seq_len // tm
        spec = pl.BlockSpec((1, MOBA_WIDTH, tm), lambda i: (i // tiles, 0, i % tiles))
        shape = jax.ShapeDtypeStruct((m // seq_len, MOBA_WIDTH, seq_len), F32)
    else:
        spec = pl.BlockSpec((tm, MOBA_WIDTH), row)
        shape = jax.ShapeDtypeStruct((m, MOBA_WIDTH), F32)
    out_specs += [spec, spec]
    out_shape += [shape, shape]
    return pl.pallas_call(
        functools.partial(_inproj_body, transposed_qv=transposed_qv),
        name="inproj",
        grid=(m // tm,),
        in_specs=[pl.BlockSpec((tm, d), row), _resident((1, d)), _resident(w_rows.shape),
                  _resident(w_qv.shape)],
        out_specs=out_specs,
        out_shape=out_shape,
        compiler_params=_params("arbitrary"),
    )(x, nw.reshape(1, d), w_rows, w_qv)


def _mixer_body(*refs, c, ng, has_state, row_lo, row_hi):
    refs = list(refs)
    qkv_ref, z_ref, ba_ref, sc_ref = refs[:4]
    refs = refs[4:]
    if has_state:
        qh_ref, sh_ref, s0_ref = refs[:3]
        refs = refs[3:]
    cw_ref, gp_ref, nw_ref, scw_ref = refs[:4]
    refs = refs[5:]
    oa_ref, ob_ref, st_ref, gt_ref = refs[:4]
    xbuf, gbuf, s_ref = refs[4:]
    n = pl.program_id(1)
    hd = GDN_HEAD_DIM
    gr = GROUP_ROWS
    sr = STACK_ROWS
    pad = SUBLANES
    seqs = gr // c
    shift = c.bit_length() - 1
    assert 1 << shift == c

    @pl.when(n == 0)
    def _():
        xbuf[:, 0:pad, :] = jnp.zeros((ng, pad, xbuf.shape[2]), F32)
        gbuf[:, 0:pad, :] = jnp.zeros((ng, pad, gbuf.shape[2]), F32)
        if has_state:
            s_ref[...] = s0_ref[0]
        else:
            s_ref[...] = jnp.zeros(s_ref.shape, F32)

    rows = lax.broadcasted_iota(jnp.int32, (gr, 1), 0) & (c - 1)
    is_hist = rows < row_lo
    live = jnp.where((rows >= row_lo) & (rows < row_hi), 1.0, 0.0)
    ri = lax.broadcasted_iota(jnp.int32, (gr, gr), 0)
    ci = lax.broadcasted_iota(jnp.int32, (gr, gr), 1)
    same = (ri >> shift) == (ci >> shift)
    cum_op = jnp.concatenate([jnp.where(same & (ri >= ci), 1.0, 0.0),
                              jnp.where(same, 1.0, 0.0)], axis=0)
    nh = GDN_HEADS
    tiles = sr // LANES
    band_tile = [h * gr // LANES for h in range(nh)]

    def bands(mat):
        return [mat[h * gr:(h + 1) * gr, band_tile[h] * LANES:(band_tile[h] + 1) * LANES]
                for h in range(nh)]

    def unband(pieces):
        zero = jnp.zeros((gr, LANES), BF16)
        return jnp.concatenate(
            [jnp.concatenate([p.astype(BF16) if t == band_tile[h] else zero
                              for t in range(tiles)], axis=1)
             for h, p in enumerate(pieces)], axis=0)

    incl, strict, eye = [], [], []
    for h in range(nh):
        rs = lax.broadcasted_iota(jnp.int32, (gr, LANES), 0) + h * gr
        cs = lax.broadcasted_iota(jnp.int32, (gr, LANES), 1) + band_tile[h] * LANES
        same_s = (rs >> shift) == (cs >> shift)
        incl.append(same_s & (rs >= cs))
        strict.append(same_s & (rs > cs))
        eye.append(jnp.where(rs == cs, 1.0, 0.0))

    def stack_heads(a, lo):
        return jnp.concatenate([a[:, lo + h * hd:lo + (h + 1) * hd] for h in range(GDN_HEADS)],
                               axis=0)

    def stack_cols(a, lo):
        return jnp.concatenate([a[:, lo + h:lo + h + 1] for h in range(GDN_HEADS)], axis=0)

    groups = range(ng)

    def front(g):
        x = qkv_ref[g]
        if has_state:
            x = jnp.where(is_hist, qh_ref[g], x)
        act = _silu(_causal_conv(xbuf.at[g], x, cw_ref))
        q = _l2_normalize(stack_heads(act, 0)) * (hd ** -0.5)
        k = _l2_normalize(stack_heads(act, GDN_WIDTH))
        v = stack_heads(act, 2 * GDN_WIDTH)

        sc = sc_ref[g]
        scw = SC_WIDTH
        gated = sc[:, scw:2 * scw] * sc[:, 0:scw]
        if has_state:
            gated = jnp.where(is_hist, sh_ref[g], gated)
        yb = _causal_conv(gbuf.at[g], gated, scw_ref)
        ob_ref[g] = (sc[:, 2 * scw:3 * scw] * yb).astype(ob_ref.dtype)
        gt_ref[g] = gated

        beta_all, g_all = _gates(ba_ref[g], gp_ref)
        beta_all, g_all = beta_all * live, g_all * live
        cum = _dot(cum_op, g_all, HI)
        gc_all, glast_all = cum[:gr], cum[gr:]
        beta = stack_cols(beta_all, 0)
        gc = stack_cols(gc_all, nh)
        eg = jnp.exp(gc)
        etail = jnp.exp(stack_cols(glast_all, nh) - gc)
        gc_row = jnp.broadcast_to(gc, (sr, LANES)).T[0:1, :]
        decay = []
        for h in range(nh):
            diff = (gc[h * gr:(h + 1) * gr]
                    - gc_row[:, band_tile[h] * LANES:(band_tile[h] + 1) * LANES])
            decay.append(jnp.where(incl[h], jnp.exp(jnp.where(incl[h], diff, 0.0)), 0.0))

        kb = k.astype(BF16)
        kk = bands(_dot_nt(kb, kb))
        m_strict = [jnp.where(strict[h], beta[h * gr:(h + 1) * gr] * kk[h] * decay[h], 0.0)
                    for h in range(nh)]
        return dict(q=q, k=k, v=v, kb=kb, beta=beta, eg=eg, etail=etail, decay=decay,
                    cd_all=jnp.exp(glast_all), m_strict=m_strict)

    fr = [front(g) for g in groups]

    invs = [[e - m for e, m in zip(eye, f["m_strict"])] for f in fr]
    pbs = [unband(f["m_strict"]) for f in fr]
    for _ in range(shift - 1):
        pbs = [unband(bands(_dot(pb, pb))) for pb in pbs]
        invs = [[i + d for i, d in zip(inv, bands(_dot(unband(inv), pb)))]
                for inv, pb in zip(invs, pbs)]

    def solve(f, inv):
        rhs = jnp.concatenate([f["k"] * (f["beta"] * f["eg"]), f["v"] * f["beta"]],
                              axis=-1).astype(BF16)
        sol = _dot(unband(inv), rhs)
        qk = bands(_dot_nt(f["q"].astype(BF16), f["kb"]))
        attn = unband([a * d for a, d in zip(qk, f["decay"])])
        return dict(w_k=sol[:, :hd], u_v=sol[:, hd:], attn=attn, qd=f["q"] * f["eg"],
                    kt=(f["k"] * f["etail"]).astype(BF16))

    sv = [solve(f, inv) for f, inv in zip(fr, invs)]

    pair_ids = [(h, s) for h in range(GDN_HEADS) for s in range(seqs)]

    def read_state(g, t):
        parts = []
        for h, s in pair_ids:
            r0 = h * gr + s * c
            s_old = s_ref[g, s * nh + h]
            lhs = jnp.concatenate([t["w_k"][r0:r0 + c], t["qd"][r0:r0 + c]], axis=0)
            res = _dot(lhs.astype(BF16), s_old.astype(BF16))
            parts.append((s_old, t["u_v"][r0:r0 + c] - res[:c], res[c:]))
        return parts

    rd = [read_state(g, t) for g, t in zip(groups, sv)]
    us = [jnp.concatenate([p[1] for p in parts], axis=0) for parts in rd]
    outs = [jnp.concatenate([p[2] for p in parts], axis=0) + _dot(t["attn"], u.astype(BF16))
            for parts, t, u in zip(rd, sv, us)]
    for g in groups:
        for (h, s), (s_old, u_p, _) in zip(pair_ids, rd[g]):
            r0 = h * gr + s * c
            cd = fr[g]["cd_all"][s * c:s * c + 1, nh + h:nh + h + 1]
            s_ref[g, s * nh + h] = s_old * cd + _dot_tn(sv[g]["kt"][r0:r0 + c],
                                                        u_p.astype(BF16))
    for g in groups:
        zs = stack_heads(z_ref[g], 0)
        o_n = _rms(outs[g], nw_ref[...]) * _silu(zs)
        oa_ref[g] = jnp.concatenate([o_n[h * gr:(h + 1) * gr] for h in range(GDN_HEADS)],
                                    axis=1).astype(oa_ref.dtype)

    @pl.when(n == pl.num_programs(1) - 1)
    def _():
        st_ref[0] = s_ref[...]


def _mixer(qkv, z, ba, sc, conv_w, gparams, norm_w, sc_conv_w, *, c, ng, layer, depth,
           prev_states=None, state=None, row_lo=0, row_hi=None):
    g_total, l, _ = qkv.shape
    gr = GROUP_ROWS
    assert l % gr == 0 and gr % c == 0 and g_total % ng == 0
    nc = l // gr
    pairs = GDN_HEADS * (gr // c)
    row_hi = c if row_hi is None else row_hi
    blk = lambda w: pl.BlockSpec((ng, gr, w), lambda i, n: (i, n, 0))
    in_specs = [blk(qkv.shape[2]), blk(z.shape[2]), blk(ba.shape[2]), blk(sc.shape[2])]
    args = [qkv, z, ba, sc]
    st_shape = (g_total, pairs, GDN_HEAD_DIM, GDN_HEAD_DIM)
    st_spec = pl.BlockSpec((1, ng) + st_shape[1:], lambda i, n: (layer, i, 0, 0, 0))
    if state is not None:
        assert nc == 1
        qh, sh, s0 = state
        assert s0.shape[1:] == st_shape
        in_specs += [pl.BlockSpec((ng, gr, qh.shape[2]), lambda i, n: (i, 0, 0)),
                     pl.BlockSpec((ng, gr, sh.shape[2]), lambda i, n: (i, 0, 0)), st_spec]
        args += [qh, sh, s0]
    const = lambda a: pl.BlockSpec(a.shape, lambda i, n: (0,) * a.ndim)
    norm_w = norm_w.reshape(1, -1)
    in_specs += [const(conv_w), const(gparams), const(norm_w), const(sc_conv_w)]
    args += [conv_w, gparams, norm_w, sc_conv_w]
    if prev_states is None:
        prev_states = jnp.zeros((depth,) + st_shape, F32)
    assert prev_states.shape == (depth,) + st_shape
    aliases = {len(args): 2}
    in_specs.append(pl.BlockSpec(memory_space=pl.ANY))
    args.append(prev_states)
    out_shape = [jax.ShapeDtypeStruct((g_total, l, GDN_WIDTH), BF16),
                 jax.ShapeDtypeStruct((g_total, l, SC_WIDTH), BF16),
                 jax.ShapeDtypeStruct((depth,) + st_shape, F32),
                 jax.ShapeDtypeStruct((g_total, l, SC_WIDTH), F32)]
    out_specs = [blk(GDN_WIDTH), blk(SC_WIDTH), st_spec, blk(SC_WIDTH)]
    body = functools.partial(_mixer_body, c=c, ng=ng, has_state=state is not None,
                             row_lo=row_lo, row_hi=row_hi)
    return pl.pallas_call(
        body,
        name="seq_mixer",
        grid=(g_total // ng, nc),
        in_specs=in_specs,
        out_specs=out_specs,
        out_shape=out_shape,
        input_output_aliases=aliases,
        scratch_shapes=[pltpu.VMEM((ng, gr + SUBLANES, qkv.shape[2]), F32),
                        pltpu.VMEM((ng, gr + SUBLANES, SC_WIDTH), F32),
                        pltpu.VMEM((ng,) + st_shape[1:], F32)],
        compiler_params=_params("arbitrary", "arbitrary"),
    )(*args)


def _select_blocks(gate, n_valid, axis=0):
    nb = gate.shape[axis]
    sub = lax.broadcasted_iota(jnp.int32, gate.shape, axis)
    gate = jnp.where(sub < n_valid, gate, NEG_INF)
    sel = jnp.zeros(gate.shape, F32)
    for j in range(nb):
        gj = gate[j:j + 1, :] if axis == 0 else gate[:, j:j + 1]
        beats = (gate > gj) | ((gate == gj) & (sub < j))
        cnt = jnp.sum(jnp.where(beats, 1.0, 0.0), axis=axis, keepdims=True)
        hit = (cnt < MOBA_TOPK) & (j < n_valid)
        sel = jnp.where((sub == j) & hit, 1.0, sel)
    return sel


_PV_ROWS = MOBA_HEAD_DIM + 16


def _moba_prompt_body(q_ref, k_ref, v_ref, o_ref, kmean_ref, kb_ref, vb_ref, sel_ref, *, nbat):
    i = pl.program_id(1)
    blk = MOBA_BLOCK
    hd = MOBA_HEAD_DIM
    w = MOBA_WIDTH
    l_seq = v_ref.shape[2]
    nb = l_seq // blk
    heads = range(MOBA_HEADS)
    head_rows = [slice(h * hd, (h + 1) * hd) for h in heads]
    chains = [(b, h) for b in range(nbat) for h in heads]

    @pl.when(i == 0)
    def _():
        ones = jnp.ones((_PV_ROWS - hd, blk), BF16)
        lane = lax.broadcasted_iota(jnp.int32, (1, w), 1)
        for b in range(nbat):
            for j in range(nb):
                kj = k_ref[b * l_seq + j * blk:b * l_seq + (j + 1) * blk, :]
                kmean = jnp.mean(kj, axis=0, keepdims=True)
                kb_ref[b, j] = kj.astype(BF16)
                for h in heads:
                    kmean_ref[b, h * nb + j:h * nb + j + 1, :] = jnp.where(
                        (lane >= h * hd) & (lane < (h + 1) * hd), kmean, 0.0)
                    vb_ref[b, j, h, 0:hd, :] = v_ref[b, head_rows[h],
                                                     j * blk:(j + 1) * blk].astype(BF16)
                    vb_ref[b, j, h, hd:_PV_ROWS, :] = ones

    feat = lax.broadcasted_iota(jnp.int32, (w, blk), 0)
    key_i = lax.broadcasted_iota(jnp.int32, (blk, blk), 0)
    qry_i = lax.broadcasted_iota(jnp.int32, (blk, blk), 1)
    causal = key_i <= qry_i
    qmb = []
    for b in range(nbat):
        qt = q_ref[b] * (hd ** -0.5)
        qms = [jnp.where((feat >= h * hd) & (feat < (h + 1) * hd), qt, 0.0) for h in heads]
        qmb.append(jnp.concatenate(qms, axis=1).astype(BF16))
        gates = _dot(kmean_ref[b], qt, HI)
        for h in heads:
            sel_ref[b, h] = _select_blocks(gates[h * nb:(h + 1) * nb], i)

    def scores(j):
        s_all = [_dot(kb_ref[b, j], qmb[b]) for b in range(nbat)]
        return [s_all[b][:, h * blk:(h + 1) * blk] for b, h in chains]

    def attend(j, s_list, m_list):
        ps = [jnp.exp(s - m).astype(BF16) for s, m in zip(s_list, m_list)]
        return [_dot(vb_ref[b, j, h], p) for (b, h), p in zip(chains, ps)]

    s0 = [jnp.where(causal, s, NEG_INF) for s in scores(i)]
    m0 = [jnp.max(s, axis=0, keepdims=True) for s in s0]
    pv0 = attend(i, s0, m0)
    init = []
    for m, pv in zip(m0, pv0):
        init += [m, pv[hd:hd + 1], pv[:hd]]

    def past_block(j, carry):
        s_list = [jnp.where(sel_ref[b, h, pl.ds(j, 1), :] > 0.5, s, NEG_INF)
                  for (b, h), s in zip(chains, scores(j))]
        m_old = carry[0::3]
        m_new = [jnp.maximum(m, jnp.max(s, axis=0, keepdims=True))
                 for m, s in zip(m_old, s_list)]
        pv = attend(j, s_list, m_new)
        new = []
        for c in range(len(chains)):
            alpha = jnp.exp(m_old[c] - m_new[c])
            new += [m_new[c], alpha * carry[3 * c + 1] + pv[c][hd:hd + 1],
                    alpha * carry[3 * c + 2] + pv[c][:hd]]
        return tuple(new)

    final = lax.fori_loop(0, i, past_block, tuple(init))
    for c, (b, h) in enumerate(chains):
        o_ref[b, head_rows[h], :] = (final[3 * c + 2] / final[3 * c + 1]).astype(o_ref.dtype)


def _moba_prompt(qt, k, vt, nbat=4):
    b, w, l = qt.shape
    blk = MOBA_BLOCK
    nbat = nbat if b % nbat == 0 else 1
    assert l % blk == 0
    nb = l // blk
    return pl.pallas_call(
        functools.partial(_moba_prompt_body, nbat=nbat),
        name="moba_prompt",
        grid=(b // nbat, nb),
        in_specs=[pl.BlockSpec((nbat, w, blk), lambda i, t: (i, 0, t)),
                  pl.BlockSpec((nbat * l, w), lambda i, t: (i, 0)),
                  pl.BlockSpec((nbat, w, l), lambda i, t: (i, 0, 0))],
        out_specs=pl.BlockSpec((nbat, w, blk), lambda i, t: (i, 0, t)),
        out_shape=jax.ShapeDtypeStruct((b, w, l), BF16),
        scratch_shapes=[pltpu.VMEM((nbat, MOBA_HEADS * nb, w), F32),
                        pltpu.VMEM((nbat, nb, blk, w), BF16),
                        pltpu.VMEM((nbat, nb, MOBA_HEADS, _PV_ROWS, blk), BF16),
                        pltpu.VMEM((nbat, MOBA_HEADS, nb, blk), F32)],
        compiler_params=_params("arbitrary", "arbitrary"),
    )(qt, k, vt)


PAGE_SLOTS = 3
SEQS_PER_STEP = 2


def _moba_sample_body(pt_ref, q_ref, kn_ref, vn_ref, ck_ref, cv_ref, o_ref, kbuf, vbuf, sem,
                      *, n_pages, base, n_steps, ns):
    i = pl.program_id(0)
    hd = MOBA_HEAD_DIM
    r = S_ROWS
    w = MOBA_WIDTH
    ppb = MOBA_BLOCK // PAGE_SIZE
    nb = n_pages // ppb
    ahead = PAGE_SLOTS - 1
    seqs = range(ns)
    blocks = range(nb)

    def page_copies(step, slot):
        cps = []
        for s in seqs:
            for p in range(n_pages):
                row = base + pt_ref[step * ns + s, p]
                cps.append(pltpu.make_async_copy(ck_ref.at[row], kbuf.at[slot, s, p],
                                                 sem.at[0, slot]))
                cps.append(pltpu.make_async_copy(cv_ref.at[row], vbuf.at[slot, s, p],
                                                 sem.at[1, slot]))
        return cps

    def start_all(cps):
        for n, cp in enumerate(cps):
            cp.start(priority=n % 2)

    @pl.when(i == 0)
    def _():
        for d in range(ahead):
            start_all(page_copies(d, d))

    slot = lax.rem(i, PAGE_SLOTS)
    for cp in page_copies(i, slot):
        cp.wait()

    lane = lax.broadcasted_iota(jnp.int32, (r, w), 1)
    head_masks = [jnp.where((lane >= h * hd) & (lane < (h + 1) * hd), 1.0, 0.0)
                  for h in range(MOBA_HEADS)]
    col = lax.broadcasted_iota(jnp.int32, (w, LANES), 1)
    q_bd = [jnp.concatenate([q_ref[s] * (hd ** -0.5) * hm for hm in head_masks], axis=0)
            for s in seqs]
    q_bb = [x.astype(BF16) for x in q_bd]

    kp = [[kbuf[slot, s, p] for p in range(n_pages)] for s in seqs]
    s_pages = [[_dot(q_bb[s], x.astype(BF16)) for x in kp[s]] for s in seqs]
    kmean_t = []
    for s in seqs:
        km = jnp.zeros((w, LANES), F32)
        for j in blocks:
            ksum = kp[s][j * ppb]
            for pp in range(1, ppb):
                ksum = ksum + kp[s][j * ppb + pp]
            km = jnp.where(col == j, jnp.sum(ksum, axis=1, keepdims=True) * (1.0 / MOBA_BLOCK),
                           km)
        kmean_t.append(km)
    s_blk = [[jnp.concatenate(s_pages[s][j * ppb:(j + 1) * ppb], axis=-1) for j in blocks]
             for s in seqs]
    m_cols = [[jnp.max(x, axis=1, keepdims=True) for x in s_blk[s]] for s in seqs]
    e_blk = [[jnp.exp(x - m) for x, m in zip(s_blk[s], m_cols[s])] for s in seqs]
    l_cols = [[jnp.sum(e, axis=1, keepdims=True) for e in e_blk[s]] for s in seqs]
    accs = []
    for s in seqs:
        row_accs = []
        for j in blocks:
            eb = e_blk[s][j].astype(BF16)
            acc = None
            for pp in range(ppb):
                vp = vbuf[slot, s, j * ppb + pp].astype(BF16)
                part = _dot_nt(eb[:, pp * PAGE_SIZE:(pp + 1) * PAGE_SIZE], vp)
                acc = part if acc is None else acc + part
            row_accs.append(acc)
        accs.append(row_accs)

    nxt = jnp.minimum(i + ahead, n_steps - 1)
    start_all(page_copies(nxt, lax.rem(i + ahead, PAGE_SLOTS)))

    gates = [_dot(q_bd[s], kmean_t[s], HI)[:, :nb] for s in seqs]
    sels = [_select_blocks(g, nb, axis=1) > 0.5 for g in gates]
    qt = lax.broadcasted_iota(jnp.int32, (MOBA_HEADS * r, r), 0) & (r - 1)
    kt = lax.broadcasted_iota(jnp.int32, (MOBA_HEADS * r, r), 1)
    own = (kt >= S_LO) & (kt <= qt)
    for s in seqs:
        sel = sels[s]
        m_all = jnp.concatenate(m_cols[s], axis=1)
        l_all = jnp.concatenate(l_cols[s], axis=1)
        s_own = _dot_nt(q_bb[s], kn_ref[s].astype(BF16))
        m_tot = jnp.maximum(jnp.max(jnp.where(sel, m_all, NEG_INF), axis=1, keepdims=True),
                            jnp.max(jnp.where(own, s_own, NEG_INF), axis=1, keepdims=True))
        wj = jnp.where(sel, jnp.exp(jnp.where(sel, m_all - m_tot, 0.0)), 0.0)
        p_own = jnp.where(own, jnp.exp(jnp.where(own, s_own - m_tot, 0.0)), 0.0)
        l_tot = (jnp.sum(wj * l_all, axis=1, keepdims=True)
                 + jnp.sum(p_own, axis=1, keepdims=True))
        acc = _dot(p_own.astype(BF16), vn_ref[s].astype(BF16))
        for j in blocks:
            acc = acc + wj[:, j:j + 1] * accs[s][j]
        o_bd = acc / l_tot
        o = None
        for h in range(MOBA_HEADS):
            part = o_bd[h * r:(h + 1) * r, :] * head_masks[h]
            o = part if o is None else o + part
        o_ref[s] = o.astype(o_ref.dtype)

    @pl.when(i == n_steps - 1)
    def _():
        for d in range(1, PAGE_SLOTS):
            for cp in page_copies(n_steps - 1, lax.rem(i + d, PAGE_SLOTS)):
                cp.wait()


def _moba_sample(q, k_new, v_new, cache_kt, cache_vt, page_table, base):
    b, r, w = q.shape
    page = cache_kt.shape[2]
    n_pages = page_table.shape[1]
    ns = SEQS_PER_STEP if b % SEQS_PER_STEP == 0 else 1
    n_steps = b // ns
    assert page == PAGE_SIZE and r == S_ROWS and cache_kt.shape[1] == w
    assert n_steps >= PAGE_SLOTS
    tile = pl.BlockSpec((ns, r, w), lambda i, pt: (i, 0, 0))
    hbm = pl.BlockSpec(memory_space=pl.ANY)
    grid_spec = pltpu.PrefetchScalarGridSpec(
        num_scalar_prefetch=1,
        grid=(n_steps,),
        in_specs=[tile, tile, tile, hbm, hbm],
        out_specs=tile,
        scratch_shapes=[pltpu.VMEM((PAGE_SLOTS, ns, n_pages, w, page), F32),
                        pltpu.VMEM((PAGE_SLOTS, ns, n_pages, w, page), F32),
                        pltpu.SemaphoreType.DMA((2, PAGE_SLOTS))],
    )
    body = functools.partial(_moba_sample_body, n_pages=n_pages, base=base, n_steps=n_steps,
                             ns=ns)
    return pl.pallas_call(
        body,
        name="moba_sample",
        grid_spec=grid_spec,
        out_shape=jax.ShapeDtypeStruct((b, r, w), BF16),
        compiler_params=_params("arbitrary"),
    )(page_table, q, k_new, v_new, cache_kt, cache_vt)


def _trunk(x, weights, ffn_w, norm_final, *, sample=None):
    b, l, d = x.shape
    depth = len(weights)
    x = x.reshape(b * l, d)
    gr = GROUP_ROWS
    states = []
    gdn_states = None
    for layer, wts in enumerate(weights):
        x = _ffn(x, wts["n_ffn1"], *ffn_w[0], layer)
        if sample is None:
            qkv, z, sc, ba, mk, mq, mv = _inproj(x, wts["n_mix"], wts["w_rows"], wts["w_qv_t"],
                                                 seq_len=l)
            grp = lambda a: a.reshape(b, l, a.shape[-1])
            o_a, o_b, gdn_states, gated = _mixer(
                grp(qkv), grp(z), grp(ba), grp(sc), wts["gdn_conv_w"], wts["gparams"],
                wts["gdn_norm_w"], wts["sc_conv_w"], c=GDN_CHUNK, ng=8 if b % 8 == 0 else 1,
                layer=layer, depth=depth, prev_states=gdn_states)
            o_c = _moba_prompt(mq, mk, mv)
        else:
            qkv, z, sc, ba, mk, mq, mv = _inproj(x, wts["n_mix"], wts["w_rows"], wts["w_qv"])
            ngroups = b * l // gr
            grp = lambda a: a.reshape(ngroups, gr, a.shape[-1])
            state = (grp(sample["gdn_conv"][layer]), grp(sample["sconv"][layer]), sample["gdn"])
            o_a, o_b, gdn_states, gated = _mixer(
                grp(qkv), grp(z), grp(ba), grp(sc), wts["gdn_conv_w"], wts["gparams"],
                wts["gdn_norm_w"], wts["sc_conv_w"], c=l, ng=2 if ngroups % 2 == 0 else 1,
                layer=layer, depth=depth, prev_states=gdn_states, state=state,
                row_lo=S_LO, row_hi=S_HI)
            seq = lambda a: a.reshape(b, l, a.shape[-1])
            o_c = _moba_sample(seq(mq), seq(mk), seq(mv), sample["cache_kt"], sample["cache_vt"],
                               sample["page_table"], layer * sample["n_pool"])
            o_c = o_c.reshape(b * l, -1)
        mix = (o_a.reshape(b * l, -1), o_b.reshape(b * l, -1), o_c, wts["w_out"])
        x = _ffn(x, wts["n_ffn2"], *ffn_w[1], layer, mix=mix,
                 final_w=norm_final if layer == depth - 1 else None)
        states.append(dict(qkv=qkv.reshape(b, l, -1), gated=gated.reshape(b, l, -1),
                           k=mk.reshape(b, l, -1), v=mv))
    return x.reshape(b, l, d), states, gdn_states


def kernel(x_prompt, x_sample, state_gdn, state_gdn_conv, state_sconv, cache_k, cache_v,
           page_table, norm_ffn1, ffn1_w_gate, ffn1_w_up, ffn1_w_down, norm_mix, w_in,
           gdn_conv_w, gdn_a_log, gdn_dt_bias, gdn_norm_w, sc_conv_w, w_out,
           norm_ffn2, ffn2_w_gate, ffn2_w_up, ffn2_w_down, norm_final):
    depth = w_in.shape[0]
    gw, sw, mw = GDN_WIDTH, SC_WIDTH, MOBA_WIDTH
    nh = GDN_HEADS
    mh, md = MOBA_HEADS, MOBA_HEAD_DIM

    weights = []
    for la in range(depth):
        w = w_in[la]
        ba_off = 4 * gw
        sc_off = ba_off + 2 * nh
        mo_off = sc_off + 3 * sw
        w_rows = jnp.concatenate(
            [w[:, :ba_off], w[:, sc_off:mo_off],
             jnp.pad(w[:, ba_off:sc_off], ((0, 0), (0, LANES - 2 * nh))),
             w[:, mo_off + mw:mo_off + 2 * mw]], axis=1).astype(BF16)
        w_qv = jnp.concatenate([w[:, mo_off:mo_off + mw], w[:, mo_off + 2 * mw:mo_off + 3 * mw]],
                               axis=1).astype(BF16)
        gparams = jnp.zeros((SUBLANES, LANES), F32)
        gparams = gparams.at[0, nh:2 * nh].set(gdn_a_log[la].astype(F32))
        gparams = gparams.at[1, nh:2 * nh].set(gdn_dt_bias[la].astype(F32))
        weights.append(dict(
            n_ffn1=norm_ffn1[la],
            n_mix=norm_mix[la], w_rows=w_rows, w_qv=w_qv, w_qv_t=w_qv.T,
            gdn_conv_w=gdn_conv_w[la], gparams=gparams, gdn_norm_w=gdn_norm_w[la],
            sc_conv_w=sc_conv_w[la], w_out=w_out,
            n_ffn2=norm_ffn2[la]))
    ffn_w = ((ffn1_w_gate, ffn1_w_up, ffn1_w_down), (ffn2_w_gate, ffn2_w_up, ffn2_w_down))

    pb, pl_, _ = x_prompt.shape
    y_p, st_p, p_gdn = _trunk(x_prompt, weights, ffn_w, norm_final)
    hist = GDN_CONV - 1
    p_conv = jnp.stack([s["qkv"][:, -hist:] for s in st_p])
    p_sconv = jnp.stack([s["gated"][:, -(SC_CONV - 1):] for s in st_p])
    p_k = jnp.stack([s["k"] for s in st_p]).reshape(depth, pb, pl_, mh, md)
    p_v = jnp.stack([s["v"] for s in st_p]).reshape(depth, pb, mh, md, pl_)
    p_v = p_v.transpose(0, 1, 4, 2, 3)

    db, dl, d = x_sample.shape
    assert dl == S_HI - S_LO and (db * S_ROWS) % GROUP_ROWS == 0
    n_pool = cache_k.shape[1]
    tile_pad = lambda a, lo: jnp.pad(a, ((0, 0),) * (a.ndim - 2)
                                     + ((lo, S_ROWS - lo - a.shape[-2]), (0, 0)))
    seqs_per_group = GROUP_ROWS // S_ROWS
    paged_t = lambda c: c.transpose(0, 1, 3, 4, 2).reshape(depth * n_pool, mw, PAGE_SIZE)
    sample = dict(
        gdn=state_gdn.reshape(depth, db // seqs_per_group, seqs_per_group * nh,
                              GDN_HEAD_DIM, GDN_HEAD_DIM),
        gdn_conv=tile_pad(state_gdn_conv, 0),
        sconv=tile_pad(state_sconv, S_LO - (SC_CONV - 1)),
        cache_kt=paged_t(cache_k), cache_vt=paged_t(cache_v), n_pool=n_pool,
        page_table=page_table)
    y_s, st_s, s_gdn = _trunk(tile_pad(x_sample, S_LO), weights, ffn_w, norm_final,
                              sample=sample)
    y_s = y_s[:, S_LO:S_HI]
    s_gdn = s_gdn.reshape(state_gdn.shape)
    s_conv = jnp.stack([s["qkv"][:, S_HI - hist:S_HI] for s in st_s])
    s_sconv = jnp.stack([s["gated"][:, S_HI - (SC_CONV - 1):S_HI] for s in st_s])
    s_k = jnp.stack([s["k"][:, S_LO:S_HI] for s in st_s]).reshape(depth, db, dl, mh, md)
    s_v = jnp.stack([s["v"].reshape(db, S_ROWS, mw)[:, S_LO:S_HI] for s in st_s]).reshape(
        depth, db, dl, mh, md)
    return (y_p, y_s, p_gdn, p_conv, p_sconv, p_k, p_v, s_gdn, s_conv, s_sconv, s_k, s_v)
```

```python
import functools

import jax
import jax.numpy as jnp
from jax import lax
from jax.experimental import pallas as pl
from jax.experimental.pallas import tpu as pltpu

NORM_EPS = 1e-6
NEG_INF = -1e30

GDN_HEADS = 4
GDN_HEAD_DIM = 128
GDN_WIDTH = GDN_HEADS * GDN_HEAD_DIM
GDN_CONV = 4
GDN_CHUNK = 64
SC_WIDTH = 256
SC_CONV = 3
MOBA_HEADS = 4
MOBA_HEAD_DIM = 64
MOBA_WIDTH = MOBA_HEADS * MOBA_HEAD_DIM
MOBA_BLOCK = 256
MOBA_TOPK = 3
PAGE_SIZE = 128

LANES = 128
SUBLANES = 8
VMEM_LIMIT_BYTES = 56 * 1024 * 1024

S_ROWS = SUBLANES
S_LO = GDN_CONV - 1
S_HI = S_LO + 4

GROUP_ROWS = GDN_CHUNK
STACK_ROWS = GDN_HEADS * GROUP_ROWS

F32 = jnp.float32
BF16 = jnp.bfloat16
HI = lax.Precision.HIGHEST


def _dot(a, b, precision=None):
    return jnp.dot(a, b, preferred_element_type=F32, precision=precision)


def _dot_nt(a, b, precision=None):
    return lax.dot_general(a, b, (((1,), (1,)), ((), ())),
                           preferred_element_type=F32, precision=precision)


def _dot_tn(a, b, precision=None):
    return lax.dot_general(a, b, (((0,), (0,)), ((), ())),
                           preferred_element_type=F32, precision=precision)


def _rms(x, w):
    return x * lax.rsqrt(jnp.mean(x * x, axis=-1, keepdims=True) + NORM_EPS) * w


def _silu(x):
    return x * jax.nn.sigmoid(x)


def _softplus(x):
    return jnp.maximum(x, 0.0) + jnp.log1p(jnp.exp(-jnp.abs(x)))


def _resident(shape):
    nd = len(shape)
    return pl.BlockSpec(shape, lambda *_: (0,) * nd, pipeline_mode=pl.Buffered(1))


def _params(*semantics):
    return pltpu.CompilerParams(dimension_semantics=semantics,
                                vmem_limit_bytes=VMEM_LIMIT_BYTES)


def _cast_body(w_ref, o_ref):
    o_ref[...] = w_ref[0].astype(o_ref.dtype)


def _to_bf16(w, layer, rows=256):
    _, r, c = w.shape
    rows = min(rows, r)
    assert r % rows == 0
    return pl.pallas_call(
        _cast_body,
        name="to_bf16",
        grid=(r // rows,),
        in_specs=[pl.BlockSpec((1, rows, c), lambda i: (layer, i, 0))],
        out_specs=pl.BlockSpec((rows, c), lambda i: (i, 0)),
        out_shape=jax.ShapeDtypeStruct((r, c), BF16),
        compiler_params=_params("arbitrary"),
    )(w)


def _ffn_body(*refs, has_mix, oc_transposed, has_final, f_chunk, layer):
    refs = list(refs)
    x_ref = refs.pop(0)
    if has_mix:
        oa_ref, ob_ref, oc_ref, wo_ref = refs[:4]
        refs = refs[4:]
    nw_ref, wg_hbm, wu_hbm, wd_hbm = refs[:4]
    refs = refs[4:]
    if has_final:
        nf_ref = refs.pop(0)
    out_ref, xn_ref, acc_ref, wg_ref, wu_ref, wd_ref, wsem = refs
    i = pl.program_id(0)
    chunks = range(wg_ref.shape[1] // f_chunk)

    def chunk_copies(c):
        cols = pl.ds(c * f_chunk, f_chunk)
        return [pltpu.make_async_copy(wg_hbm.at[layer, :, cols], wg_ref.at[:, cols], wsem.at[0, c]),
                pltpu.make_async_copy(wu_hbm.at[layer, :, cols], wu_ref.at[:, cols], wsem.at[1, c]),
                pltpu.make_async_copy(wd_hbm.at[layer, cols, :], wd_ref.at[cols, :], wsem.at[2, c])]

    @pl.when(i == 0)
    def _():
        for c in chunks:
            for cp in chunk_copies(c):
                cp.start()

    def run(first_step):
        x = x_ref[...]
        if has_mix:
            ga = oa_ref.shape[1]
            gb = ga + ob_ref.shape[1]
            x = x + _dot(oa_ref[...], wo_ref[0:ga, :]) + _dot(ob_ref[...], wo_ref[ga:gb, :])
            if oc_transposed:
                x = x + _dot_tn(oc_ref[0], wo_ref[gb:, :])
            else:
                x = x + _dot(oc_ref[...], wo_ref[gb:, :])
        xn_ref[...] = _rms(x, nw_ref[...]).astype(BF16)
        for c in chunks:
            lo = c * f_chunk
            if first_step:
                for cp in chunk_copies(c):
                    cp.wait()
            xn = xn_ref[...]
            g = _dot(xn, wg_ref[:, lo:lo + f_chunk].astype(BF16))
            u = _dot(xn, wu_ref[:, lo:lo + f_chunk].astype(BF16))
            h = (_silu(g) * u).astype(BF16)
            part = _dot(h, wd_ref[lo:lo + f_chunk, :].astype(BF16))
            if c == 0:
                acc_ref[...] = part
            else:
                acc_ref[...] += part
        y = x + 0.5 * acc_ref[...]
        if has_final:
            y = _rms(y, nf_ref[...])
        out_ref[...] = y

    @pl.when(i == 0)
    def _():
        run(True)

    @pl.when(i != 0)
    def _():
        run(False)


def _ffn(x, nw, wg, wu, wd, layer, mix=None, final_w=None, tm=512, f_chunk=256):
    m, d = x.shape
    d_ff = wg.shape[2]
    hbm = pl.BlockSpec(memory_space=pl.ANY)
    tm = min(tm, m)
    assert m % tm == 0 and d_ff % f_chunk == 0
    row = lambda i: (i, 0)
    in_specs = [pl.BlockSpec((tm, d), row)]
    args = [x]
    oc_transposed = False
    if mix is not None:
        oa, ob, oc, wo = mix
        oc_transposed = oc.ndim == 3
        if oc_transposed:
            tiles = oc.shape[2] // tm
            assert oc.shape[2] % tm == 0
            oc_spec = pl.BlockSpec((1, oc.shape[1], tm), lambda i: (i // tiles, 0, i % tiles))
        else:
            oc_spec = pl.BlockSpec((tm, oc.shape[1]), row)
        in_specs += [pl.BlockSpec((tm, oa.shape[1]), row), pl.BlockSpec((tm, ob.shape[1]), row),
                     oc_spec, _resident(wo.shape)]
        args += [oa, ob, oc, wo]
    in_specs += [_resident((1, d)), hbm, hbm, hbm]
    args += [nw.reshape(1, d), wg, wu, wd]
    if final_w is not None:
        in_specs.append(_resident((1, d)))
        args.append(final_w.reshape(1, d))
    body = functools.partial(_ffn_body, has_mix=mix is not None, oc_transposed=oc_transposed,
                             has_final=final_w is not None, f_chunk=f_chunk, layer=layer)
    return pl.pallas_call(
        body,
        name="ffn_mix" if mix is not None else "ffn",
        grid=(m // tm,),
        in_specs=in_specs,
        out_specs=pl.BlockSpec((tm, d), row),
        out_shape=jax.ShapeDtypeStruct((m, d), F32),
        scratch_shapes=[pltpu.VMEM((tm, d), BF16), pltpu.VMEM((tm, d), F32),
                        pltpu.VMEM(wg.shape[1:], F32), pltpu.VMEM(wu.shape[1:], F32),
                        pltpu.VMEM(wd.shape[1:], F32),
                        pltpu.SemaphoreType.DMA((3, d_ff // f_chunk))],
        compiler_params=_params("arbitrary"),
    )(*args)


_ROW_WIDTHS = (3 * GDN_WIDTH, GDN_WIDTH, 3 * SC_WIDTH, LANES, MOBA_WIDTH)


def _causal_conv(buf, x, w_ref):
    pad = SUBLANES
    n = x.shape[0]
    taps = w_ref.shape[0]
    buf[pad:pad + n, :] = x
    y = None
    for t in range(taps - 1):
        lo = pad - (taps - 1) + t
        term = buf[lo:lo + n, :] * w_ref[t:t + 1, :]
        y = term if y is None else y + term
    y = y + x * w_ref[taps - 1:taps, :]
    tail = buf[n:n + pad, :]
    buf[0:pad, :] = tail
    return y


def _l2_normalize(x):
    return x * lax.rsqrt(jnp.sum(x * x, axis=-1, keepdims=True) + NORM_EPS)


def _gates(ba, gp_ref):
    return jax.nn.sigmoid(ba), -jnp.exp(gp_ref[0:1, :]) * _softplus(ba + gp_ref[1:2, :])


def _inproj_body(x_ref, nw_ref, w_ref, wqv_ref, *refs, transposed_qv, kv_slot, n_aliased):
    out_refs = refs[n_aliased:]
    xn = _rms(x_ref[...], nw_ref[...]).astype(BF16)

    def put(ref, value, lead=()):
        if kv_slot is None:
            ref[lead or ...] = value
            return
        for s in range(ref.shape[0]):
            ref[(s,) + lead] = value if s == kv_slot else jnp.zeros(value.shape, value.dtype)

    lo = 0
    for n, ref in enumerate(out_refs[:len(_ROW_WIDTHS)]):
        width = ref.shape[-1]
        h = _dot(xn, w_ref[:, lo:lo + width])
        if n == len(_ROW_WIDTHS) - 1:
            put(ref, h)
        else:
            ref[...] = h
        lo += width
    q_ref, v_ref = out_refs[len(_ROW_WIDTHS):]
    mw = MOBA_WIDTH
    if transposed_qv:
        q_ref[0] = _dot_nt(wqv_ref[0:mw, :], xn)
        put(v_ref, _dot_nt(wqv_ref[mw:2 * mw, :], xn), lead=(0,))
    else:
        q_ref[...] = _dot(xn, wqv_ref[:, 0:mw])
        v_ref[...] = _dot(xn, wqv_ref[:, mw:2 * mw])


def _inproj(x, nw, w_rows, w_qv, *, seq_len=None, tm=1024, stack=None):
    m, d = x.shape
    tm = min(tm, m if seq_len is None else seq_len)
    assert m % tm == 0 and w_rows.shape[1] == sum(_ROW_WIDTHS)
    row = lambda i: (i, 0)
    out_specs = [pl.BlockSpec((tm, wd), row) for wd in _ROW_WIDTHS]
    out_shape = [jax.ShapeDtypeStruct((m, wd), F32) for wd in _ROW_WIDTHS]
    transposed_qv = seq_len is not None
    if transposed_qv:
        assert seq_len % tm == 0
        tiles = seq_len // tm
        spec = pl.BlockSpec((1, MOBA_WIDTH, tm), lambda i: (i // tiles, 0, i % tiles))
        shape = jax.ShapeDtypeStruct((m // seq_len, MOBA_WIDTH, seq_len), F32)
    else:
        spec = pl.BlockSpec((tm, MOBA_WIDTH), row)
        shape = jax.ShapeDtypeStruct((m, MOBA_WIDTH), F32)
    out_specs += [spec, spec]
    out_shape += [shape, shape]
    in_specs = [pl.BlockSpec((tm, d), row), _resident((1, d)), _resident(w_rows.shape),
                _resident(w_qv.shape)]
    args = [x, nw.reshape(1, d), w_rows, w_qv]
    aliases, kv_slot = {}, None
    if stack is not None:
        assert transposed_qv
        depth, layer, prev = stack
        k_out, v_out = len(_ROW_WIDTHS) - 1, len(_ROW_WIDTHS) + 1
        mw = MOBA_WIDTH
        out_shape[k_out] = jax.ShapeDtypeStruct((depth, m, mw), F32)
        out_shape[v_out] = jax.ShapeDtypeStruct((depth, m // seq_len, mw, seq_len), F32)
        if prev is None:
            kv_slot = layer
            out_specs[k_out] = pl.BlockSpec((depth, tm, mw), lambda i: (0, i, 0))
            out_specs[v_out] = pl.BlockSpec((depth, 1, mw, tm),
                                            lambda i: (0, i // tiles, 0, i % tiles))
        else:
            kv_slot = 0
            out_specs[k_out] = pl.BlockSpec((1, tm, mw), lambda i: (layer, i, 0))
            out_specs[v_out] = pl.BlockSpec((1, 1, mw, tm),
                                            lambda i: (layer, i // tiles, 0, i % tiles))
            aliases = {len(args): k_out, len(args) + 1: v_out}
            in_specs += [pl.BlockSpec(memory_space=pl.ANY)] * 2
            args += list(prev)
    return pl.pallas_call(
        functools.partial(_inproj_body, transposed_qv=transposed_qv, kv_slot=kv_slot,
                          n_aliased=len(aliases)),
        name="inproj",
        grid=(m // tm,),
        in_specs=in_specs,
        out_specs=out_specs,
        out_shape=out_shape,
        input_output_aliases=aliases,
        compiler_params=_params("arbitrary"),
    )(*args)


def _mixer_body(*refs, c, ng, has_state, row_lo, row_hi):
    refs = list(refs)
    qkv_ref, z_ref, ba_ref, sc_ref = refs[:4]
    refs = refs[4:]
    if has_state:
        qh_ref, sh_ref, s0_ref = refs[:3]
        refs = refs[3:]
    cw_ref, gp_ref, nw_ref, scw_ref = refs[:4]
    refs = refs[5:]
    oa_ref, ob_ref, st_ref, gt_ref = refs[:4]
    xbuf, gbuf, s_ref = refs[4:]
    n = pl.program_id(1)
    hd = GDN_HEAD_DIM
    gr = GROUP_ROWS
    sr = STACK_ROWS
    pad = SUBLANES
    seqs = gr // c
    shift = c.bit_length() - 1
    assert 1 << shift == c

    @pl.when(n == 0)
    def _():
        xbuf[:, 0:pad, :] = jnp.zeros((ng, pad, xbuf.shape[2]), F32)
        gbuf[:, 0:pad, :] = jnp.zeros((ng, pad, gbuf.shape[2]), F32)
        if has_state:
            s_ref[...] = s0_ref[0]
        else:
            s_ref[...] = jnp.zeros(s_ref.shape, F32)

    rows = lax.broadcasted_iota(jnp.int32, (gr, 1), 0) & (c - 1)
    is_hist = rows < row_lo
    live = jnp.where((rows >= row_lo) & (rows < row_hi), 1.0, 0.0)
    ri = lax.broadcasted_iota(jnp.int32, (gr, gr), 0)
    ci = lax.broadcasted_iota(jnp.int32, (gr, gr), 1)
    same = (ri >> shift) == (ci >> shift)
    cum_op = jnp.concatenate([jnp.where(same & (ri >= ci), 1.0, 0.0),
                              jnp.where(same, 1.0, 0.0)], axis=0)
    nh = GDN_HEADS
    tiles = sr // LANES
    band_tile = [h * gr // LANES for h in range(nh)]

    def bands(mat):
        return [mat[h * gr:(h + 1) * gr, band_tile[h] * LANES:(band_tile[h] + 1) * LANES]
                for h in range(nh)]

    def unband(pieces):
        zero = jnp.zeros((gr, LANES), BF16)
        return jnp.concatenate(
            [jnp.concatenate([p.astype(BF16) if t == band_tile[h] else zero
                              for t in range(tiles)], axis=1)
             for h, p in enumerate(pieces)], axis=0)

    incl, strict, eye = [], [], []
    for h in range(nh):
        rs = lax.broadcasted_iota(jnp.int32, (gr, LANES), 0) + h * gr
        cs = lax.broadcasted_iota(jnp.int32, (gr, LANES), 1) + band_tile[h] * LANES
        same_s = (rs >> shift) == (cs >> shift)
        incl.append(same_s & (rs >= cs))
        strict.append(same_s & (rs > cs))
        eye.append(jnp.where(rs == cs, 1.0, 0.0))

    def stack_heads(a, lo):
        return jnp.concatenate([a[:, lo + h * hd:lo + (h + 1) * hd] for h in range(GDN_HEADS)],
                               axis=0)

    def stack_cols(a, lo):
        return jnp.concatenate([a[:, lo + h:lo + h + 1] for h in range(GDN_HEADS)], axis=0)

    groups = range(ng)

    def front(g):
        x = qkv_ref[g]
        if has_state:
            x = jnp.where(is_hist, qh_ref[g], x)
        act = _silu(_causal_conv(xbuf.at[g], x, cw_ref))
        q = _l2_normalize(stack_heads(act, 0)) * (hd ** -0.5)
        k = _l2_normalize(stack_heads(act, GDN_WIDTH))
        v = stack_heads(act, 2 * GDN_WIDTH)

        sc = sc_ref[g]
        scw = SC_WIDTH
        gated = sc[:, scw:2 * scw] * sc[:, 0:scw]
        if has_state:
            gated = jnp.where(is_hist, sh_ref[g], gated)
        yb = _causal_conv(gbuf.at[g], gated, scw_ref)
        ob_ref[g] = (sc[:, 2 * scw:3 * scw] * yb).astype(ob_ref.dtype)
        gt_ref[g] = gated

        beta_all, g_all = _gates(ba_ref[g], gp_ref)
        beta_all, g_all = beta_all * live, g_all * live
        cum = _dot(cum_op, g_all, HI)
        gc_all, glast_all = cum[:gr], cum[gr:]
        beta = stack_cols(beta_all, 0)
        gc = stack_cols(gc_all, nh)
        eg = jnp.exp(gc)
        etail = jnp.exp(stack_cols(glast_all, nh) - gc)
        gc_row = jnp.broadcast_to(gc, (sr, LANES)).T[0:1, :]
        decay = []
        for h in range(nh):
            diff = (gc[h * gr:(h + 1) * gr]
                    - gc_row[:, band_tile[h] * LANES:(band_tile[h] + 1) * LANES])
            decay.append(jnp.where(incl[h], jnp.exp(jnp.where(incl[h], diff, 0.0)), 0.0))

        kb = k.astype(BF16)
        kk = bands(_dot_nt(kb, kb))
        m_strict = [jnp.where(strict[h], beta[h * gr:(h + 1) * gr] * kk[h] * decay[h], 0.0)
                    for h in range(nh)]
        return dict(q=q, k=k, v=v, kb=kb, beta=beta, eg=eg, etail=etail, decay=decay,
                    cd_all=jnp.exp(glast_all), m_strict=m_strict)

    fr = [front(g) for g in groups]

    invs = [[e - m for e, m in zip(eye, f["m_strict"])] for f in fr]
    pbs = [unband(f["m_strict"]) for f in fr]
    for _ in range(shift - 1):
        pbs = [unband(bands(_dot(pb, pb))) for pb in pbs]
        invs = [[i + d for i, d in zip(inv, bands(_dot(unband(inv), pb)))]
                for inv, pb in zip(invs, pbs)]

    def solve(f, inv):
        rhs = jnp.concatenate([f["k"] * (f["beta"] * f["eg"]), f["v"] * f["beta"]],
                              axis=-1).astype(BF16)
        sol = _dot(unband(inv), rhs)
        qk = bands(_dot_nt(f["q"].astype(BF16), f["kb"]))
        attn = unband([a * d for a, d in zip(qk, f["decay"])])
        return dict(w_k=sol[:, :hd], u_v=sol[:, hd:], attn=attn, qd=f["q"] * f["eg"],
                    kt=(f["k"] * f["etail"]).astype(BF16))

    sv = [solve(f, inv) for f, inv in zip(fr, invs)]

    pair_ids = [(h, s) for h in range(GDN_HEADS) for s in range(seqs)]

    def read_state(g, t):
        parts = []
        for h, s in pair_ids:
            r0 = h * gr + s * c
            s_old = s_ref[g, s * nh + h]
            lhs = jnp.concatenate([t["w_k"][r0:r0 + c], t["qd"][r0:r0 + c]], axis=0)
            res = _dot(lhs.astype(BF16), s_old.astype(BF16))
            parts.append((s_old, t["u_v"][r0:r0 + c] - res[:c], res[c:]))
        return parts

    rd = [read_state(g, t) for g, t in zip(groups, sv)]
    us = [jnp.concatenate([p[1] for p in parts], axis=0) for parts in rd]
    outs = [jnp.concatenate([p[2] for p in parts], axis=0) + _dot(t["attn"], u.astype(BF16))
            for parts, t, u in zip(rd, sv, us)]
    for g in groups:
        for (h, s), (s_old, u_p, _) in zip(pair_ids, rd[g]):
            r0 = h * gr + s * c
            cd = fr[g]["cd_all"][s * c:s * c + 1, nh + h:nh + h + 1]
            s_ref[g, s * nh + h] = s_old * cd + _dot_tn(sv[g]["kt"][r0:r0 + c],
                                                        u_p.astype(BF16))
    for g in groups:
        zs = stack_heads(z_ref[g], 0)
        o_n = _rms(outs[g], nw_ref[...]) * _silu(zs)
        oa_ref[g] = jnp.concatenate([o_n[h * gr:(h + 1) * gr] for h in range(GDN_HEADS)],
                                    axis=1).astype(oa_ref.dtype)

    @pl.when(n == pl.num_programs(1) - 1)
    def _():
        st_ref[0] = s_ref[...]


def _mixer(qkv, z, ba, sc, conv_w, gparams, norm_w, sc_conv_w, *, c, ng, layer, depth,
           prev_states=None, state=None, row_lo=0, row_hi=None):
    g_total, l, _ = qkv.shape
    gr = GROUP_ROWS
    assert l % gr == 0 and gr % c == 0 and g_total % ng == 0
    nc = l // gr
    pairs = GDN_HEADS * (gr // c)
    row_hi = c if row_hi is None else row_hi
    blk = lambda w: pl.BlockSpec((ng, gr, w), lambda i, n: (i, n, 0))
    in_specs = [blk(qkv.shape[2]), blk(z.shape[2]), blk(ba.shape[2]), blk(sc.shape[2])]
    args = [qkv, z, ba, sc]
    st_shape = (g_total, pairs, GDN_HEAD_DIM, GDN_HEAD_DIM)
    st_spec = pl.BlockSpec((1, ng) + st_shape[1:], lambda i, n: (layer, i, 0, 0, 0))
    if state is not None:
        assert nc == 1
        qh, sh, s0 = state
        assert s0.shape[1:] == st_shape
        in_specs += [pl.BlockSpec((ng, gr, qh.shape[2]), lambda i, n: (i, 0, 0)),
                     pl.BlockSpec((ng, gr, sh.shape[2]), lambda i, n: (i, 0, 0)), st_spec]
        args += [qh, sh, s0]
    const = lambda a: pl.BlockSpec(a.shape, lambda i, n: (0,) * a.ndim)
    norm_w = norm_w.reshape(1, -1)
    in_specs += [const(conv_w), const(gparams), const(norm_w), const(sc_conv_w)]
    args += [conv_w, gparams, norm_w, sc_conv_w]
    if prev_states is None:
        prev_states = jnp.zeros((depth,) + st_shape, F32)
    assert prev_states.shape == (depth,) + st_shape
    aliases = {len(args): 2}
    in_specs.append(pl.BlockSpec(memory_space=pl.ANY))
    args.append(prev_states)
    out_shape = [jax.ShapeDtypeStruct((g_total, l, GDN_WIDTH), BF16),
                 jax.ShapeDtypeStruct((g_total, l, SC_WIDTH), BF16),
                 jax.ShapeDtypeStruct((depth,) + st_shape, F32),
                 jax.ShapeDtypeStruct((g_total, l, SC_WIDTH), F32)]
    out_specs = [blk(GDN_WIDTH), blk(SC_WIDTH), st_spec, blk(SC_WIDTH)]
    body = functools.partial(_mixer_body, c=c, ng=ng, has_state=state is not None,
                             row_lo=row_lo, row_hi=row_hi)
    return pl.pallas_call(
        body,
        name="seq_mixer",
        grid=(g_total // ng, nc),
        in_specs=in_specs,
        out_specs=out_specs,
        out_shape=out_shape,
        input_output_aliases=aliases,
        scratch_shapes=[pltpu.VMEM((ng, gr + SUBLANES, qkv.shape[2]), F32),
                        pltpu.VMEM((ng, gr + SUBLANES, SC_WIDTH), F32),
                        pltpu.VMEM((ng,) + st_shape[1:], F32)],
        compiler_params=_params("arbitrary", "arbitrary"),
    )(*args)


def _select_blocks(gate, n_valid, axis=0):
    nb = gate.shape[axis]
    sub = lax.broadcasted_iota(jnp.int32, gate.shape, axis)
    gate = jnp.where(sub < n_valid, gate, NEG_INF)
    sel = jnp.zeros(gate.shape, F32)
    for j in range(nb):
        gj = gate[j:j + 1, :] if axis == 0 else gate[:, j:j + 1]
        beats = (gate > gj) | ((gate == gj) & (sub < j))
        cnt = jnp.sum(jnp.where(beats, 1.0, 0.0), axis=axis, keepdims=True)
        hit = (cnt < MOBA_TOPK) & (j < n_valid)
        sel = jnp.where((sub == j) & hit, 1.0, sel)
    return sel


_PV_ROWS = MOBA_HEAD_DIM + 16


def _moba_prompt_body(q_ref, k_ref, v_ref, o_ref, kmean_ref, kb_ref, vb_ref, sel_ref, *, nbat):
    i = pl.program_id(1)
    blk = MOBA_BLOCK
    hd = MOBA_HEAD_DIM
    w = MOBA_WIDTH
    l_seq = v_ref.shape[3]
    nb = l_seq // blk
    heads = range(MOBA_HEADS)
    head_rows = [slice(h * hd, (h + 1) * hd) for h in heads]
    chains = [(b, h) for b in range(nbat) for h in heads]

    @pl.when(i == 0)
    def _():
        ones = jnp.ones((_PV_ROWS - hd, blk), BF16)
        for b in range(nbat):
            for j in range(nb):
                kj = k_ref[0, b * l_seq + j * blk:b * l_seq + (j + 1) * blk, :]
                kmean_ref[b, j:j + 1, :] = jnp.mean(kj, axis=0, keepdims=True)
                kb_ref[b, j] = kj.astype(BF16)
                for h in heads:
                    vb_ref[b, j, h, 0:hd, :] = v_ref[0, b, head_rows[h],
                                                     j * blk:(j + 1) * blk].astype(BF16)
                    vb_ref[b, j, h, hd:_PV_ROWS, :] = ones

    feat = lax.broadcasted_iota(jnp.int32, (w, blk), 0)
    key_i = lax.broadcasted_iota(jnp.int32, (blk, blk), 0)
    qry_i = lax.broadcasted_iota(jnp.int32, (blk, blk), 1)
    causal = key_i <= qry_i
    qmb = []
    for b in range(nbat):
        qt = q_ref[b] * (hd ** -0.5)
        qms = [jnp.where((feat >= h * hd) & (feat < (h + 1) * hd), qt, 0.0) for h in heads]
        qmb.append(jnp.concatenate(qms, axis=1).astype(BF16))
        for h in heads:
            gate = _dot(kmean_ref[b], qms[h], HI)
            sel_ref[b, h] = _select_blocks(gate, i)

    def scores(j):
        s_all = [_dot(kb_ref[b, j], qmb[b]) for b in range(nbat)]
        return [s_all[b][:, h * blk:(h + 1) * blk] for b, h in chains]

    def attend(j, s_list, m_list):
        ps = [jnp.exp(s - m).astype(BF16) for s, m in zip(s_list, m_list)]
        return [_dot(vb_ref[b, j, h], p) for (b, h), p in zip(chains, ps)]

    s0 = [jnp.where(causal, s, NEG_INF) for s in scores(i)]
    m0 = [jnp.max(s, axis=0, keepdims=True) for s in s0]
    pv0 = attend(i, s0, m0)
    init = []
    for m, pv in zip(m0, pv0):
        init += [m, pv[hd:hd + 1], pv[:hd]]

    def past_block(j, carry):
        s_list = [jnp.where(sel_ref[b, h, pl.ds(j, 1), :] > 0.5, s, NEG_INF)
                  for (b, h), s in zip(chains, scores(j))]
        m_old = carry[0::3]
        m_new = [jnp.maximum(m, jnp.max(s, axis=0, keepdims=True))
                 for m, s in zip(m_old, s_list)]
        pv = attend(j, s_list, m_new)
        new = []
        for c in range(len(chains)):
            alpha = jnp.exp(m_old[c] - m_new[c])
            new += [m_new[c], alpha * carry[3 * c + 1] + pv[c][hd:hd + 1],
                    alpha * carry[3 * c + 2] + pv[c][:hd]]
        return tuple(new)

    final = lax.fori_loop(0, i, past_block, tuple(init))
    for c, (b, h) in enumerate(chains):
        o_ref[b, head_rows[h], :] = (final[3 * c + 2] / final[3 * c + 1]).astype(o_ref.dtype)


def _moba_prompt(qt, k, vt, layer, nbat=4):
    b, w, l = qt.shape
    blk = MOBA_BLOCK
    nbat = nbat if b % nbat == 0 else 1
    assert l % blk == 0
    nb = l // blk
    return pl.pallas_call(
        functools.partial(_moba_prompt_body, nbat=nbat),
        name="moba_prompt",
        grid=(b // nbat, nb),
        in_specs=[pl.BlockSpec((nbat, w, blk), lambda i, t: (i, 0, t)),
                  pl.BlockSpec((1, nbat * l, w), lambda i, t: (layer, i, 0)),
                  pl.BlockSpec((1, nbat, w, l), lambda i, t: (layer, i, 0, 0))],
        out_specs=pl.BlockSpec((nbat, w, blk), lambda i, t: (i, 0, t)),
        out_shape=jax.ShapeDtypeStruct((b, w, l), BF16),
        scratch_shapes=[pltpu.VMEM((nbat, nb, w), F32), pltpu.VMEM((nbat, nb, blk, w), BF16),
                        pltpu.VMEM((nbat, nb, MOBA_HEADS, _PV_ROWS, blk), BF16),
                        pltpu.VMEM((nbat, MOBA_HEADS, nb, blk), F32)],
        compiler_params=_params("arbitrary", "arbitrary"),
    )(qt, k, vt)


PAGE_SLOTS = 3
SEQS_PER_STEP = 2


def _moba_sample_body(pt_ref, q_ref, kn_ref, vn_ref, ck_ref, cv_ref, o_ref, kbuf, vbuf, sem,
                      *, n_pages, base, n_steps, ns):
    i = pl.program_id(0)
    hd = MOBA_HEAD_DIM
    r = S_ROWS
    w = MOBA_WIDTH
    ppb = MOBA_BLOCK // PAGE_SIZE
    nb = n_pages // ppb
    ahead = PAGE_SLOTS - 1
    seqs = range(ns)
    blocks = range(nb)

    def page_copies(step, slot):
        cps = []
        for s in seqs:
            for p in range(n_pages):
                row = base + pt_ref[step * ns + s, p]
                cps.append(pltpu.make_async_copy(ck_ref.at[row], kbuf.at[slot, s, p],
                                                 sem.at[0, slot]))
                cps.append(pltpu.make_async_copy(cv_ref.at[row], vbuf.at[slot, s, p],
                                                 sem.at[1, slot]))
        return cps

    def start_all(cps):
        for n, cp in enumerate(cps):
            cp.start(priority=n % 2)

    @pl.when(i == 0)
    def _():
        for d in range(ahead):
            start_all(page_copies(d, d))

    slot = lax.rem(i, PAGE_SLOTS)
    for cp in page_copies(i, slot):
        cp.wait()

    lane = lax.broadcasted_iota(jnp.int32, (r, w), 1)
    head_masks = [jnp.where((lane >= h * hd) & (lane < (h + 1) * hd), 1.0, 0.0)
                  for h in range(MOBA_HEADS)]
    col = lax.broadcasted_iota(jnp.int32, (w, LANES), 1)
    q_bd = [jnp.concatenate([q_ref[s] * (hd ** -0.5) * hm for hm in head_masks], axis=0)
            for s in seqs]
    q_bb = [x.astype(BF16) for x in q_bd]

    kp = [[kbuf[slot, s, p] for p in range(n_pages)] for s in seqs]
    s_pages = [[_dot(q_bb[s], x.astype(BF16)) for x in kp[s]] for s in seqs]
    kmean_t = []
    for s in seqs:
        km = jnp.zeros((w, LANES), F32)
        for j in blocks:
            ksum = kp[s][j * ppb]
            for pp in range(1, ppb):
                ksum = ksum + kp[s][j * ppb + pp]
            km = jnp.where(col == j, jnp.sum(ksum, axis=1, keepdims=True) * (1.0 / MOBA_BLOCK),
                           km)
        kmean_t.append(km)
    s_blk = [[jnp.concatenate(s_pages[s][j * ppb:(j + 1) * ppb], axis=-1) for j in blocks]
             for s in seqs]
    m_cols = [[jnp.max(x, axis=1, keepdims=True) for x in s_blk[s]] for s in seqs]
    e_blk = [[jnp.exp(x - m) for x, m in zip(s_blk[s], m_cols[s])] for s in seqs]
    l_cols = [[jnp.sum(e, axis=1, keepdims=True) for e in e_blk[s]] for s in seqs]
    accs = []
    for s in seqs:
        row_accs = []
        for j in blocks:
            eb = e_blk[s][j].astype(BF16)
            acc = None
            for pp in range(ppb):
                vp = vbuf[slot, s, j * ppb + pp].astype(BF16)
                part = _dot_nt(eb[:, pp * PAGE_SIZE:(pp + 1) * PAGE_SIZE], vp)
                acc = part if acc is None else acc + part
            row_accs.append(acc)
        accs.append(row_accs)

    nxt = jnp.minimum(i + ahead, n_steps - 1)
    start_all(page_copies(nxt, lax.rem(i + ahead, PAGE_SLOTS)))

    gates = [_dot(q_bd[s], kmean_t[s], HI)[:, :nb] for s in seqs]
    sels = [_select_blocks(g, nb, axis=1) > 0.5 for g in gates]
    qt = lax.broadcasted_iota(jnp.int32, (MOBA_HEADS * r, r), 0) & (r - 1)
    kt = lax.broadcasted_iota(jnp.int32, (MOBA_HEADS * r, r), 1)
    own = (kt >= S_LO) & (kt <= qt)
    for s in seqs:
        sel = sels[s]
        m_all = jnp.concatenate(m_cols[s], axis=1)
        l_all = jnp.concatenate(l_cols[s], axis=1)
        s_own = _dot_nt(q_bb[s], kn_ref[s].astype(BF16))
        m_tot = jnp.maximum(jnp.max(jnp.where(sel, m_all, NEG_INF), axis=1, keepdims=True),
                            jnp.max(jnp.where(own, s_own, NEG_INF), axis=1, keepdims=True))
        wj = jnp.where(sel, jnp.exp(jnp.where(sel, m_all - m_tot, 0.0)), 0.0)
        p_own = jnp.where(own, jnp.exp(jnp.where(own, s_own - m_tot, 0.0)), 0.0)
        l_tot = (jnp.sum(wj * l_all, axis=1, keepdims=True)
                 + jnp.sum(p_own, axis=1, keepdims=True))
        acc = _dot(p_own.astype(BF16), vn_ref[s].astype(BF16))
        for j in blocks:
            acc = acc + wj[:, j:j + 1] * accs[s][j]
        o_bd = acc / l_tot
        o = None
        for h in range(MOBA_HEADS):
            part = o_bd[h * r:(h + 1) * r, :] * head_masks[h]
            o = part if o is None else o + part
        o_ref[s] = o.astype(o_ref.dtype)

    @pl.when(i == n_steps - 1)
    def _():
        for d in range(1, PAGE_SLOTS):
            for cp in page_copies(n_steps - 1, lax.rem(i + d, PAGE_SLOTS)):
                cp.wait()


def _moba_sample(q, k_new, v_new, cache_kt, cache_vt, page_table, base):
    b, r, w = q.shape
    page = cache_kt.shape[2]
    n_pages = page_table.shape[1]
    ns = SEQS_PER_STEP if b % SEQS_PER_STEP == 0 else 1
    n_steps = b // ns
    assert page == PAGE_SIZE and r == S_ROWS and cache_kt.shape[1] == w
    assert n_steps >= PAGE_SLOTS
    tile = pl.BlockSpec((ns, r, w), lambda i, pt: (i, 0, 0))
    hbm = pl.BlockSpec(memory_space=pl.ANY)
    grid_spec = pltpu.PrefetchScalarGridSpec(
        num_scalar_prefetch=1,
        grid=(n_steps,),
        in_specs=[tile, tile, tile, hbm, hbm],
        out_specs=tile,
        scratch_shapes=[pltpu.VMEM((PAGE_SLOTS, ns, n_pages, w, page), F32),
                        pltpu.VMEM((PAGE_SLOTS, ns, n_pages, w, page), F32),
                        pltpu.SemaphoreType.DMA((2, PAGE_SLOTS))],
    )
    body = functools.partial(_moba_sample_body, n_pages=n_pages, base=base, n_steps=n_steps,
                             ns=ns)
    return pl.pallas_call(
        body,
        name="moba_sample",
        grid_spec=grid_spec,
        out_shape=jax.ShapeDtypeStruct((b, r, w), BF16),
        compiler_params=_params("arbitrary"),
    )(page_table, q, k_new, v_new, cache_kt, cache_vt)


def _trunk(x, weights, ffn_w, norm_final, *, sample=None):
    b, l, d = x.shape
    depth = len(weights)
    x = x.reshape(b * l, d)
    gr = GROUP_ROWS
    states = []
    gdn_states = kv = None
    for layer, wts in enumerate(weights):
        x = _ffn(x, wts["n_ffn1"], *ffn_w[0], layer)
        if sample is None:
            qkv, z, sc, ba, mk, mq, mv = _inproj(x, wts["n_mix"], wts["w_rows"], wts["w_qv_t"],
                                                 seq_len=l, stack=(depth, layer, kv))
            kv = (mk, mv)
            grp = lambda a: a.reshape(b, l, a.shape[-1])
            o_a, o_b, gdn_states, gated = _mixer(
                grp(qkv), grp(z), grp(ba), grp(sc), wts["gdn_conv_w"], wts["gparams"],
                wts["gdn_norm_w"], wts["sc_conv_w"], c=GDN_CHUNK, ng=8 if b % 8 == 0 else 1,
                layer=layer, depth=depth, prev_states=gdn_states)
            o_c = _moba_prompt(mq, mk, mv, layer)
        else:
            qkv, z, sc, ba, mk, mq, mv = _inproj(x, wts["n_mix"], wts["w_rows"], wts["w_qv"])
            ngroups = b * l // gr
            grp = lambda a: a.reshape(ngroups, gr, a.shape[-1])
            state = (grp(sample["gdn_conv"][layer]), grp(sample["sconv"][layer]), sample["gdn"])
            o_a, o_b, gdn_states, gated = _mixer(
                grp(qkv), grp(z), grp(ba), grp(sc), wts["gdn_conv_w"], wts["gparams"],
                wts["gdn_norm_w"], wts["sc_conv_w"], c=l, ng=2 if ngroups % 2 == 0 else 1,
                layer=layer, depth=depth, prev_states=gdn_states, state=state,
                row_lo=S_LO, row_hi=S_HI)
            seq = lambda a: a.reshape(b, l, a.shape[-1])
            o_c = _moba_sample(seq(mq), seq(mk), seq(mv), sample["cache_kt"], sample["cache_vt"],
                               sample["page_table"], layer * sample["n_pool"])
            o_c = o_c.reshape(b * l, -1)
        mix = (o_a.reshape(b * l, -1), o_b.reshape(b * l, -1), o_c, wts["w_out"])
        x = _ffn(x, wts["n_ffn2"], *ffn_w[1], layer, mix=mix,
                 final_w=norm_final if layer == depth - 1 else None)
        states.append(dict(qkv=qkv.reshape(b, l, -1), gated=gated.reshape(b, l, -1)))
        if sample is not None:
            states[-1].update(k=mk.reshape(b, l, -1), v=mv)
    return x.reshape(b, l, d), states, gdn_states, kv


def kernel(x_prompt, x_sample, state_gdn, state_gdn_conv, state_sconv, cache_k, cache_v,
           page_table, norm_ffn1, ffn1_w_gate, ffn1_w_up, ffn1_w_down, norm_mix, w_in,
           gdn_conv_w, gdn_a_log, gdn_dt_bias, gdn_norm_w, sc_conv_w, w_out,
           norm_ffn2, ffn2_w_gate, ffn2_w_up, ffn2_w_down, norm_final):
    depth = w_in.shape[0]
    gw, sw, mw = GDN_WIDTH, SC_WIDTH, MOBA_WIDTH
    nh = GDN_HEADS
    mh, md = MOBA_HEADS, MOBA_HEAD_DIM

    weights = []
    for la in range(depth):
        w = w_in[la]
        ba_off = 4 * gw
        sc_off = ba_off + 2 * nh
        mo_off = sc_off + 3 * sw
        w_rows = jnp.concatenate(
            [w[:, :ba_off], w[:, sc_off:mo_off],
             jnp.pad(w[:, ba_off:sc_off], ((0, 0), (0, LANES - 2 * nh))),
             w[:, mo_off + mw:mo_off + 2 * mw]], axis=1).astype(BF16)
        w_qv = jnp.concatenate([w[:, mo_off:mo_off + mw], w[:, mo_off + 2 * mw:mo_off + 3 * mw]],
                               axis=1).astype(BF16)
        gparams = jnp.zeros((SUBLANES, LANES), F32)
        gparams = gparams.at[0, nh:2 * nh].set(gdn_a_log[la].astype(F32))
        gparams = gparams.at[1, nh:2 * nh].set(gdn_dt_bias[la].astype(F32))
        weights.append(dict(
            n_ffn1=norm_ffn1[la],
            n_mix=norm_mix[la], w_rows=w_rows, w_qv=w_qv, w_qv_t=w_qv.T,
            gdn_conv_w=gdn_conv_w[la], gparams=gparams, gdn_norm_w=gdn_norm_w[la],
            sc_conv_w=sc_conv_w[la], w_out=_to_bf16(w_out, la),
            n_ffn2=norm_ffn2[la]))
    ffn_w = ((ffn1_w_gate, ffn1_w_up, ffn1_w_down), (ffn2_w_gate, ffn2_w_up, ffn2_w_down))

    pb, pl_, _ = x_prompt.shape
    y_p, st_p, p_gdn, (p_k, p_v) = _trunk(x_prompt, weights, ffn_w, norm_final)
    hist = GDN_CONV - 1
    p_conv = jnp.stack([s["qkv"][:, -hist:] for s in st_p])
    p_sconv = jnp.stack([s["gated"][:, -(SC_CONV - 1):] for s in st_p])
    p_k = p_k.reshape(depth, pb, pl_, mh, md)
    p_v = p_v.reshape(depth, pb, mh, md, pl_).transpose(0, 1, 4, 2, 3)

    db, dl, d = x_sample.shape
    assert dl == S_HI - S_LO and (db * S_ROWS) % GROUP_ROWS == 0
    n_pool = cache_k.shape[1]
    tile_pad = lambda a, lo: jnp.pad(a, ((0, 0),) * (a.ndim - 2)
                                     + ((lo, S_ROWS - lo - a.shape[-2]), (0, 0)))
    seqs_per_group = GROUP_ROWS // S_ROWS
    paged_t = lambda c: c.transpose(0, 1, 3, 4, 2).reshape(depth * n_pool, mw, PAGE_SIZE)
    sample = dict(
        gdn=state_gdn.reshape(depth, db // seqs_per_group, seqs_per_group * nh,
                              GDN_HEAD_DIM, GDN_HEAD_DIM),
        gdn_conv=tile_pad(state_gdn_conv, 0),
        sconv=tile_pad(state_sconv, S_LO - (SC_CONV - 1)),
        cache_kt=paged_t(cache_k), cache_vt=paged_t(cache_v), n_pool=n_pool,
        page_table=page_table)
    y_s, st_s, s_gdn, _ = _trunk(tile_pad(x_sample, S_LO), weights, ffn_w, norm_final,
                                 sample=sample)
    y_s = y_s[:, S_LO:S_HI]
    s_gdn = s_gdn.reshape(state_gdn.shape)
    s_conv = jnp.stack([s["qkv"][:, S_HI - hist:S_HI] for s in st_s])
    s_sconv = jnp.stack([s["gated"][:, S_HI - (SC_CONV - 1):S_HI] for s in st_s])
    s_k = jnp.stack([s["k"][:, S_LO:S_HI] for s in st_s]).reshape(depth, db, dl, mh, md)
    s_v = jnp.stack([s["v"].reshape(db, S_ROWS, mw)[:, S_LO:S_HI] for s in st_s]).reshape(
        depth, db, dl, mh, md)
    return (y_p, y_s, p_gdn, p_conv, p_sconv, p_k, p_v, s_gdn, s_conv, s_sconv, s_k, s_v)
```

```python
import functools

import jax
import jax.numpy as jnp
from jax import lax
from jax.experimental import pallas as pl
from jax.experimental.pallas import tpu as pltpu

NORM_EPS = 1e-6
NEG_INF = -1e30

GDN_HEADS = 4
GDN_HEAD_DIM = 128
GDN_WIDTH = GDN_HEADS * GDN_HEAD_DIM
GDN_CONV = 4
GDN_CHUNK = 64
SC_WIDTH = 256
SC_CONV = 3
MOBA_HEADS = 4
MOBA_HEAD_DIM = 64
MOBA_WIDTH = MOBA_HEADS * MOBA_HEAD_DIM
MOBA_BLOCK = 256
MOBA_TOPK = 3
PAGE_SIZE = 128

LANES = 128
SUBLANES = 8
VMEM_LIMIT_BYTES = 56 * 1024 * 1024

S_ROWS = SUBLANES
S_LO = GDN_CONV - 1
S_HI = S_LO + 4

GROUP_ROWS = GDN_CHUNK
STACK_ROWS = GDN_HEADS * GROUP_ROWS

F32 = jnp.float32
BF16 = jnp.bfloat16
HI = lax.Precision.HIGHEST


def _dot(a, b, precision=None):
    return jnp.dot(a, b, preferred_element_type=F32, precision=precision)


def _dot_nt(a, b, precision=None):
    return lax.dot_general(a, b, (((1,), (1,)), ((), ())),
                           preferred_element_type=F32, precision=precision)


def _dot_tn(a, b, precision=None):
    return lax.dot_general(a, b, (((0,), (0,)), ((), ())),
                           preferred_element_type=F32, precision=precision)


def _rms(x, w):
    return x * lax.rsqrt(jnp.mean(x * x, axis=-1, keepdims=True) + NORM_EPS) * w


def _silu(x):
    return x * jax.nn.sigmoid(x)


def _softplus(x):
    return jnp.maximum(x, 0.0) + jnp.log1p(jnp.exp(-jnp.abs(x)))


def _resident(shape):
    nd = len(shape)
    return pl.BlockSpec(shape, lambda *_: (0,) * nd, pipeline_mode=pl.Buffered(1))


def _params(*semantics):
    return pltpu.CompilerParams(dimension_semantics=semantics,
                                vmem_limit_bytes=VMEM_LIMIT_BYTES)


def _cast_body(w_ref, o_ref):
    o_ref[...] = w_ref[0].astype(o_ref.dtype)


def _to_bf16(w, layer, rows=256):
    _, r, c = w.shape
    rows = min(rows, r)
    assert r % rows == 0
    return pl.pallas_call(
        _cast_body,
        name="to_bf16",
        grid=(r // rows,),
        in_specs=[pl.BlockSpec((1, rows, c), lambda i: (layer, i, 0))],
        out_specs=pl.BlockSpec((rows, c), lambda i: (i, 0)),
        out_shape=jax.ShapeDtypeStruct((r, c), BF16),
        compiler_params=_params("arbitrary"),
    )(w)


def _ffn_body(*refs, has_mix, oc_transposed, has_final, f_chunk, layer):
    refs = list(refs)
    x_ref = refs.pop(0)
    if has_mix:
        oa_ref, ob_ref, oc_ref, wo_ref = refs[:4]
        refs = refs[4:]
    nw_ref, wg_hbm, wu_hbm, wd_hbm = refs[:4]
    refs = refs[4:]
    if has_final:
        nf_ref = refs.pop(0)
    out_ref, xn_ref, acc_ref, wg_ref, wu_ref, wd_ref, wsem = refs
    i = pl.program_id(0)
    chunks = range(wg_ref.shape[1] // f_chunk)

    def chunk_copies(c):
        cols = pl.ds(c * f_chunk, f_chunk)
        return [pltpu.make_async_copy(wg_hbm.at[layer, :, cols], wg_ref.at[:, cols], wsem.at[0, c]),
                pltpu.make_async_copy(wu_hbm.at[layer, :, cols], wu_ref.at[:, cols], wsem.at[1, c]),
                pltpu.make_async_copy(wd_hbm.at[layer, cols, :], wd_ref.at[cols, :], wsem.at[2, c])]

    @pl.when(i == 0)
    def _():
        for c in chunks:
            for cp in chunk_copies(c):
                cp.start()

    def run(first_step):
        x = x_ref[...]
        if has_mix:
            ga = oa_ref.shape[1]
            gb = ga + ob_ref.shape[1]
            x = x + _dot(oa_ref[...], wo_ref[0:ga, :]) + _dot(ob_ref[...], wo_ref[ga:gb, :])
            if oc_transposed:
                x = x + _dot_tn(oc_ref[0], wo_ref[gb:, :])
            else:
                x = x + _dot(oc_ref[...], wo_ref[gb:, :])
        xn_ref[...] = _rms(x, nw_ref[...]).astype(BF16)
        for c in chunks:
            lo = c * f_chunk
            if first_step:
                for cp in chunk_copies(c):
                    cp.wait()
            xn = xn_ref[...]
            g = _dot(xn, wg_ref[:, lo:lo + f_chunk].astype(BF16))
            u = _dot(xn, wu_ref[:, lo:lo + f_chunk].astype(BF16))
            h = (_silu(g) * u).astype(BF16)
            part = _dot(h, wd_ref[lo:lo + f_chunk, :].astype(BF16))
            if c == 0:
                acc_ref[...] = part
            else:
                acc_ref[...] += part
        y = x + 0.5 * acc_ref[...]
        if has_final:
            y = _rms(y, nf_ref[...])
        out_ref[...] = y

    @pl.when(i == 0)
    def _():
        run(True)

    @pl.when(i != 0)
    def _():
        run(False)


def _ffn(x, nw, wg, wu, wd, layer, mix=None, final_w=None, tm=512, f_chunk=256):
    m, d = x.shape
    d_ff = wg.shape[2]
    hbm = pl.BlockSpec(memory_space=pl.ANY)
    tm = min(tm, m)
    assert m % tm == 0 and d_ff % f_chunk == 0
    row = lambda i: (i, 0)
    in_specs = [pl.BlockSpec((tm, d), row)]
    args = [x]
    oc_transposed = False
    if mix is not None:
        oa, ob, oc, wo = mix
        oc_transposed = oc.ndim == 3
        if oc_transposed:
            tiles = oc.shape[2] // tm
            assert oc.shape[2] % tm == 0
            oc_spec = pl.BlockSpec((1, oc.shape[1], tm), lambda i: (i // tiles, 0, i % tiles))
        else:
            oc_spec = pl.BlockSpec((tm, oc.shape[1]), row)
        in_specs += [pl.BlockSpec((tm, oa.shape[1]), row), pl.BlockSpec((tm, ob.shape[1]), row),
                     oc_spec, _resident(wo.shape)]
        args += [oa, ob, oc, wo]
    in_specs += [_resident((1, d)), hbm, hbm, hbm]
    args += [nw.reshape(1, d), wg, wu, wd]
    if final_w is not None:
        in_specs.append(_resident((1, d)))
        args.append(final_w.reshape(1, d))
    body = functools.partial(_ffn_body, has_mix=mix is not None, oc_transposed=oc_transposed,
                             has_final=final_w is not None, f_chunk=f_chunk, layer=layer)
    return pl.pallas_call(
        body,
        name="ffn_mix" if mix is not None else "ffn",
        grid=(m // tm,),
        in_specs=in_specs,
        out_specs=pl.BlockSpec((tm, d), row),
        out_shape=jax.ShapeDtypeStruct((m, d), F32),
        scratch_shapes=[pltpu.VMEM((tm, d), BF16), pltpu.VMEM((tm, d), F32),
                        pltpu.VMEM(wg.shape[1:], F32), pltpu.VMEM(wu.shape[1:], F32),
                        pltpu.VMEM(wd.shape[1:], F32),
                        pltpu.SemaphoreType.DMA((3, d_ff // f_chunk))],
        compiler_params=_params("arbitrary"),
    )(*args)


_ROW_WIDTHS = (3 * GDN_WIDTH, GDN_WIDTH, 3 * SC_WIDTH, LANES, MOBA_WIDTH)


def _causal_conv(buf, x, w_ref):
    pad = SUBLANES
    n = x.shape[0]
    taps = w_ref.shape[0]
    buf[pad:pad + n, :] = x
    y = None
    for t in range(taps - 1):
        lo = pad - (taps - 1) + t
        term = buf[lo:lo + n, :] * w_ref[t:t + 1, :]
        y = term if y is None else y + term
    y = y + x * w_ref[taps - 1:taps, :]
    tail = buf[n:n + pad, :]
    buf[0:pad, :] = tail
    return y


def _l2_normalize(x):
    return x * lax.rsqrt(jnp.sum(x * x, axis=-1, keepdims=True) + NORM_EPS)


def _gates(ba, gp_ref):
    return jax.nn.sigmoid(ba), -jnp.exp(gp_ref[0:1, :]) * _softplus(ba + gp_ref[1:2, :])


def _inproj_body(x_ref, nw_ref, w_ref, wqv_ref, *refs, transposed_qv, kv_slot, n_aliased):
    out_refs = refs[n_aliased:]
    xn = _rms(x_ref[...], nw_ref[...]).astype(BF16)

    def put(ref, value, lead=()):
        if kv_slot is None:
            ref[lead or ...] = value
            return
        for s in range(ref.shape[0]):
            ref[(s,) + lead] = value if s == kv_slot else jnp.zeros(value.shape, value.dtype)

    lo = 0
    for ref in out_refs[:len(_ROW_WIDTHS)]:
        width = ref.shape[-1]
        ref[...] = _dot(xn, w_ref[:, lo:lo + width])
        lo += width
    q_ref, v_ref = out_refs[len(_ROW_WIDTHS):len(_ROW_WIDTHS) + 2]
    mw = MOBA_WIDTH
    if transposed_qv:
        q_ref[0] = _dot_nt(wqv_ref[0:mw, :], xn)
        put(v_ref, _dot_nt(wqv_ref[mw:2 * mw, :], xn), lead=(0,))
        if kv_slot is not None:
            put(out_refs[-1], _dot_nt(wqv_ref[2 * mw:3 * mw, :], xn), lead=(0,))
    else:
        q_ref[...] = _dot(xn, wqv_ref[:, 0:mw])
        v_ref[...] = _dot(xn, wqv_ref[:, mw:2 * mw])


def _inproj(x, nw, w_rows, w_qv, *, seq_len=None, tm=1024, stack=None):
    m, d = x.shape
    tm = min(tm, m if seq_len is None else seq_len)
    assert m % tm == 0 and w_rows.shape[1] == sum(_ROW_WIDTHS)
    row = lambda i: (i, 0)
    out_specs = [pl.BlockSpec((tm, wd), row) for wd in _ROW_WIDTHS]
    out_shape = [jax.ShapeDtypeStruct((m, wd), F32) for wd in _ROW_WIDTHS]
    transposed_qv = seq_len is not None
    if transposed_qv:
        assert seq_len % tm == 0
        tiles = seq_len // tm
        spec = pl.BlockSpec((1, MOBA_WIDTH, tm), lambda i: (i // tiles, 0, i % tiles))
        shape = jax.ShapeDtypeStruct((m // seq_len, MOBA_WIDTH, seq_len), F32)
    else:
        spec = pl.BlockSpec((tm, MOBA_WIDTH), row)
        shape = jax.ShapeDtypeStruct((m, MOBA_WIDTH), F32)
    out_specs += [spec, spec]
    out_shape += [shape, shape]
    in_specs = [pl.BlockSpec((tm, d), row), _resident((1, d)), _resident(w_rows.shape),
                _resident(w_qv.shape)]
    args = [x, nw.reshape(1, d), w_rows, w_qv]
    aliases, kv_slot = {}, None
    if stack is not None:
        assert transposed_qv
        depth, layer, prev = stack
        v_out, k_out = len(_ROW_WIDTHS) + 1, len(_ROW_WIDTHS) + 2
        mw = MOBA_WIDTH
        stacked = jax.ShapeDtypeStruct((depth, m // seq_len, mw, seq_len), F32)
        if prev is None:
            kv_slot = layer
            spec = pl.BlockSpec((depth, 1, mw, tm), lambda i: (0, i // tiles, 0, i % tiles))
        else:
            kv_slot = 0
            spec = pl.BlockSpec((1, 1, mw, tm), lambda i: (layer, i // tiles, 0, i % tiles))
            aliases = {len(args): v_out, len(args) + 1: k_out}
            in_specs += [pl.BlockSpec(memory_space=pl.ANY)] * 2
            args += list(prev)
        out_shape[v_out] = stacked
        out_specs[v_out] = spec
        out_shape.append(stacked)
        out_specs.append(spec)
    return pl.pallas_call(
        functools.partial(_inproj_body, transposed_qv=transposed_qv, kv_slot=kv_slot,
                          n_aliased=len(aliases)),
        name="inproj",
        grid=(m // tm,),
        in_specs=in_specs,
        out_specs=out_specs,
        out_shape=out_shape,
        input_output_aliases=aliases,
        compiler_params=_params("arbitrary"),
    )(*args)


def _mixer_body(*refs, c, ng, has_state, row_lo, row_hi):
    refs = list(refs)
    qkv_ref, z_ref, ba_ref, sc_ref = refs[:4]
    refs = refs[4:]
    if has_state:
        qh_ref, sh_ref, s0_ref = refs[:3]
        refs = refs[3:]
    cw_ref, gp_ref, nw_ref, scw_ref = refs[:4]
    refs = refs[5:]
    oa_ref, ob_ref, st_ref, gt_ref = refs[:4]
    xbuf, gbuf, s_ref = refs[4:]
    n = pl.program_id(1)
    hd = GDN_HEAD_DIM
    gr = GROUP_ROWS
    sr = STACK_ROWS
    pad = SUBLANES
    seqs = gr // c
    shift = c.bit_length() - 1
    assert 1 << shift == c

    @pl.when(n == 0)
    def _():
        xbuf[:, 0:pad, :] = jnp.zeros((ng, pad, xbuf.shape[2]), F32)
        gbuf[:, 0:pad, :] = jnp.zeros((ng, pad, gbuf.shape[2]), F32)
        if has_state:
            s_ref[...] = s0_ref[0]
        else:
            s_ref[...] = jnp.zeros(s_ref.shape, F32)

    rows = lax.broadcasted_iota(jnp.int32, (gr, 1), 0) & (c - 1)
    is_hist = rows < row_lo
    live = jnp.where((rows >= row_lo) & (rows < row_hi), 1.0, 0.0)
    ri = lax.broadcasted_iota(jnp.int32, (gr, gr), 0)
    ci = lax.broadcasted_iota(jnp.int32, (gr, gr), 1)
    same = (ri >> shift) == (ci >> shift)
    cum_op = jnp.concatenate([jnp.where(same & (ri >= ci), 1.0, 0.0),
                              jnp.where(same, 1.0, 0.0)], axis=0)
    nh = GDN_HEADS
    tiles = sr // LANES
    band_tile = [h * gr // LANES for h in range(nh)]

    def bands(mat):
        return [mat[h * gr:(h + 1) * gr, band_tile[h] * LANES:(band_tile[h] + 1) * LANES]
                for h in range(nh)]

    def unband(pieces):
        zero = jnp.zeros((gr, LANES), BF16)
        return jnp.concatenate(
            [jnp.concatenate([p.astype(BF16) if t == band_tile[h] else zero
                              for t in range(tiles)], axis=1)
             for h, p in enumerate(pieces)], axis=0)

    incl, strict, eye = [], [], []
    for h in range(nh):
        rs = lax.broadcasted_iota(jnp.int32, (gr, LANES), 0) + h * gr
        cs = lax.broadcasted_iota(jnp.int32, (gr, LANES), 1) + band_tile[h] * LANES
        same_s = (rs >> shift) == (cs >> shift)
        incl.append(same_s & (rs >= cs))
        strict.append(same_s & (rs > cs))
        eye.append(jnp.where(rs == cs, 1.0, 0.0))

    def stack_heads(a, lo):
        return jnp.concatenate([a[:, lo + h * hd:lo + (h + 1) * hd] for h in range(GDN_HEADS)],
                               axis=0)

    def stack_cols(a, lo):
        return jnp.concatenate([a[:, lo + h:lo + h + 1] for h in range(GDN_HEADS)], axis=0)

    groups = range(ng)

    def front(g):
        x = qkv_ref[g]
        if has_state:
            x = jnp.where(is_hist, qh_ref[g], x)
        act = _silu(_causal_conv(xbuf.at[g], x, cw_ref))
        q = _l2_normalize(stack_heads(act, 0)) * (hd ** -0.5)
        k = _l2_normalize(stack_heads(act, GDN_WIDTH))
        v = stack_heads(act, 2 * GDN_WIDTH)

        sc = sc_ref[g]
        scw = SC_WIDTH
        gated = sc[:, scw:2 * scw] * sc[:, 0:scw]
        if has_state:
            gated = jnp.where(is_hist, sh_ref[g], gated)
        yb = _causal_conv(gbuf.at[g], gated, scw_ref)
        ob_ref[g] = (sc[:, 2 * scw:3 * scw] * yb).astype(ob_ref.dtype)
        gt_ref[g] = gated

        beta_all, g_all = _gates(ba_ref[g], gp_ref)
        beta_all, g_all = beta_all * live, g_all * live
        cum = _dot(cum_op, g_all, HI)
        gc_all, glast_all = cum[:gr], cum[gr:]
        beta = stack_cols(beta_all, 0)
        gc = stack_cols(gc_all, nh)
        eg = jnp.exp(gc)
        etail = jnp.exp(stack_cols(glast_all, nh) - gc)
        gc_row = jnp.broadcast_to(gc, (sr, LANES)).T[0:1, :]
        decay = []
        for h in range(nh):
            diff = (gc[h * gr:(h + 1) * gr]
                    - gc_row[:, band_tile[h] * LANES:(band_tile[h] + 1) * LANES])
            decay.append(jnp.where(incl[h], jnp.exp(jnp.where(incl[h], diff, 0.0)), 0.0))

        kb = k.astype(BF16)
        kk = bands(_dot_nt(kb, kb))
        m_strict = [jnp.where(strict[h], beta[h * gr:(h + 1) * gr] * kk[h] * decay[h], 0.0)
                    for h in range(nh)]
        return dict(q=q, k=k, v=v, kb=kb, beta=beta, eg=eg, etail=etail, decay=decay,
                    cd_all=jnp.exp(glast_all), m_strict=m_strict)

    fr = [front(g) for g in groups]

    invs = [[e - m for e, m in zip(eye, f["m_strict"])] for f in fr]
    pbs = [unband(f["m_strict"]) for f in fr]
    for _ in range(shift - 1):
        pbs = [unband(bands(_dot(pb, pb))) for pb in pbs]
        invs = [[i + d for i, d in zip(inv, bands(_dot(unband(inv), pb)))]
                for inv, pb in zip(invs, pbs)]

    def solve(f, inv):
        rhs = jnp.concatenate([f["k"] * (f["beta"] * f["eg"]), f["v"] * f["beta"]],
                              axis=-1).astype(BF16)
        sol = _dot(unband(inv), rhs)
        qk = bands(_dot_nt(f["q"].astype(BF16), f["kb"]))
        attn = unband([a * d for a, d in zip(qk, f["decay"])])
        return dict(w_k=sol[:, :hd], u_v=sol[:, hd:], attn=attn, qd=f["q"] * f["eg"],
                    kt=(f["k"] * f["etail"]).astype(BF16))

    sv = [solve(f, inv) for f, inv in zip(fr, invs)]

    pair_ids = [(h, s) for h in range(GDN_HEADS) for s in range(seqs)]

    def read_state(g, t):
        parts = []
        for h, s in pair_ids:
            r0 = h * gr + s * c
            s_old = s_ref[g, s * nh + h]
            lhs = jnp.concatenate([t["w_k"][r0:r0 + c], t["qd"][r0:r0 + c]], axis=0)
            res = _dot(lhs.astype(BF16), s_old.astype(BF16))
            parts.append((s_old, t["u_v"][r0:r0 + c] - res[:c], res[c:]))
        return parts

    rd = [read_state(g, t) for g, t in zip(groups, sv)]
    us = [jnp.concatenate([p[1] for p in parts], axis=0) for parts in rd]
    outs = [jnp.concatenate([p[2] for p in parts], axis=0) + _dot(t["attn"], u.astype(BF16))
            for parts, t, u in zip(rd, sv, us)]
    for g in groups:
        for (h, s), (s_old, u_p, _) in zip(pair_ids, rd[g]):
            r0 = h * gr + s * c
            cd = fr[g]["cd_all"][s * c:s * c + 1, nh + h:nh + h + 1]
            s_ref[g, s * nh + h] = s_old * cd + _dot_tn(sv[g]["kt"][r0:r0 + c],
                                                        u_p.astype(BF16))
    for g in groups:
        zs = stack_heads(z_ref[g], 0)
        o_n = _rms(outs[g], nw_ref[...]) * _silu(zs)
        oa_ref[g] = jnp.concatenate([o_n[h * gr:(h + 1) * gr] for h in range(GDN_HEADS)],
                                    axis=1).astype(oa_ref.dtype)

    @pl.when(n == pl.num_programs(1) - 1)
    def _():
        st_ref[0] = s_ref[...]


def _mixer(qkv, z, ba, sc, conv_w, gparams, norm_w, sc_conv_w, *, c, ng, layer, depth,
           prev_states=None, state=None, row_lo=0, row_hi=None):
    g_total, l, _ = qkv.shape
    gr = GROUP_ROWS
    assert l % gr == 0 and gr % c == 0 and g_total % ng == 0
    nc = l // gr
    pairs = GDN_HEADS * (gr // c)
    row_hi = c if row_hi is None else row_hi
    blk = lambda w: pl.BlockSpec((ng, gr, w), lambda i, n: (i, n, 0))
    in_specs = [blk(qkv.shape[2]), blk(z.shape[2]), blk(ba.shape[2]), blk(sc.shape[2])]
    args = [qkv, z, ba, sc]
    st_shape = (g_total, pairs, GDN_HEAD_DIM, GDN_HEAD_DIM)
    st_spec = pl.BlockSpec((1, ng) + st_shape[1:], lambda i, n: (layer, i, 0, 0, 0))
    if state is not None:
        assert nc == 1
        qh, sh, s0 = state
        assert s0.shape[1:] == st_shape
        in_specs += [pl.BlockSpec((ng, gr, qh.shape[2]), lambda i, n: (i, 0, 0)),
                     pl.BlockSpec((ng, gr, sh.shape[2]), lambda i, n: (i, 0, 0)), st_spec]
        args += [qh, sh, s0]
    const = lambda a: pl.BlockSpec(a.shape, lambda i, n: (0,) * a.ndim)
    norm_w = norm_w.reshape(1, -1)
    in_specs += [const(conv_w), const(gparams), const(norm_w), const(sc_conv_w)]
    args += [conv_w, gparams, norm_w, sc_conv_w]
    if prev_states is None:
        prev_states = jnp.zeros((depth,) + st_shape, F32)
    assert prev_states.shape == (depth,) + st_shape
    aliases = {len(args): 2}
    in_specs.append(pl.BlockSpec(memory_space=pl.ANY))
    args.append(prev_states)
    out_shape = [jax.ShapeDtypeStruct((g_total, l, GDN_WIDTH), BF16),
                 jax.ShapeDtypeStruct((g_total, l, SC_WIDTH), BF16),
                 jax.ShapeDtypeStruct((depth,) + st_shape, F32),
                 jax.ShapeDtypeStruct((g_total, l, SC_WIDTH), F32)]
    out_specs = [blk(GDN_WIDTH), blk(SC_WIDTH), st_spec, blk(SC_WIDTH)]
    body = functools.partial(_mixer_body, c=c, ng=ng, has_state=state is not None,
                             row_lo=row_lo, row_hi=row_hi)
    return pl.pallas_call(
        body,
        name="seq_mixer",
        grid=(g_total // ng, nc),
        in_specs=in_specs,
        out_specs=out_specs,
        out_shape=out_shape,
        input_output_aliases=aliases,
        scratch_shapes=[pltpu.VMEM((ng, gr + SUBLANES, qkv.shape[2]), F32),
                        pltpu.VMEM((ng, gr + SUBLANES, SC_WIDTH), F32),
                        pltpu.VMEM((ng,) + st_shape[1:], F32)],
        compiler_params=_params("arbitrary", "arbitrary"),
    )(*args)


def _select_blocks(gate, n_valid, axis=0):
    nb = gate.shape[axis]
    sub = lax.broadcasted_iota(jnp.int32, gate.shape, axis)
    gate = jnp.where(sub < n_valid, gate, NEG_INF)
    sel = jnp.zeros(gate.shape, F32)
    for j in range(nb):
        gj = gate[j:j + 1, :] if axis == 0 else gate[:, j:j + 1]
        beats = (gate > gj) | ((gate == gj) & (sub < j))
        cnt = jnp.sum(jnp.where(beats, 1.0, 0.0), axis=axis, keepdims=True)
        hit = (cnt < MOBA_TOPK) & (j < n_valid)
        sel = jnp.where((sub == j) & hit, 1.0, sel)
    return sel


_PV_ROWS = MOBA_HEAD_DIM + 16


def _moba_prompt_body(q_ref, k_ref, v_ref, o_ref, kmean_ref, kb_ref, vb_ref, sel_ref, *, nbat):
    i = pl.program_id(1)
    blk = MOBA_BLOCK
    hd = MOBA_HEAD_DIM
    w = MOBA_WIDTH
    l_seq = v_ref.shape[3]
    nb = l_seq // blk
    heads = range(MOBA_HEADS)
    head_rows = [slice(h * hd, (h + 1) * hd) for h in heads]
    chains = [(b, h) for b in range(nbat) for h in heads]

    @pl.when(i == 0)
    def _():
        ones = jnp.ones((_PV_ROWS - hd, blk), BF16)
        for b in range(nbat):
            for j in range(nb):
                kj = k_ref[b * l_seq + j * blk:b * l_seq + (j + 1) * blk, :]
                kmean_ref[b, j:j + 1, :] = jnp.mean(kj, axis=0, keepdims=True)
                kb_ref[b, j] = kj.astype(BF16)
                for h in heads:
                    vb_ref[b, j, h, 0:hd, :] = v_ref[0, b, head_rows[h],
                                                     j * blk:(j + 1) * blk].astype(BF16)
                    vb_ref[b, j, h, hd:_PV_ROWS, :] = ones

    feat = lax.broadcasted_iota(jnp.int32, (w, blk), 0)
    key_i = lax.broadcasted_iota(jnp.int32, (blk, blk), 0)
    qry_i = lax.broadcasted_iota(jnp.int32, (blk, blk), 1)
    causal = key_i <= qry_i
    qmb = []
    for b in range(nbat):
        qt = q_ref[b] * (hd ** -0.5)
        qms = [jnp.where((feat >= h * hd) & (feat < (h + 1) * hd), qt, 0.0) for h in heads]
        qmb.append(jnp.concatenate(qms, axis=1).astype(BF16))
        for h in heads:
            gate = _dot(kmean_ref[b], qms[h], HI)
            sel_ref[b, h] = _select_blocks(gate, i)

    def scores(j):
        s_all = [_dot(kb_ref[b, j], qmb[b]) for b in range(nbat)]
        return [s_all[b][:, h * blk:(h + 1) * blk] for b, h in chains]

    def attend(j, s_list, m_list):
        ps = [jnp.exp(s - m).astype(BF16) for s, m in zip(s_list, m_list)]
        return [_dot(vb_ref[b, j, h], p) for (b, h), p in zip(chains, ps)]

    s0 = [jnp.where(causal, s, NEG_INF) for s in scores(i)]
    m0 = [jnp.max(s, axis=0, keepdims=True) for s in s0]
    pv0 = attend(i, s0, m0)
    init = []
    for m, pv in zip(m0, pv0):
        init += [m, pv[hd:hd + 1], pv[:hd]]

    def past_block(j, carry):
        s_list = [jnp.where(sel_ref[b, h, pl.ds(j, 1), :] > 0.5, s, NEG_INF)
                  for (b, h), s in zip(chains, scores(j))]
        m_old = carry[0::3]
        m_new = [jnp.maximum(m, jnp.max(s, axis=0, keepdims=True))
                 for m, s in zip(m_old, s_list)]
        pv = attend(j, s_list, m_new)
        new = []
        for c in range(len(chains)):
            alpha = jnp.exp(m_old[c] - m_new[c])
            new += [m_new[c], alpha * carry[3 * c + 1] + pv[c][hd:hd + 1],
                    alpha * carry[3 * c + 2] + pv[c][:hd]]
        return tuple(new)

    final = lax.fori_loop(0, i, past_block, tuple(init))
    for c, (b, h) in enumerate(chains):
        o_ref[b, head_rows[h], :] = (final[3 * c + 2] / final[3 * c + 1]).astype(o_ref.dtype)


def _moba_prompt(qt, k, vt, layer, nbat=4):
    b, w, l = qt.shape
    blk = MOBA_BLOCK
    nbat = nbat if b % nbat == 0 else 1
    assert l % blk == 0
    nb = l // blk
    return pl.pallas_call(
        functools.partial(_moba_prompt_body, nbat=nbat),
        name="moba_prompt",
        grid=(b // nbat, nb),
        in_specs=[pl.BlockSpec((nbat, w, blk), lambda i, t: (i, 0, t)),
                  pl.BlockSpec((nbat * l, w), lambda i, t: (i, 0)),
                  pl.BlockSpec((1, nbat, w, l), lambda i, t: (layer, i, 0, 0))],
        out_specs=pl.BlockSpec((nbat, w, blk), lambda i, t: (i, 0, t)),
        out_shape=jax.ShapeDtypeStruct((b, w, l), BF16),
        scratch_shapes=[pltpu.VMEM((nbat, nb, w), F32), pltpu.VMEM((nbat, nb, blk, w), BF16),
                        pltpu.VMEM((nbat, nb, MOBA_HEADS, _PV_ROWS, blk), BF16),
                        pltpu.VMEM((nbat, MOBA_HEADS, nb, blk), F32)],
        compiler_params=_params("arbitrary", "arbitrary"),
    )(qt, k, vt)


PAGE_SLOTS = 3
SEQS_PER_STEP = 2


def _moba_sample_body(pt_ref, q_ref, kn_ref, vn_ref, ck_ref, cv_ref, o_ref, kbuf, vbuf, sem,
                      *, n_pages, base, n_steps, ns):
    i = pl.program_id(0)
    hd = MOBA_HEAD_DIM
    r = S_ROWS
    w = MOBA_WIDTH
    ppb = MOBA_BLOCK // PAGE_SIZE
    nb = n_pages // ppb
    ahead = PAGE_SLOTS - 1
    seqs = range(ns)
    blocks = range(nb)

    def page_copies(step, slot):
        cps = []
        for s in seqs:
            for p in range(n_pages):
                row = base + pt_ref[step * ns + s, p]
                cps.append(pltpu.make_async_copy(ck_ref.at[row], kbuf.at[slot, s, p],
                                                 sem.at[0, slot]))
                cps.append(pltpu.make_async_copy(cv_ref.at[row], vbuf.at[slot, s, p],
                                                 sem.at[1, slot]))
        return cps

    def start_all(cps):
        for n, cp in enumerate(cps):
            cp.start(priority=n % 2)

    @pl.when(i == 0)
    def _():
        for d in range(ahead):
            start_all(page_copies(d, d))

    slot = lax.rem(i, PAGE_SLOTS)
    for cp in page_copies(i, slot):
        cp.wait()

    lane = lax.broadcasted_iota(jnp.int32, (r, w), 1)
    head_masks = [jnp.where((lane >= h * hd) & (lane < (h + 1) * hd), 1.0, 0.0)
                  for h in range(MOBA_HEADS)]
    col = lax.broadcasted_iota(jnp.int32, (w, LANES), 1)
    q_bd = [jnp.concatenate([q_ref[s] * (hd ** -0.5) * hm for hm in head_masks], axis=0)
            for s in seqs]
    q_bb = [x.astype(BF16) for x in q_bd]

    kp = [[kbuf[slot, s, p] for p in range(n_pages)] for s in seqs]
    s_pages = [[_dot(q_bb[s], x.astype(BF16)) for x in kp[s]] for s in seqs]
    kmean_t = []
    for s in seqs:
        km = jnp.zeros((w, LANES), F32)
        for j in blocks:
            ksum = kp[s][j * ppb]
            for pp in range(1, ppb):
                ksum = ksum + kp[s][j * ppb + pp]
            km = jnp.where(col == j, jnp.sum(ksum, axis=1, keepdims=True) * (1.0 / MOBA_BLOCK),
                           km)
        kmean_t.append(km)
    s_blk = [[jnp.concatenate(s_pages[s][j * ppb:(j + 1) * ppb], axis=-1) for j in blocks]
             for s in seqs]
    m_cols = [[jnp.max(x, axis=1, keepdims=True) for x in s_blk[s]] for s in seqs]
    e_blk = [[jnp.exp(x - m) for x, m in zip(s_blk[s], m_cols[s])] for s in seqs]
    l_cols = [[jnp.sum(e, axis=1, keepdims=True) for e in e_blk[s]] for s in seqs]
    accs = []
    for s in seqs:
        row_accs = []
        for j in blocks:
            eb = e_blk[s][j].astype(BF16)
            acc = None
            for pp in range(ppb):
                vp = vbuf[slot, s, j * ppb + pp].astype(BF16)
                part = _dot_nt(eb[:, pp * PAGE_SIZE:(pp + 1) * PAGE_SIZE], vp)
                acc = part if acc is None else acc + part
            row_accs.append(acc)
        accs.append(row_accs)

    nxt = jnp.minimum(i + ahead, n_steps - 1)
    start_all(page_copies(nxt, lax.rem(i + ahead, PAGE_SLOTS)))

    gates = [_dot(q_bd[s], kmean_t[s], HI)[:, :nb] for s in seqs]
    sels = [_select_blocks(g, nb, axis=1) > 0.5 for g in gates]
    qt = lax.broadcasted_iota(jnp.int32, (MOBA_HEADS * r, r), 0) & (r - 1)
    kt = lax.broadcasted_iota(jnp.int32, (MOBA_HEADS * r, r), 1)
    own = (kt >= S_LO) & (kt <= qt)
    for s in seqs:
        sel = sels[s]
        m_all = jnp.concatenate(m_cols[s], axis=1)
        l_all = jnp.concatenate(l_cols[s], axis=1)
        s_own = _dot_nt(q_bb[s], kn_ref[s].astype(BF16))
        m_tot = jnp.maximum(jnp.max(jnp.where(sel, m_all, NEG_INF), axis=1, keepdims=True),
                            jnp.max(jnp.where(own, s_own, NEG_INF), axis=1, keepdims=True))
        wj = jnp.where(sel, jnp.exp(jnp.where(sel, m_all - m_tot, 0.0)), 0.0)
        p_own = jnp.where(own, jnp.exp(jnp.where(own, s_own - m_tot, 0.0)), 0.0)
        l_tot = (jnp.sum(wj * l_all, axis=1, keepdims=True)
                 + jnp.sum(p_own, axis=1, keepdims=True))
        acc = _dot(p_own.astype(BF16), vn_ref[s].astype(BF16))
        for j in blocks:
            acc = acc + wj[:, j:j + 1] * accs[s][j]
        o_bd = acc / l_tot
        o = None
        for h in range(MOBA_HEADS):
            part = o_bd[h * r:(h + 1) * r, :] * head_masks[h]
            o = part if o is None else o + part
        o_ref[s] = o.astype(o_ref.dtype)

    @pl.when(i == n_steps - 1)
    def _():
        for d in range(1, PAGE_SLOTS):
            for cp in page_copies(n_steps - 1, lax.rem(i + d, PAGE_SLOTS)):
                cp.wait()


def _moba_sample(q, k_new, v_new, cache_kt, cache_vt, page_table, base):
    b, r, w = q.shape
    page = cache_kt.shape[2]
    n_pages = page_table.shape[1]
    ns = SEQS_PER_STEP if b % SEQS_PER_STEP == 0 else 1
    n_steps = b // ns
    assert page == PAGE_SIZE and r == S_ROWS and cache_kt.shape[1] == w
    assert n_steps >= PAGE_SLOTS
    tile = pl.BlockSpec((ns, r, w), lambda i, pt: (i, 0, 0))
    hbm = pl.BlockSpec(memory_space=pl.ANY)
    grid_spec = pltpu.PrefetchScalarGridSpec(
        num_scalar_prefetch=1,
        grid=(n_steps,),
        in_specs=[tile, tile, tile, hbm, hbm],
        out_specs=tile,
        scratch_shapes=[pltpu.VMEM((PAGE_SLOTS, ns, n_pages, w, page), F32),
                        pltpu.VMEM((PAGE_SLOTS, ns, n_pages, w, page), F32),
                        pltpu.SemaphoreType.DMA((2, PAGE_SLOTS))],
    )
    body = functools.partial(_moba_sample_body, n_pages=n_pages, base=base, n_steps=n_steps,
                             ns=ns)
    return pl.pallas_call(
        body,
        name="moba_sample",
        grid_spec=grid_spec,
        out_shape=jax.ShapeDtypeStruct((b, r, w), BF16),
        compiler_params=_params("arbitrary"),
    )(page_table, q, k_new, v_new, cache_kt, cache_vt)


def _trunk(x, weights, ffn_w, norm_final, *, sample=None):
    b, l, d = x.shape
    depth = len(weights)
    x = x.reshape(b * l, d)
    gr = GROUP_ROWS
    states = []
    gdn_states = kv = None
    for layer, wts in enumerate(weights):
        x = _ffn(x, wts["n_ffn1"], *ffn_w[0], layer)
        if sample is None:
            qkv, z, sc, ba, mk, mq, mv, mkt = _inproj(
                x, wts["n_mix"], wts["w_rows"], wts["w_qvk_t"], seq_len=l,
                stack=(depth, layer, kv))
            kv = (mv, mkt)
            grp = lambda a: a.reshape(b, l, a.shape[-1])
            o_a, o_b, gdn_states, gated = _mixer(
                grp(qkv), grp(z), grp(ba), grp(sc), wts["gdn_conv_w"], wts["gparams"],
                wts["gdn_norm_w"], wts["sc_conv_w"], c=GDN_CHUNK, ng=8 if b % 8 == 0 else 1,
                layer=layer, depth=depth, prev_states=gdn_states)
            o_c = _moba_prompt(mq, mk, mv, layer)
        else:
            qkv, z, sc, ba, mk, mq, mv = _inproj(x, wts["n_mix"], wts["w_rows"], wts["w_qv"])
            ngroups = b * l // gr
            grp = lambda a: a.reshape(ngroups, gr, a.shape[-1])
            state = (grp(sample["gdn_conv"][layer]), grp(sample["sconv"][layer]), sample["gdn"])
            o_a, o_b, gdn_states, gated = _mixer(
                grp(qkv), grp(z), grp(ba), grp(sc), wts["gdn_conv_w"], wts["gparams"],
                wts["gdn_norm_w"], wts["sc_conv_w"], c=l, ng=2 if ngroups % 2 == 0 else 1,
                layer=layer, depth=depth, prev_states=gdn_states, state=state,
                row_lo=S_LO, row_hi=S_HI)
            seq = lambda a: a.reshape(b, l, a.shape[-1])
            o_c = _moba_sample(seq(mq), seq(mk), seq(mv), sample["cache_kt"], sample["cache_vt"],
                               sample["page_table"], layer * sample["n_pool"])
            o_c = o_c.reshape(b * l, -1)
        mix = (o_a.reshape(b * l, -1), o_b.reshape(b * l, -1), o_c, wts["w_out"])
        x = _ffn(x, wts["n_ffn2"], *ffn_w[1], layer, mix=mix,
                 final_w=norm_final if layer == depth - 1 else None)
        states.append(dict(qkv=qkv.reshape(b, l, -1), gated=gated.reshape(b, l, -1)))
        if sample is not None:
            states[-1].update(k=mk.reshape(b, l, -1), v=mv)
    return x.reshape(b, l, d), states, gdn_states, kv


def kernel(x_prompt, x_sample, state_gdn, state_gdn_conv, state_sconv, cache_k, cache_v,
           page_table, norm_ffn1, ffn1_w_gate, ffn1_w_up, ffn1_w_down, norm_mix, w_in,
           gdn_conv_w, gdn_a_log, gdn_dt_bias, gdn_norm_w, sc_conv_w, w_out,
           norm_ffn2, ffn2_w_gate, ffn2_w_up, ffn2_w_down, norm_final):
    depth = w_in.shape[0]
    gw, sw, mw = GDN_WIDTH, SC_WIDTH, MOBA_WIDTH
    nh = GDN_HEADS
    mh, md = MOBA_HEADS, MOBA_HEAD_DIM

    weights = []
    for la in range(depth):
        w = w_in[la]
        ba_off = 4 * gw
        sc_off = ba_off + 2 * nh
        mo_off = sc_off + 3 * sw
        w_rows = jnp.concatenate(
            [w[:, :ba_off], w[:, sc_off:mo_off],
             jnp.pad(w[:, ba_off:sc_off], ((0, 0), (0, LANES - 2 * nh))),
             w[:, mo_off + mw:mo_off + 2 * mw]], axis=1).astype(BF16)
        w_qv = jnp.concatenate([w[:, mo_off:mo_off + mw], w[:, mo_off + 2 * mw:mo_off + 3 * mw]],
                               axis=1).astype(BF16)
        gparams = jnp.zeros((SUBLANES, LANES), F32)
        gparams = gparams.at[0, nh:2 * nh].set(gdn_a_log[la].astype(F32))
        gparams = gparams.at[1, nh:2 * nh].set(gdn_dt_bias[la].astype(F32))
        weights.append(dict(
            n_ffn1=norm_ffn1[la],
            n_mix=norm_mix[la], w_rows=w_rows, w_qv=w_qv,
            w_qvk_t=jnp.concatenate([w_qv, w_rows[:, -mw:]], axis=1).T,
            gdn_conv_w=gdn_conv_w[la], gparams=gparams, gdn_norm_w=gdn_norm_w[la],
            sc_conv_w=sc_conv_w[la], w_out=_to_bf16(w_out, la),
            n_ffn2=norm_ffn2[la]))
    ffn_w = ((ffn1_w_gate, ffn1_w_up, ffn1_w_down), (ffn2_w_gate, ffn2_w_up, ffn2_w_down))

    pb, pl_, _ = x_prompt.shape
    y_p, st_p, p_gdn, (p_v, p_k) = _trunk(x_prompt, weights, ffn_w, norm_final)
    hist = GDN_CONV - 1
    p_conv = jnp.stack([s["qkv"][:, -hist:] for s in st_p])
    p_sconv = jnp.stack([s["gated"][:, -(SC_CONV - 1):] for s in st_p])
    p_k = p_k.reshape(depth, pb, mh, md, pl_).transpose(0, 1, 4, 2, 3)
    p_v = p_v.reshape(depth, pb, mh, md, pl_).transpose(0, 1, 4, 2, 3)

    db, dl, d = x_sample.shape
    assert dl == S_HI - S_LO and (db * S_ROWS) % GROUP_ROWS == 0
    n_pool = cache_k.shape[1]
    tile_pad = lambda a, lo: jnp.pad(a, ((0, 0),) * (a.ndim - 2)
                                     + ((lo, S_ROWS - lo - a.shape[-2]), (0, 0)))
    seqs_per_group = GROUP_ROWS // S_ROWS
    paged_t = lambda c: c.transpose(0, 1, 3, 4, 2).reshape(depth * n_pool, mw, PAGE_SIZE)
    sample = dict(
        gdn=state_gdn.reshape(depth, db // seqs_per_group, seqs_per_group * nh,
                              GDN_HEAD_DIM, GDN_HEAD_DIM),
        gdn_conv=tile_pad(state_gdn_conv, 0),
        sconv=tile_pad(state_sconv, S_LO - (SC_CONV - 1)),
        cache_kt=paged_t(cache_k), cache_vt=paged_t(cache_v), n_pool=n_pool,
        page_table=page_table)
    y_s, st_s, s_gdn, _ = _trunk(tile_pad(x_sample, S_LO), weights, ffn_w, norm_final,
                                 sample=sample)
    y_s = y_s[:, S_LO:S_HI]
    s_gdn = s_gdn.reshape(state_gdn.shape)
    s_conv = jnp.stack([s["qkv"][:, S_HI - hist:S_HI] for s in st_s])
    s_sconv = jnp.stack([s["gated"][:, S_HI - (SC_CONV - 1):S_HI] for s in st_s])
    s_k = jnp.stack([s["k"][:, S_LO:S_HI] for s in st_s]).reshape(depth, db, dl, mh, md)
    s_v = jnp.stack([s["v"].reshape(db, S_ROWS, mw)[:, S_LO:S_HI] for s in st_s]).reshape(
        depth, db, dl, mh, md)
    return (y_p, y_s, p_gdn, p_conv, p_sconv, p_k, p_v, s_gdn, s_conv, s_sconv, s_k, s_v)
```

```python
import functools

import jax
import jax.numpy as jnp
from jax import lax
from jax.experimental import pallas as pl
from jax.experimental.pallas import tpu as pltpu

NORM_EPS = 1e-6
NEG_INF = -1e30

GDN_HEADS = 4
GDN_HEAD_DIM = 128
GDN_WIDTH = GDN_HEADS * GDN_HEAD_DIM
GDN_CONV = 4
GDN_CHUNK = 64
SC_WIDTH = 256
SC_CONV = 3
MOBA_HEADS = 4
MOBA_HEAD_DIM = 64
MOBA_WIDTH = MOBA_HEADS * MOBA_HEAD_DIM
MOBA_BLOCK = 256
MOBA_TOPK = 3
PAGE_SIZE = 128

LANES = 128
SUBLANES = 8
VMEM_LIMIT_BYTES = 56 * 1024 * 1024

S_ROWS = SUBLANES
S_LO = GDN_CONV - 1
S_HI = S_LO + 4

GROUP_ROWS = GDN_CHUNK
STACK_ROWS = GDN_HEADS * GROUP_ROWS

F32 = jnp.float32
BF16 = jnp.bfloat16
HI = lax.Precision.HIGHEST


def _dot(a, b, precision=None):
    return jnp.dot(a, b, preferred_element_type=F32, precision=precision)


def _dot_nt(a, b, precision=None):
    return lax.dot_general(a, b, (((1,), (1,)), ((), ())),
                           preferred_element_type=F32, precision=precision)


def _dot_tn(a, b, precision=None):
    return lax.dot_general(a, b, (((0,), (0,)), ((), ())),
                           preferred_element_type=F32, precision=precision)


def _rms(x, w):
    return x * lax.rsqrt(jnp.mean(x * x, axis=-1, keepdims=True) + NORM_EPS) * w


def _silu(x):
    return x * jax.nn.sigmoid(x)


def _softplus(x):
    return jnp.maximum(x, 0.0) + jnp.log1p(jnp.exp(-jnp.abs(x)))


def _resident(shape):
    nd = len(shape)
    return pl.BlockSpec(shape, lambda *_: (0,) * nd, pipeline_mode=pl.Buffered(1))


def _params(*semantics):
    return pltpu.CompilerParams(dimension_semantics=semantics,
                                vmem_limit_bytes=VMEM_LIMIT_BYTES)


def _cast_body(w_ref, o_ref):
    o_ref[...] = w_ref[0].astype(o_ref.dtype)


def _to_bf16(w, layer, rows=256):
    _, r, c = w.shape
    rows = min(rows, r)
    assert r % rows == 0
    return pl.pallas_call(
        _cast_body,
        name="to_bf16",
        grid=(r // rows,),
        in_specs=[pl.BlockSpec((1, rows, c), lambda i: (layer, i, 0))],
        out_specs=pl.BlockSpec((rows, c), lambda i: (i, 0)),
        out_shape=jax.ShapeDtypeStruct((r, c), BF16),
        compiler_params=_params("arbitrary"),
    )(w)


def _ffn_body(*refs, has_mix, oc_transposed, has_final, f_chunk, layer):
    refs = list(refs)
    x_ref = refs.pop(0)
    if has_mix:
        oa_ref, ob_ref, oc_ref, wo_ref = refs[:4]
        refs = refs[4:]
    nw_ref, wg_hbm, wu_hbm, wd_hbm = refs[:4]
    refs = refs[4:]
    if has_final:
        nf_ref = refs.pop(0)
    out_ref, xn_ref, acc_ref, wg_ref, wu_ref, wd_ref, wsem = refs
    i = pl.program_id(0)
    chunks = range(wg_ref.shape[1] // f_chunk)

    def chunk_copies(c):
        cols = pl.ds(c * f_chunk, f_chunk)
        return [pltpu.make_async_copy(wg_hbm.at[layer, :, cols], wg_ref.at[:, cols], wsem.at[0, c]),
                pltpu.make_async_copy(wu_hbm.at[layer, :, cols], wu_ref.at[:, cols], wsem.at[1, c]),
                pltpu.make_async_copy(wd_hbm.at[layer, cols, :], wd_ref.at[cols, :], wsem.at[2, c])]

    @pl.when(i == 0)
    def _():
        for c in chunks:
            for cp in chunk_copies(c):
                cp.start()

    def run(first_step):
        x = x_ref[...]
        if has_mix:
            ga = oa_ref.shape[1]
            gb = ga + ob_ref.shape[1]
            x = x + _dot(oa_ref[...], wo_ref[0:ga, :]) + _dot(ob_ref[...], wo_ref[ga:gb, :])
            if oc_transposed:
                x = x + _dot_tn(oc_ref[0], wo_ref[gb:, :])
            else:
                x = x + _dot(oc_ref[...], wo_ref[gb:, :])
        xn_ref[...] = _rms(x, nw_ref[...]).astype(BF16)
        for c in chunks:
            lo = c * f_chunk
            if first_step:
                for cp in chunk_copies(c):
                    cp.wait()
            xn = xn_ref[...]
            g = _dot(xn, wg_ref[:, lo:lo + f_chunk].astype(BF16))
            u = _dot(xn, wu_ref[:, lo:lo + f_chunk].astype(BF16))
            h = (_silu(g) * u).astype(BF16)
            part = _dot(h, wd_ref[lo:lo + f_chunk, :].astype(BF16))
            if c == 0:
                acc_ref[...] = part
            else:
                acc_ref[...] += part
        y = x + 0.5 * acc_ref[...]
        if has_final:
            y = _rms(y, nf_ref[...])
        out_ref[...] = y

    @pl.when(i == 0)
    def _():
        run(True)

    @pl.when(i != 0)
    def _():
        run(False)


def _ffn(x, nw, wg, wu, wd, layer, mix=None, final_w=None, tm=512, f_chunk=256):
    m, d = x.shape
    d_ff = wg.shape[2]
    hbm = pl.BlockSpec(memory_space=pl.ANY)
    tm = min(tm, m)
    assert m % tm == 0 and d_ff % f_chunk == 0
    row = lambda i: (i, 0)
    in_specs = [pl.BlockSpec((tm, d), row)]
    args = [x]
    oc_transposed = False
    if mix is not None:
        oa, ob, oc, wo = mix
        oc_transposed = oc.ndim == 3
        if oc_transposed:
            tiles = oc.shape[2] // tm
            assert oc.shape[2] % tm == 0
            oc_spec = pl.BlockSpec((1, oc.shape[1], tm), lambda i: (i // tiles, 0, i % tiles))
        else:
            oc_spec = pl.BlockSpec((tm, oc.shape[1]), row)
        in_specs += [pl.BlockSpec((tm, oa.shape[1]), row), pl.BlockSpec((tm, ob.shape[1]), row),
                     oc_spec, _resident(wo.shape)]
        args += [oa, ob, oc, wo]
    in_specs += [_resident((1, d)), hbm, hbm, hbm]
    args += [nw.reshape(1, d), wg, wu, wd]
    if final_w is not None:
        in_specs.append(_resident((1, d)))
        args.append(final_w.reshape(1, d))
    body = functools.partial(_ffn_body, has_mix=mix is not None, oc_transposed=oc_transposed,
                             has_final=final_w is not None, f_chunk=f_chunk, layer=layer)
    return pl.pallas_call(
        body,
        name="ffn_mix" if mix is not None else "ffn",
        grid=(m // tm,),
        in_specs=in_specs,
        out_specs=pl.BlockSpec((tm, d), row),
        out_shape=jax.ShapeDtypeStruct((m, d), F32),
        scratch_shapes=[pltpu.VMEM((tm, d), BF16), pltpu.VMEM((tm, d), F32),
                        pltpu.VMEM(wg.shape[1:], F32), pltpu.VMEM(wu.shape[1:], F32),
                        pltpu.VMEM(wd.shape[1:], F32),
                        pltpu.SemaphoreType.DMA((3, d_ff // f_chunk))],
        compiler_params=_params("arbitrary"),
    )(*args)


_ROW_WIDTHS = (3 * GDN_WIDTH, GDN_WIDTH, 3 * SC_WIDTH, LANES, MOBA_WIDTH)


def _causal_conv(buf, x, w_ref):
    pad = SUBLANES
    n = x.shape[0]
    taps = w_ref.shape[0]
    buf[pad:pad + n, :] = x
    y = None
    for t in range(taps - 1):
        lo = pad - (taps - 1) + t
        term = buf[lo:lo + n, :] * w_ref[t:t + 1, :]
        y = term if y is None else y + term
    y = y + x * w_ref[taps - 1:taps, :]
    tail = buf[n:n + pad, :]
    buf[0:pad, :] = tail
    return y


def _l2_normalize(x):
    return x * lax.rsqrt(jnp.sum(x * x, axis=-1, keepdims=True) + NORM_EPS)


def _gates(ba, gp_ref):
    return jax.nn.sigmoid(ba), -jnp.exp(gp_ref[0:1, :]) * _softplus(ba + gp_ref[1:2, :])


def _inproj_body(x_ref, nw_ref, w_ref, wqv_ref, *refs, transposed_qv, kv_slot, n_aliased):
    out_refs = refs[n_aliased:]
    xn = _rms(x_ref[...], nw_ref[...]).astype(BF16)

    def put(ref, value, lead=()):
        if kv_slot is None:
            ref[lead or ...] = value
            return
        for s in range(ref.shape[0]):
            ref[(s,) + lead] = value if s == kv_slot else jnp.zeros(value.shape, value.dtype)

    lo = 0
    for ref in out_refs[:len(_ROW_WIDTHS)]:
        width = ref.shape[-1]
        ref[...] = _dot(xn, w_ref[:, lo:lo + width])
        lo += width
    q_ref, v_ref = out_refs[len(_ROW_WIDTHS):len(_ROW_WIDTHS) + 2]
    mw = MOBA_WIDTH
    if transposed_qv:
        q_ref[0] = _dot_nt(wqv_ref[0:mw, :], xn)
        put(v_ref, _dot_nt(wqv_ref[mw:2 * mw, :], xn), lead=(0,))
        if kv_slot is not None:
            put(out_refs[-1], out_refs[len(_ROW_WIDTHS) - 1][...].T, lead=(0,))
    else:
        q_ref[...] = _dot(xn, wqv_ref[:, 0:mw])
        v_ref[...] = _dot(xn, wqv_ref[:, mw:2 * mw])


def _inproj(x, nw, w_rows, w_qv, *, seq_len=None, tm=1024, stack=None):
    m, d = x.shape
    tm = min(tm, m if seq_len is None else seq_len)
    assert m % tm == 0 and w_rows.shape[1] == sum(_ROW_WIDTHS)
    row = lambda i: (i, 0)
    out_specs = [pl.BlockSpec((tm, wd), row) for wd in _ROW_WIDTHS]
    out_shape = [jax.ShapeDtypeStruct((m, wd), F32) for wd in _ROW_WIDTHS]
    transposed_qv = seq_len is not None
    if transposed_qv:
        assert seq_len % tm == 0
        tiles = seq_len // tm
        spec = pl.BlockSpec((1, MOBA_WIDTH, tm), lambda i: (i // tiles, 0, i % tiles))
        shape = jax.ShapeDtypeStruct((m // seq_len, MOBA_WIDTH, seq_len), F32)
    else:
        spec = pl.BlockSpec((tm, MOBA_WIDTH), row)
        shape = jax.ShapeDtypeStruct((m, MOBA_WIDTH), F32)
    out_specs += [spec, spec]
    out_shape += [shape, shape]
    in_specs = [pl.BlockSpec((tm, d), row), _resident((1, d)), _resident(w_rows.shape),
                _resident(w_qv.shape)]
    args = [x, nw.reshape(1, d), w_rows, w_qv]
    aliases, kv_slot = {}, None
    if stack is not None:
        assert transposed_qv
        depth, layer, prev = stack
        v_out, k_out = len(_ROW_WIDTHS) + 1, len(_ROW_WIDTHS) + 2
        mw = MOBA_WIDTH
        stacked = jax.ShapeDtypeStruct((depth, m // seq_len, mw, seq_len), F32)
        if prev is None:
            kv_slot = layer
            spec = pl.BlockSpec((depth, 1, mw, tm), lambda i: (0, i // tiles, 0, i % tiles))
        else:
            kv_slot = 0
            spec = pl.BlockSpec((1, 1, mw, tm), lambda i: (layer, i // tiles, 0, i % tiles))
            aliases = {len(args): v_out, len(args) + 1: k_out}
            in_specs += [pl.BlockSpec(memory_space=pl.ANY)] * 2
            args += list(prev)
        out_shape[v_out] = stacked
        out_specs[v_out] = spec
        out_shape.append(stacked)
        out_specs.append(spec)
    return pl.pallas_call(
        functools.partial(_inproj_body, transposed_qv=transposed_qv, kv_slot=kv_slot,
                          n_aliased=len(aliases)),
        name="inproj",
        grid=(m // tm,),
        in_specs=in_specs,
        out_specs=out_specs,
        out_shape=out_shape,
        input_output_aliases=aliases,
        compiler_params=_params("arbitrary"),
    )(*args)


def _mixer_body(*refs, c, ng, has_state, row_lo, row_hi):
    refs = list(refs)
    qkv_ref, z_ref, ba_ref, sc_ref = refs[:4]
    refs = refs[4:]
    if has_state:
        qh_ref, sh_ref, s0_ref = refs[:3]
        refs = refs[3:]
    cw_ref, gp_ref, nw_ref, scw_ref = refs[:4]
    refs = refs[5:]
    oa_ref, ob_ref, st_ref, gt_ref = refs[:4]
    xbuf, gbuf, s_ref = refs[4:]
    n = pl.program_id(1)
    hd = GDN_HEAD_DIM
    gr = GROUP_ROWS
    sr = STACK_ROWS
    pad = SUBLANES
    seqs = gr // c
    shift = c.bit_length() - 1
    assert 1 << shift == c

    @pl.when(n == 0)
    def _():
        xbuf[:, 0:pad, :] = jnp.zeros((ng, pad, xbuf.shape[2]), F32)
        gbuf[:, 0:pad, :] = jnp.zeros((ng, pad, gbuf.shape[2]), F32)
        if has_state:
            s_ref[...] = s0_ref[0]
        else:
            s_ref[...] = jnp.zeros(s_ref.shape, F32)

    rows = lax.broadcasted_iota(jnp.int32, (gr, 1), 0) & (c - 1)
    is_hist = rows < row_lo
    live = jnp.where((rows >= row_lo) & (rows < row_hi), 1.0, 0.0)
    ri = lax.broadcasted_iota(jnp.int32, (gr, gr), 0)
    ci = lax.broadcasted_iota(jnp.int32, (gr, gr), 1)
    same = (ri >> shift) == (ci >> shift)
    cum_op = jnp.concatenate([jnp.where(same & (ri >= ci), 1.0, 0.0),
                              jnp.where(same, 1.0, 0.0)], axis=0)
    nh = GDN_HEADS
    tiles = sr // LANES
    band_tile = [h * gr // LANES for h in range(nh)]

    def bands(mat):
        return [mat[h * gr:(h + 1) * gr, band_tile[h] * LANES:(band_tile[h] + 1) * LANES]
                for h in range(nh)]

    def unband(pieces):
        zero = jnp.zeros((gr, LANES), BF16)
        return jnp.concatenate(
            [jnp.concatenate([p.astype(BF16) if t == band_tile[h] else zero
                              for t in range(tiles)], axis=1)
             for h, p in enumerate(pieces)], axis=0)

    incl, strict, eye = [], [], []
    for h in range(nh):
        rs = lax.broadcasted_iota(jnp.int32, (gr, LANES), 0) + h * gr
        cs = lax.broadcasted_iota(jnp.int32, (gr, LANES), 1) + band_tile[h] * LANES
        same_s = (rs >> shift) == (cs >> shift)
        incl.append(same_s & (rs >= cs))
        strict.append(same_s & (rs > cs))
        eye.append(jnp.where(rs == cs, 1.0, 0.0))

    def stack_heads(a, lo):
        return jnp.concatenate([a[:, lo + h * hd:lo + (h + 1) * hd] for h in range(GDN_HEADS)],
                               axis=0)

    def stack_cols(a, lo):
        return jnp.concatenate([a[:, lo + h:lo + h + 1] for h in range(GDN_HEADS)], axis=0)

    groups = range(ng)

    def front(g):
        x = qkv_ref[g]
        if has_state:
            x = jnp.where(is_hist, qh_ref[g], x)
        act = _silu(_causal_conv(xbuf.at[g], x, cw_ref))
        q = _l2_normalize(stack_heads(act, 0)) * (hd ** -0.5)
        k = _l2_normalize(stack_heads(act, GDN_WIDTH))
        v = stack_heads(act, 2 * GDN_WIDTH)

        sc = sc_ref[g]
        scw = SC_WIDTH
        gated = sc[:, scw:2 * scw] * sc[:, 0:scw]
        if has_state:
            gated = jnp.where(is_hist, sh_ref[g], gated)
        yb = _causal_conv(gbuf.at[g], gated, scw_ref)
        ob_ref[g] = (sc[:, 2 * scw:3 * scw] * yb).astype(ob_ref.dtype)
        gt_ref[g] = gated

        beta_all, g_all = _gates(ba_ref[g], gp_ref)
        beta_all, g_all = beta_all * live, g_all * live
        cum = _dot(cum_op, g_all, HI)
        gc_all, glast_all = cum[:gr], cum[gr:]
        beta = stack_cols(beta_all, 0)
        gc = stack_cols(gc_all, nh)
        eg = jnp.exp(gc)
        etail = jnp.exp(stack_cols(glast_all, nh) - gc)
        gc_row = jnp.broadcast_to(gc, (sr, LANES)).T[0:1, :]
        decay = []
        for h in range(nh):
            diff = (gc[h * gr:(h + 1) * gr]
                    - gc_row[:, band_tile[h] * LANES:(band_tile[h] + 1) * LANES])
            decay.append(jnp.where(incl[h], jnp.exp(jnp.where(incl[h], diff, 0.0)), 0.0))

        kb = k.astype(BF16)
        kk = bands(_dot_nt(kb, kb))
        m_strict = [jnp.where(strict[h], beta[h * gr:(h + 1) * gr] * kk[h] * decay[h], 0.0)
                    for h in range(nh)]
        return dict(q=q, k=k, v=v, kb=kb, beta=beta, eg=eg, etail=etail, decay=decay,
                    cd_all=jnp.exp(glast_all), m_strict=m_strict)

    fr = [front(g) for g in groups]

    invs = [[e - m for e, m in zip(eye, f["m_strict"])] for f in fr]
    pbs = [unband(f["m_strict"]) for f in fr]
    for _ in range(shift - 1):
        pbs = [unband(bands(_dot(pb, pb))) for pb in pbs]
        invs = [[i + d for i, d in zip(inv, bands(_dot(unband(inv), pb)))]
                for inv, pb in zip(invs, pbs)]

    def solve(f, inv):
        rhs = jnp.concatenate([f["k"] * (f["beta"] * f["eg"]), f["v"] * f["beta"]],
                              axis=-1).astype(BF16)
        sol = _dot(unband(inv), rhs)
        qk = bands(_dot_nt(f["q"].astype(BF16), f["kb"]))
        attn = unband([a * d for a, d in zip(qk, f["decay"])])
        return dict(w_k=sol[:, :hd], u_v=sol[:, hd:], attn=attn, qd=f["q"] * f["eg"],
                    kt=(f["k"] * f["etail"]).astype(BF16))

    sv = [solve(f, inv) for f, inv in zip(fr, invs)]

    pair_ids = [(h, s) for h in range(GDN_HEADS) for s in range(seqs)]

    def read_state(g, t):
        parts = []
        for h, s in pair_ids:
            r0 = h * gr + s * c
            s_old = s_ref[g, s * nh + h]
            lhs = jnp.concatenate([t["w_k"][r0:r0 + c], t["qd"][r0:r0 + c]], axis=0)
            res = _dot(lhs.astype(BF16), s_old.astype(BF16))
            parts.append((s_old, t["u_v"][r0:r0 + c] - res[:c], res[c:]))
        return parts

    rd = [read_state(g, t) for g, t in zip(groups, sv)]
    us = [jnp.concatenate([p[1] for p in parts], axis=0) for parts in rd]
    outs = [jnp.concatenate([p[2] for p in parts], axis=0) + _dot(t["attn"], u.astype(BF16))
            for parts, t, u in zip(rd, sv, us)]
    for g in groups:
        for (h, s), (s_old, u_p, _) in zip(pair_ids, rd[g]):
            r0 = h * gr + s * c
            cd = fr[g]["cd_all"][s * c:s * c + 1, nh + h:nh + h + 1]
            s_ref[g, s * nh + h] = s_old * cd + _dot_tn(sv[g]["kt"][r0:r0 + c],
                                                        u_p.astype(BF16))
    for g in groups:
        zs = stack_heads(z_ref[g], 0)
        o_n = _rms(outs[g], nw_ref[...]) * _silu(zs)
        oa_ref[g] = jnp.concatenate([o_n[h * gr:(h + 1) * gr] for h in range(GDN_HEADS)],
                                    axis=1).astype(oa_ref.dtype)

    @pl.when(n == pl.num_programs(1) - 1)
    def _():
        st_ref[0] = s_ref[...]


def _mixer(qkv, z, ba, sc, conv_w, gparams, norm_w, sc_conv_w, *, c, ng, layer, depth,
           prev_states=None, state=None, row_lo=0, row_hi=None):
    g_total, l, _ = qkv.shape
    gr = GROUP_ROWS
    assert l % gr == 0 and gr % c == 0 and g_total % ng == 0
    nc = l // gr
    pairs = GDN_HEADS * (gr // c)
    row_hi = c if row_hi is None else row_hi
    blk = lambda w: pl.BlockSpec((ng, gr, w), lambda i, n: (i, n, 0))
    in_specs = [blk(qkv.shape[2]), blk(z.shape[2]), blk(ba.shape[2]), blk(sc.shape[2])]
    args = [qkv, z, ba, sc]
    st_shape = (g_total, pairs, GDN_HEAD_DIM, GDN_HEAD_DIM)
    st_spec = pl.BlockSpec((1, ng) + st_shape[1:], lambda i, n: (layer, i, 0, 0, 0))
    if state is not None:
        assert nc == 1
        qh, sh, s0 = state
        assert s0.shape[1:] == st_shape
        in_specs += [pl.BlockSpec((ng, gr, qh.shape[2]), lambda i, n: (i, 0, 0)),
                     pl.BlockSpec((ng, gr, sh.shape[2]), lambda i, n: (i, 0, 0)), st_spec]
        args += [qh, sh, s0]
    const = lambda a: pl.BlockSpec(a.shape, lambda i, n: (0,) * a.ndim)
    norm_w = norm_w.reshape(1, -1)
    in_specs += [const(conv_w), const(gparams), const(norm_w), const(sc_conv_w)]
    args += [conv_w, gparams, norm_w, sc_conv_w]
    if prev_states is None:
        prev_states = jnp.zeros((depth,) + st_shape, F32)
    assert prev_states.shape == (depth,) + st_shape
    aliases = {len(args): 2}
    in_specs.append(pl.BlockSpec(memory_space=pl.ANY))
    args.append(prev_states)
    out_shape = [jax.ShapeDtypeStruct((g_total, l, GDN_WIDTH), BF16),
                 jax.ShapeDtypeStruct((g_total, l, SC_WIDTH), BF16),
                 jax.ShapeDtypeStruct((depth,) + st_shape, F32),
                 jax.ShapeDtypeStruct((g_total, l, SC_WIDTH), F32)]
    out_specs = [blk(GDN_WIDTH), blk(SC_WIDTH), st_spec, blk(SC_WIDTH)]
    body = functools.partial(_mixer_body, c=c, ng=ng, has_state=state is not None,
                             row_lo=row_lo, row_hi=row_hi)
    return pl.pallas_call(
        body,
        name="seq_mixer",
        grid=(g_total // ng, nc),
        in_specs=in_specs,
        out_specs=out_specs,
        out_shape=out_shape,
        input_output_aliases=aliases,
        scratch_shapes=[pltpu.VMEM((ng, gr + SUBLANES, qkv.shape[2]), F32),
                        pltpu.VMEM((ng, gr + SUBLANES, SC_WIDTH), F32),
                        pltpu.VMEM((ng,) + st_shape[1:], F32)],
        compiler_params=_params("arbitrary", "arbitrary"),
    )(*args)


def _select_blocks(gate, n_valid, axis=0):
    nb = gate.shape[axis]
    sub = lax.broadcasted_iota(jnp.int32, gate.shape, axis)
    gate = jnp.where(sub < n_valid, gate, NEG_INF)
    sel = jnp.zeros(gate.shape, F32)
    for j in range(nb):
        gj = gate[j:j + 1, :] if axis == 0 else gate[:, j:j + 1]
        beats = (gate > gj) | ((gate == gj) & (sub < j))
        cnt = jnp.sum(jnp.where(beats, 1.0, 0.0), axis=axis, keepdims=True)
        hit = (cnt < MOBA_TOPK) & (j < n_valid)
        sel = jnp.where((sub == j) & hit, 1.0, sel)
    return sel


_PV_ROWS = MOBA_HEAD_DIM + 16


def _moba_prompt_body(q_ref, k_ref, v_ref, o_ref, kmean_ref, kb_ref, vb_ref, sel_ref, *, nbat):
    i = pl.program_id(1)
    blk = MOBA_BLOCK
    hd = MOBA_HEAD_DIM
    w = MOBA_WIDTH
    l_seq = v_ref.shape[3]
    nb = l_seq // blk
    heads = range(MOBA_HEADS)
    head_rows = [slice(h * hd, (h + 1) * hd) for h in heads]
    chains = [(b, h) for b in range(nbat) for h in heads]

    @pl.when(i == 0)
    def _():
        ones = jnp.ones((_PV_ROWS - hd, blk), BF16)
        for b in range(nbat):
            for j in range(nb):
                kj = k_ref[b * l_seq + j * blk:b * l_seq + (j + 1) * blk, :]
                kmean_ref[b, j:j + 1, :] = jnp.mean(kj, axis=0, keepdims=True)
                kb_ref[b, j] = kj.astype(BF16)
                for h in heads:
                    vb_ref[b, j, h, 0:hd, :] = v_ref[0, b, head_rows[h],
                                                     j * blk:(j + 1) * blk].astype(BF16)
                    vb_ref[b, j, h, hd:_PV_ROWS, :] = ones

    feat = lax.broadcasted_iota(jnp.int32, (w, blk), 0)
    key_i = lax.broadcasted_iota(jnp.int32, (blk, blk), 0)
    qry_i = lax.broadcasted_iota(jnp.int32, (blk, blk), 1)
    causal = key_i <= qry_i
    qmb = []
    for b in range(nbat):
        qt = q_ref[b] * (hd ** -0.5)
        qms = [jnp.where((feat >= h * hd) & (feat < (h + 1) * hd), qt, 0.0) for h in heads]
        qmb.append(jnp.concatenate(qms, axis=1).astype(BF16))
        for h in heads:
            gate = _dot(kmean_ref[b], qms[h], HI)
            sel_ref[b, h] = _select_blocks(gate, i)

    def scores(j):
        s_all = [_dot(kb_ref[b, j], qmb[b]) for b in range(nbat)]
        return [s_all[b][:, h * blk:(h + 1) * blk] for b, h in chains]

    def attend(j, s_list, m_list):
        ps = [jnp.exp(s - m).astype(BF16) for s, m in zip(s_list, m_list)]
        return [_dot(vb_ref[b, j, h], p) for (b, h), p in zip(chains, ps)]

    s0 = [jnp.where(causal, s, NEG_INF) for s in scores(i)]
    m0 = [jnp.max(s, axis=0, keepdims=True) for s in s0]
    pv0 = attend(i, s0, m0)
    init = []
    for m, pv in zip(m0, pv0):
        init += [m, pv[hd:hd + 1], pv[:hd]]

    def past_block(j, carry):
        s_list = [jnp.where(sel_ref[b, h, pl.ds(j, 1), :] > 0.5, s, NEG_INF)
                  for (b, h), s in zip(chains, scores(j))]
        m_old = carry[0::3]
        m_new = [jnp.maximum(m, jnp.max(s, axis=0, keepdims=True))
                 for m, s in zip(m_old, s_list)]
        pv = attend(j, s_list, m_new)
        new = []
        for c in range(len(chains)):
            alpha = jnp.exp(m_old[c] - m_new[c])
            new += [m_new[c], alpha * carry[3 * c + 1] + pv[c][hd:hd + 1],
                    alpha * carry[3 * c + 2] + pv[c][:hd]]
        return tuple(new)

    final = lax.fori_loop(0, i, past_block, tuple(init))
    for c, (b, h) in enumerate(chains):
        o_ref[b, head_rows[h], :] = (final[3 * c + 2] / final[3 * c + 1]).astype(o_ref.dtype)


def _moba_prompt(qt, k, vt, layer, nbat=4):
    b, w, l = qt.shape
    blk = MOBA_BLOCK
    nbat = nbat if b % nbat == 0 else 1
    assert l % blk == 0
    nb = l // blk
    return pl.pallas_call(
        functools.partial(_moba_prompt_body, nbat=nbat),
        name="moba_prompt",
        grid=(b // nbat, nb),
        in_specs=[pl.BlockSpec((nbat, w, blk), lambda i, t: (i, 0, t)),
                  pl.BlockSpec((nbat * l, w), lambda i, t: (i, 0)),
                  pl.BlockSpec((1, nbat, w, l), lambda i, t: (layer, i, 0, 0))],
        out_specs=pl.BlockSpec((nbat, w, blk), lambda i, t: (i, 0, t)),
        out_shape=jax.ShapeDtypeStruct((b, w, l), BF16),
        scratch_shapes=[pltpu.VMEM((nbat, nb, w), F32), pltpu.VMEM((nbat, nb, blk, w), BF16),
                        pltpu.VMEM((nbat, nb, MOBA_HEADS, _PV_ROWS, blk), BF16),
                        pltpu.VMEM((nbat, MOBA_HEADS, nb, blk), F32)],
        compiler_params=_params("arbitrary", "arbitrary"),
    )(qt, k, vt)


PAGE_SLOTS = 3
SEQS_PER_STEP = 2


def _moba_sample_body(pt_ref, q_ref, kn_ref, vn_ref, ck_ref, cv_ref, o_ref, kbuf, vbuf, sem,
                      *, n_pages, base, n_steps, ns):
    i = pl.program_id(0)
    hd = MOBA_HEAD_DIM
    r = S_ROWS
    w = MOBA_WIDTH
    ppb = MOBA_BLOCK // PAGE_SIZE
    nb = n_pages // ppb
    ahead = PAGE_SLOTS - 1
    seqs = range(ns)
    blocks = range(nb)

    def page_copies(step, slot):
        cps = []
        for s in seqs:
            for p in range(n_pages):
                row = base + pt_ref[step * ns + s, p]
                cps.append(pltpu.make_async_copy(ck_ref.at[row], kbuf.at[slot, s, p],
                                                 sem.at[0, slot]))
                cps.append(pltpu.make_async_copy(cv_ref.at[row], vbuf.at[slot, s, p],
                                                 sem.at[1, slot]))
        return cps

    def start_all(cps):
        for n, cp in enumerate(cps):
            cp.start(priority=n % 2)

    @pl.when(i == 0)
    def _():
        for d in range(ahead):
            start_all(page_copies(d, d))

    slot = lax.rem(i, PAGE_SLOTS)
    for cp in page_copies(i, slot):
        cp.wait()

    lane = lax.broadcasted_iota(jnp.int32, (r, w), 1)
    head_masks = [jnp.where((lane >= h * hd) & (lane < (h + 1) * hd), 1.0, 0.0)
                  for h in range(MOBA_HEADS)]
    col = lax.broadcasted_iota(jnp.int32, (w, LANES), 1)
    q_bd = [jnp.concatenate([q_ref[s] * (hd ** -0.5) * hm for hm in head_masks], axis=0)
            for s in seqs]
    q_bb = [x.astype(BF16) for x in q_bd]

    kp = [[kbuf[slot, s, p] for p in range(n_pages)] for s in seqs]
    s_pages = [[_dot(q_bb[s], x.astype(BF16)) for x in kp[s]] for s in seqs]
    kmean_t = []
    for s in seqs:
        km = jnp.zeros((w, LANES), F32)
        for j in blocks:
            ksum = kp[s][j * ppb]
            for pp in range(1, ppb):
                ksum = ksum + kp[s][j * ppb + pp]
            km = jnp.where(col == j, jnp.sum(ksum, axis=1, keepdims=True) * (1.0 / MOBA_BLOCK),
                           km)
        kmean_t.append(km)
    s_blk = [[jnp.concatenate(s_pages[s][j * ppb:(j + 1) * ppb], axis=-1) for j in blocks]
             for s in seqs]
    m_cols = [[jnp.max(x, axis=1, keepdims=True) for x in s_blk[s]] for s in seqs]
    e_blk = [[jnp.exp(x - m) for x, m in zip(s_blk[s], m_cols[s])] for s in seqs]
    l_cols = [[jnp.sum(e, axis=1, keepdims=True) for e in e_blk[s]] for s in seqs]
    accs = []
    for s in seqs:
        row_accs = []
        for j in blocks:
            eb = e_blk[s][j].astype(BF16)
            acc = None
            for pp in range(ppb):
                vp = vbuf[slot, s, j * ppb + pp].astype(BF16)
                part = _dot_nt(eb[:, pp * PAGE_SIZE:(pp + 1) * PAGE_SIZE], vp)
                acc = part if acc is None else acc + part
            row_accs.append(acc)
        accs.append(row_accs)

    nxt = jnp.minimum(i + ahead, n_steps - 1)
    start_all(page_copies(nxt, lax.rem(i + ahead, PAGE_SLOTS)))

    gates = [_dot(q_bd[s], kmean_t[s], HI)[:, :nb] for s in seqs]
    sels = [_select_blocks(g, nb, axis=1) > 0.5 for g in gates]
    qt = lax.broadcasted_iota(jnp.int32, (MOBA_HEADS * r, r), 0) & (r - 1)
    kt = lax.broadcasted_iota(jnp.int32, (MOBA_HEADS * r, r), 1)
    own = (kt >= S_LO) & (kt <= qt)
    for s in seqs:
        sel = sels[s]
        m_all = jnp.concatenate(m_cols[s], axis=1)
        l_all = jnp.concatenate(l_cols[s], axis=1)
        s_own = _dot_nt(q_bb[s], kn_ref[s].astype(BF16))
        m_tot = jnp.maximum(jnp.max(jnp.where(sel, m_all, NEG_INF), axis=1, keepdims=True),
                            jnp.max(jnp.where(own, s_own, NEG_INF), axis=1, keepdims=True))
        wj = jnp.where(sel, jnp.exp(jnp.where(sel, m_all - m_tot, 0.0)), 0.0)
        p_own = jnp.where(own, jnp.exp(jnp.where(own, s_own - m_tot, 0.0)), 0.0)
        l_tot = (jnp.sum(wj * l_all, axis=1, keepdims=True)
                 + jnp.sum(p_own, axis=1, keepdims=True))
        acc = _dot(p_own.astype(BF16), vn_ref[s].astype(BF16))
        for j in blocks:
            acc = acc + wj[:, j:j + 1] * accs[s][j]
        o_bd = acc / l_tot
        o = None
        for h in range(MOBA_HEADS):
            part = o_bd[h * r:(h + 1) * r, :] * head_masks[h]
            o = part if o is None else o + part
        o_ref[s] = o.astype(o_ref.dtype)

    @pl.when(i == n_steps - 1)
    def _():
        for d in range(1, PAGE_SLOTS):
            for cp in page_copies(n_steps - 1, lax.rem(i + d, PAGE_SLOTS)):
                cp.wait()


def _moba_sample(q, k_new, v_new, cache_kt, cache_vt, page_table, base):
    b, r, w = q.shape
    page = cache_kt.shape[2]
    n_pages = page_table.shape[1]
    ns = SEQS_PER_STEP if b % SEQS_PER_STEP == 0 else 1
    n_steps = b // ns
    assert page == PAGE_SIZE and r == S_ROWS and cache_kt.shape[1] == w
    assert n_steps >= PAGE_SLOTS
    tile = pl.BlockSpec((ns, r, w), lambda i, pt: (i, 0, 0))
    hbm = pl.BlockSpec(memory_space=pl.ANY)
    grid_spec = pltpu.PrefetchScalarGridSpec(
        num_scalar_prefetch=1,
        grid=(n_steps,),
        in_specs=[tile, tile, tile, hbm, hbm],
        out_specs=tile,
        scratch_shapes=[pltpu.VMEM((PAGE_SLOTS, ns, n_pages, w, page), F32),
                        pltpu.VMEM((PAGE_SLOTS, ns, n_pages, w, page), F32),
                        pltpu.SemaphoreType.DMA((2, PAGE_SLOTS))],
    )
    body = functools.partial(_moba_sample_body, n_pages=n_pages, base=base, n_steps=n_steps,
                             ns=ns)
    return pl.pallas_call(
        body,
        name="moba_sample",
        grid_spec=grid_spec,
        out_shape=jax.ShapeDtypeStruct((b, r, w), BF16),
        compiler_params=_params("arbitrary"),
    )(page_table, q, k_new, v_new, cache_kt, cache_vt)


def _trunk(x, weights, ffn_w, norm_final, *, sample=None):
    b, l, d = x.shape
    depth = len(weights)
    x = x.reshape(b * l, d)
    gr = GROUP_ROWS
    states = []
    gdn_states = kv = None
    for layer, wts in enumerate(weights):
        x = _ffn(x, wts["n_ffn1"], *ffn_w[0], layer)
        if sample is None:
            qkv, z, sc, ba, mk, mq, mv, mkt = _inproj(
                x, wts["n_mix"], wts["w_rows"], wts["w_qv_t"], seq_len=l,
                stack=(depth, layer, kv))
            kv = (mv, mkt)
            grp = lambda a: a.reshape(b, l, a.shape[-1])
            o_a, o_b, gdn_states, gated = _mixer(
                grp(qkv), grp(z), grp(ba), grp(sc), wts["gdn_conv_w"], wts["gparams"],
                wts["gdn_norm_w"], wts["sc_conv_w"], c=GDN_CHUNK, ng=8 if b % 8 == 0 else 1,
                layer=layer, depth=depth, prev_states=gdn_states)
            o_c = _moba_prompt(mq, mk, mv, layer)
        else:
            qkv, z, sc, ba, mk, mq, mv = _inproj(x, wts["n_mix"], wts["w_rows"], wts["w_qv"])
            ngroups = b * l // gr
            grp = lambda a: a.reshape(ngroups, gr, a.shape[-1])
            state = (grp(sample["gdn_conv"][layer]), grp(sample["sconv"][layer]), sample["gdn"])
            o_a, o_b, gdn_states, gated = _mixer(
                grp(qkv), grp(z), grp(ba), grp(sc), wts["gdn_conv_w"], wts["gparams"],
                wts["gdn_norm_w"], wts["sc_conv_w"], c=l, ng=2 if ngroups % 2 == 0 else 1,
                layer=layer, depth=depth, prev_states=gdn_states, state=state,
                row_lo=S_LO, row_hi=S_HI)
            seq = lambda a: a.reshape(b, l, a.shape[-1])
            o_c = _moba_sample(seq(mq), seq(mk), seq(mv), sample["cache_kt"], sample["cache_vt"],
                               sample["page_table"], layer * sample["n_pool"])
            o_c = o_c.reshape(b * l, -1)
        mix = (o_a.reshape(b * l, -1), o_b.reshape(b * l, -1), o_c, wts["w_out"])
        x = _ffn(x, wts["n_ffn2"], *ffn_w[1], layer, mix=mix,
                 final_w=norm_final if layer == depth - 1 else None)
        states.append(dict(qkv=qkv.reshape(b, l, -1), gated=gated.reshape(b, l, -1)))
        if sample is not None:
            states[-1].update(k=mk.reshape(b, l, -1), v=mv)
    return x.reshape(b, l, d), states, gdn_states, kv


def kernel(x_prompt, x_sample, state_gdn, state_gdn_conv, state_sconv, cache_k, cache_v,
           page_table, norm_ffn1, ffn1_w_gate, ffn1_w_up, ffn1_w_down, norm_mix, w_in,
           gdn_conv_w, gdn_a_log, gdn_dt_bias, gdn_norm_w, sc_conv_w, w_out,
           norm_ffn2, ffn2_w_gate, ffn2_w_up, ffn2_w_down, norm_final):
    depth = w_in.shape[0]
    gw, sw, mw = GDN_WIDTH, SC_WIDTH, MOBA_WIDTH
    nh = GDN_HEADS
    mh, md = MOBA_HEADS, MOBA_HEAD_DIM

    weights = []
    for la in range(depth):
        w = w_in[la]
        ba_off = 4 * gw
        sc_off = ba_off + 2 * nh
        mo_off = sc_off + 3 * sw
        w_rows = jnp.concatenate(
            [w[:, :ba_off], w[:, sc_off:mo_off],
             jnp.pad(w[:, ba_off:sc_off], ((0, 0), (0, LANES - 2 * nh))),
             w[:, mo_off + mw:mo_off + 2 * mw]], axis=1).astype(BF16)
        w_qv = jnp.concatenate([w[:, mo_off:mo_off + mw], w[:, mo_off + 2 * mw:mo_off + 3 * mw]],
                               axis=1).astype(BF16)
        gparams = jnp.zeros((SUBLANES, LANES), F32)
        gparams = gparams.at[0, nh:2 * nh].set(gdn_a_log[la].astype(F32))
        gparams = gparams.at[1, nh:2 * nh].set(gdn_dt_bias[la].astype(F32))
        weights.append(dict(
            n_ffn1=norm_ffn1[la],
            n_mix=norm_mix[la], w_rows=w_rows, w_qv=w_qv, w_qv_t=w_qv.T,
            gdn_conv_w=gdn_conv_w[la], gparams=gparams, gdn_norm_w=gdn_norm_w[la],
            sc_conv_w=sc_conv_w[la], w_out=_to_bf16(w_out, la),
            n_ffn2=norm_ffn2[la]))
    ffn_w = ((ffn1_w_gate, ffn1_w_up, ffn1_w_down), (ffn2_w_gate, ffn2_w_up, ffn2_w_down))

    pb, pl_, _ = x_prompt.shape
    y_p, st_p, p_gdn, (p_v, p_k) = _trunk(x_prompt, weights, ffn_w, norm_final)
    hist = GDN_CONV - 1
    p_conv = jnp.stack([s["qkv"][:, -hist:] for s in st_p])
    p_sconv = jnp.stack([s["gated"][:, -(SC_CONV - 1):] for s in st_p])
    p_k = p_k.reshape(depth, pb, mh, md, pl_).transpose(0, 1, 4, 2, 3)
    p_v = p_v.reshape(depth, pb, mh, md, pl_).transpose(0, 1, 4, 2, 3)

    db, dl, d = x_sample.shape
    assert dl == S_HI - S_LO and (db * S_ROWS) % GROUP_ROWS == 0
    n_pool = cache_k.shape[1]
    tile_pad = lambda a, lo: jnp.pad(a, ((0, 0),) * (a.ndim - 2)
                                     + ((lo, S_ROWS - lo - a.shape[-2]), (0, 0)))
    seqs_per_group = GROUP_ROWS // S_ROWS
    paged_t = lambda c: c.transpose(0, 1, 3, 4, 2).reshape(depth * n_pool, mw, PAGE_SIZE)
    sample = dict(
        gdn=state_gdn.reshape(depth, db // seqs_per_group, seqs_per_group * nh,
                              GDN_HEAD_DIM, GDN_HEAD_DIM),
        gdn_conv=tile_pad(state_gdn_conv, 0),
        sconv=tile_pad(state_sconv, S_LO - (SC_CONV - 1)),
        cache_kt=paged_t(cache_k), cache_vt=paged_t(cache_v), n_pool=n_pool,
        page_table=page_table)
    y_s, st_s, s_gdn, _ = _trunk(tile_pad(x_sample, S_LO), weights, ffn_w, norm_final,
                                 sample=sample)
    y_s = y_s[:, S_LO:S_HI]
    s_gdn = s_gdn.reshape(state_gdn.shape)
    s_conv = jnp.stack([s["qkv"][:, S_HI - hist:S_HI] for s in st_s])
    s_sconv = jnp.stack([s["gated"][:, S_HI - (SC_CONV - 1):S_HI] for s in st_s])
    s_k = jnp.stack([s["k"][:, S_LO:S_HI] for s in st_s]).reshape(depth, db, dl, mh, md)
    s_v = jnp.stack([s["v"].reshape(db, S_ROWS, mw)[:, S_LO:S_HI] for s in st_s]).reshape(
        depth, db, dl, mh, md)
    return (y_p, y_s, p_gdn, p_conv, p_sconv, p_k, p_v, s_gdn, s_conv, s_sconv, s_k, s_v)
```
